```python
import math
import jax, jax.numpy as jnp
from jax import lax
import numpy as np

D_MODEL = 1024
BATCH = 8
SEQ = 2048
DEPTH = 2

MIX_W = D_MODEL // 2
N_BRANCH = 3
HG_KEY = 128
HG_HEADS = MIX_W // HG_KEY
HG_VAL = MIX_W // HG_HEADS
HG_CHUNK = 64
FOX_HEAD_DIM = 64
FOX_HEADS = MIX_W // FOX_HEAD_DIM
FOX_BLOCK = 128
S5_GROUP_CH = 16
S5_GROUPS = MIX_W // S5_GROUP_CH
S5_STATE = 64
S5_DT_MIN = 1e-3
S5_DT_MAX = 1e-1
FFN_HIDDEN = -(-8 * D_MODEL // (3 * 256)) * 256
DEEPNORM_ALPHA = (2 * DEPTH) ** 0.25
DEEPNORM_BETA = (8 * DEPTH) ** -0.25
LN_EPS = 1e-5
RMS_EPS = 1e-6
IN_WIDTHS = (HG_HEADS * HG_KEY, HG_HEADS * HG_KEY, MIX_W, MIX_W,
             MIX_W, MIX_W, MIX_W, FOX_HEADS,
             MIX_W,
             N_BRANCH * D_MODEL)
IN_TOTAL = sum(IN_WIDTHS)

kernel_name = 'hybrid_hgrn2_fox_s5_deepnorm'


def _split_points():
    return [int(v) for v in np.cumsum(IN_WIDTHS)[:-1]]


def layer_norm(x, g, b):
    xf = x.astype(jnp.float32)
    mu = jnp.mean(xf, axis=-1, keepdims=True)
    var = jnp.mean(jnp.square(xf - mu), axis=-1, keepdims=True)
    return ((xf - mu) * lax.rsqrt(var + LN_EPS) * g + b).astype(x.dtype)


def hgrn2_branch(q, f_logit, i, gate, lb, norm_w):
    f32 = jnp.float32
    bsz, seq_len, _ = q.shape
    n_chunks = seq_len // HG_CHUNK
    f = lb + (1.0 - lb) * jax.nn.sigmoid(f_logit.astype(f32))
    k = 1.0 - f

    def chunks(t, width):
        t = t.astype(f32).reshape(bsz, n_chunks, HG_CHUNK, HG_HEADS, width)
        return t.transpose(1, 0, 3, 2, 4)

    qc = chunks(q, HG_KEY)
    kc = chunks(k, HG_KEY)
    vc = chunks(i, HG_VAL)
    gc = jnp.cumsum(chunks(jnp.log(f), HG_KEY), axis=3)
    causal = jnp.tril(jnp.ones((HG_CHUNK, HG_CHUNK), dtype=bool))[:, :, None]

    def step(state, xs):
        qb, kb, vb, gb = xs
        g_last = gb[:, :, -1, :]
        o_inter = jnp.einsum('bhtk,bhkv->bhtv', qb * jnp.exp(gb), state)
        rel = gb[:, :, :, None, :] - gb[:, :, None, :, :]
        decay = jnp.exp(jnp.where(causal, rel, -jnp.inf))
        scores = jnp.einsum('bhtsk,bhsk->bhts', qb[:, :, :, None, :] * decay, kb)
        o_intra = jnp.einsum('bhts,bhsv->bhtv', scores, vb)
        k_to_end = kb * jnp.exp(g_last[:, :, None, :] - gb)
        new_state = (jnp.exp(g_last)[..., None] * state
                     + jnp.einsum('bhsk,bhsv->bhkv', k_to_end, vb))
        return new_state, o_inter + o_intra

    state0 = jnp.zeros((bsz, HG_HEADS, HG_KEY, HG_VAL), f32)
    _, o = lax.scan(step, state0, (qc, kc, vc, gc))
    o = o.transpose(1, 0, 3, 2, 4).reshape(bsz, seq_len, HG_HEADS, HG_VAL)
    o = o * lax.rsqrt(jnp.mean(jnp.square(o), axis=-1, keepdims=True) + RMS_EPS) * norm_w.astype(f32)
    g = gate.astype(f32).reshape(bsz, seq_len, HG_HEADS, HG_VAL)
    return (o * jax.nn.silu(g)).reshape(bsz, seq_len, MIX_W)


def fox_branch(q, k, v, f_logit):
    f32 = jnp.float32
    bsz, seq_len, _ = q.shape
    shp = (bsz, seq_len, FOX_HEADS, FOX_HEAD_DIM)
    q = q.reshape(shp)
    k = k.reshape(shp)
    v = v.reshape(shp)
    cum = jnp.cumsum(jax.nn.log_sigmoid(f_logit.astype(f32)), axis=1).transpose(0, 2, 1)
    scale = FOX_HEAD_DIM ** -0.5
    outs = []
    for blk in range(seq_len // FOX_BLOCK):
        t0 = blk * FOX_BLOCK
        t1 = t0 + FOX_BLOCK
        logits = jnp.einsum('bthd,bshd->bhts', q[:, t0:t1], k[:, :t1]).astype(f32) * scale
        logits = logits + cum[:, :, t0:t1, None] - cum[:, :, None, :t1]
        mask = (t0 + jnp.arange(FOX_BLOCK))[:, None] >= jnp.arange(t1)[None, :]
        probs = jax.nn.softmax(jnp.where(mask, logits, -jnp.inf), axis=-1)
        outs.append(jnp.einsum('bhts,bshd->bthd', probs.astype(v.dtype), v[:, :t1]))
    return jnp.concatenate(outs, axis=1).reshape(bsz, seq_len, MIX_W)


def s5_branch(u, lam_re, lam_im, log_step, b_re, b_im, c_re, c_im, d, w_glu):
    f32 = jnp.float32
    bsz, seq_len, _ = u.shape
    uc = u.astype(f32).reshape(bsz, seq_len, S5_GROUPS, S5_GROUP_CH)
    lam = lax.complex(lam_re.astype(f32), lam_im.astype(f32))
    dt = jnp.exp(log_step.astype(f32))[:, None]
    lam_bar = jnp.exp(lam * dt)
    b_bar = ((lam_bar - 1.0) / lam)[:, :, None] * lax.complex(b_re.astype(f32), b_im.astype(f32))
    c_mat = lax.complex(c_re.astype(f32), c_im.astype(f32))
    bu = jnp.einsum('gph,blgh->lbgp', b_bar, uc.astype(jnp.complex64))
    a = jnp.broadcast_to(lam_bar, (seq_len, 1, S5_GROUPS, S5_STATE))

    def combine(e1, e2):
        a1, s1 = e1
        a2, s2 = e2
        return a1 * a2, a2 * s1 + s2

    _, states = lax.associative_scan(combine, (a, bu), axis=0)
    y = jnp.einsum('ghp,lbgp->blgh', c_mat, states).real + d.astype(f32) * uc
    y = jax.nn.gelu(y.reshape(bsz, seq_len, MIX_W))
    return y * jax.nn.sigmoid(y @ w_glu.astype(f32))


def mixer_sublayer(h, w_in, lb, hg_norm_w, fox_b_f, lam_re, lam_im, log_step, b_re, b_im,
                   c_re, c_im, s5_d, s5_w_glu, w_br_a, w_br_b, w_br_c, w_out):
    dt = h.dtype
    bsz, seq_len, _ = h.shape
    proj = jnp.einsum('bld,de->ble', h, w_in)
    (hg_q, hg_f, hg_i, hg_g, fx_q, fx_k, fx_v, fx_f, s5_u, gate_logits) = jnp.split(
        proj, _split_points(), axis=-1)
    y_a = hgrn2_branch(hg_q, hg_f, hg_i, hg_g, lb, hg_norm_w).astype(dt)
    y_b = fox_branch(fx_q, fx_k, fx_v, fx_f + fox_b_f).astype(dt)
    y_c = s5_branch(s5_u, lam_re, lam_im, log_step, b_re, b_im, c_re, c_im, s5_d, s5_w_glu).astype(dt)
    gates = jax.nn.sigmoid(gate_logits.astype(jnp.float32)).astype(dt).reshape(
        bsz, seq_len, N_BRANCH, D_MODEL)
    merged = (gates[:, :, 0] * (y_a @ w_br_a)
              + gates[:, :, 1] * (y_b @ w_br_b)
              + gates[:, :, 2] * (y_c @ w_br_c))
    return merged @ w_out


def swiglu_ffn(h, w_gate, w_up, w_down):
    return (jax.nn.silu(h @ w_gate) * (h @ w_up)) @ w_down


def _normal(k, shape, scale):
    return jax.random.normal(k, shape, jnp.float32) * scale


def setup_inputs(seed: int = 0) -> dict:
    key = jax.random.key(seed)
    ks = jax.random.split(key, 25)
    G, P, Hc = S5_GROUPS, S5_STATE, S5_GROUP_CH
    return {
        'x': _normal(ks[0], (BATCH, SEQ, D_MODEL), 1.0),
        'w_in': _normal(ks[1], (DEPTH, D_MODEL, IN_TOTAL), D_MODEL ** -0.5),
        'hg_lower_bounds': _normal(ks[2], (DEPTH, HG_HEADS * HG_KEY), 0.1),
        'hg_norm_w': 1.0 + _normal(ks[3], (DEPTH, HG_VAL), 0.02),
        'fox_b_f': _normal(ks[4], (DEPTH, FOX_HEADS), 0.1),
        's5_lambda_re': -0.5 + _normal(ks[5], (DEPTH, G, P), 0.01),
        's5_lambda_im': jnp.pi * jnp.arange(P, dtype=jnp.float32) + _normal(ks[6], (DEPTH, G, P), 0.01),
        's5_log_step': jax.random.uniform(ks[7], (DEPTH, G), jnp.float32,
                                          math.log(S5_DT_MIN), math.log(S5_DT_MAX)),
        's5_b_re': _normal(ks[8], (DEPTH, G, P, Hc), (2.0 * Hc) ** -0.5),
        's5_b_im': _normal(ks[9], (DEPTH, G, P, Hc), (2.0 * Hc) ** -0.5),
        's5_c_re': _normal(ks[10], (DEPTH, G, Hc, P), P ** -0.5),
        's5_c_im': _normal(ks[11], (DEPTH, G, Hc, P), P ** -0.5),
        's5_d': _normal(ks[12], (DEPTH, G, Hc), 1.0),
        's5_w_glu': _normal(ks[13], (DEPTH, MIX_W, MIX_W), MIX_W ** -0.5),
        'w_br_a': _normal(ks[14], (DEPTH, MIX_W, D_MODEL), MIX_W ** -0.5),
        'w_br_b': _normal(ks[15], (DEPTH, MIX_W, D_MODEL), MIX_W ** -0.5),
        'w_br_c': _normal(ks[16], (DEPTH, MIX_W, D_MODEL), MIX_W ** -0.5),
        'w_out': _normal(ks[17], (DEPTH, D_MODEL, D_MODEL), D_MODEL ** -0.5 * DEEPNORM_BETA),
        'ln1_g': 1.0 + _normal(ks[18], (DEPTH, D_MODEL), 0.02),
        'ln1_b': _normal(ks[19], (DEPTH, D_MODEL), 0.02),
        'w_ffn_gate': _normal(ks[20], (DEPTH, D_MODEL, FFN_HIDDEN), D_MODEL ** -0.5),
        'w_ffn_up': _normal(ks[21], (DEPTH, D_MODEL, FFN_HIDDEN), D_MODEL ** -0.5),
        'w_ffn_down': _normal(ks[22], (DEPTH, FFN_HIDDEN, D_MODEL), FFN_HIDDEN ** -0.5 * DEEPNORM_BETA),
        'ln2_g': 1.0 + _normal(ks[23], (DEPTH, D_MODEL), 0.02),
        'ln2_b': _normal(ks[24], (DEPTH, D_MODEL), 0.02),
    }


def reference(x, w_in, hg_lower_bounds, hg_norm_w, fox_b_f, s5_lambda_re, s5_lambda_im, s5_log_step,
              s5_b_re, s5_b_im, s5_c_re, s5_c_im, s5_d, s5_w_glu, w_br_a, w_br_b, w_br_c, w_out,
              ln1_g, ln1_b, w_ffn_gate, w_ffn_up, w_ffn_down, ln2_g, ln2_b):
    sm = jax.nn.softmax(hg_lower_bounds.astype(jnp.float32), axis=0)
    lower_bounds = jnp.cumsum(sm, axis=0) - sm[0:1]
    h = x
    for l in range(DEPTH):
        mix = mixer_sublayer(h, w_in[l], lower_bounds[l], hg_norm_w[l], fox_b_f[l],
                             s5_lambda_re[l], s5_lambda_im[l], s5_log_step[l], s5_b_re[l], s5_b_im[l],
                             s5_c_re[l], s5_c_im[l], s5_d[l], s5_w_glu[l],
                             w_br_a[l], w_br_b[l], w_br_c[l], w_out[l])
        h = layer_norm(DEEPNORM_ALPHA * h + mix, ln1_g[l], ln1_b[l])
        ffn = swiglu_ffn(h, w_ffn_gate[l], w_ffn_up[l], w_ffn_down[l])
        h = layer_norm(DEEPNORM_ALPHA * h + ffn, ln2_g[l], ln2_b[l])
    return h
```

```python
import functools
import math

import jax
import jax.numpy as jnp
from jax import lax
from jax.experimental import pallas as pl
from jax.experimental.pallas import tpu as pltpu

F32 = jnp.float32
BF16 = jnp.bfloat16

D_MODEL = 1024
MIX_W = 512
HG_HEADS = 4
HG_KEY = 128
HG_CHUNK = 64
HG_REF_ROW = HG_CHUNK // 2 - 1
FOX_HEADS = 8
FOX_HEAD_DIM = 64
S5_GROUPS = 32
S5_STATE = 64
S5_GROUP_CH = 16
S5_NSTATE = S5_GROUPS * S5_STATE
FFN_HIDDEN = 2816
LN_EPS = 1e-5
RMS_EPS = 1e-6
EXP_CLAMP = 80.0

LANES = 128
MXU_DIM = 256
VMEM_LIMIT = 56 * 1024 * 1024

PROJ_ROWS = 512
HG_ROWS = 256
FOX_Q = 256
FOX_KV = 256
S5_STEPS = 32
MERGE_ROWS = 512
FFN_ROWS = 512
FFN_CHUNK = 1408


def _dot(a, b):
    return jnp.dot(a, b, preferred_element_type=F32)


def _dot_nt(a, b):
    return lax.dot_general(a, b, (((1,), (1,)), ((), ())), preferred_element_type=F32)


def _dot_tn(a, b):
    return lax.dot_general(a, b, (((0,), (0,)), ((), ())), preferred_element_type=F32)


def _sigmoid(x):
    return 1.0 / (1.0 + jnp.exp(-x))


def _layer_norm(z, g, b):
    mu = jnp.mean(z, axis=-1, keepdims=True)
    zc = z - mu
    var = jnp.mean(zc * zc, axis=-1, keepdims=True)
    return zc * lax.rsqrt(var + LN_EPS) * g + b


def _cumsum(x, axis):
    n = x.shape[axis]
    idx = lax.broadcasted_iota(jnp.int32, x.shape, axis)
    d = 1
    while d < n:
        x = x + jnp.where(idx >= d, pltpu.roll(x, d, axis=axis), 0.0)
        d *= 2
    return x


def _params(sem):
    return pltpu.CompilerParams(dimension_semantics=sem, vmem_limit_bytes=VMEM_LIMIT)


def _proj_kernel(x_ref, whg_ref, wfx_ref, wff_ref, ws5_ref, hg_ref, fx_ref, ff_ref, s5_ref):
    xb = x_ref[...].astype(BF16)
    hg_ref[...] = _dot(xb, whg_ref[...])
    fx_ref[...] = _dot(xb, wfx_ref[...]).astype(BF16)
    ff_ref[...] = _dot(xb, wff_ref[...])
    s5_ref[...] = _dot(xb, ws5_ref[...])


def _in_proj(h, whg, wfx, wff, ws5):
    bsz, seq, _ = h.shape
    tm = PROJ_ROWS
    const = lambda b, i: (0, 0)
    return pl.pallas_call(
        _proj_kernel,
        grid=(bsz, seq // tm),
        in_specs=[
            pl.BlockSpec((None, tm, D_MODEL), lambda b, i: (b, i, 0)),
            pl.BlockSpec(whg.shape, const),
            pl.BlockSpec(wfx.shape, const),
            pl.BlockSpec(wff.shape, const),
            pl.BlockSpec(ws5.shape, const),
        ],
        out_specs=[
            pl.BlockSpec((None, tm, 4 * MIX_W), lambda b, i: (b, i, 0)),
            pl.BlockSpec((None, tm, 3 * MIX_W), lambda b, i: (b, i, 0)),
            pl.BlockSpec((None, tm, LANES), lambda b, i: (b, i, 0)),
            pl.BlockSpec((tm, MIX_W), lambda b, i: (i, b)),
        ],
        out_shape=[
            jax.ShapeDtypeStruct((bsz, seq, 4 * MIX_W), F32),
            jax.ShapeDtypeStruct((bsz, seq, 3 * MIX_W), BF16),
            jax.ShapeDtypeStruct((bsz, seq, LANES), F32),
            jax.ShapeDtypeStruct((seq, bsz * MIX_W), F32),
        ],
        compiler_params=_params(("arbitrary", "arbitrary")),
        name="in_proj",
    )(h, whg, wfx, wff, ws5)


def _hgrn2_kernel(layer, hg_ref, lbp_ref, nw_ref, o_ref, st_ref):
    @pl.when(pl.program_id(1) == 0)
    def _():
        st_ref[...] = jnp.zeros_like(st_ref)

    lbp = lbp_ref[...]
    e = jnp.exp(lbp - jnp.max(lbp, axis=0, keepdims=True))
    sm = e / jnp.sum(e, axis=0, keepdims=True)
    cum = sm[0:1]
    for j in range(1, layer + 1):
        cum = cum + sm[j:j + 1]
    lb = cum - sm[0:1]

    nw = nw_ref[...]
    c = HG_CHUNK
    tri = (lax.broadcasted_iota(jnp.int32, (c, c), 0) >= lax.broadcasted_iota(jnp.int32, (c, c), 1))

    def chunk(ci, carry):
        r0 = pl.multiple_of(ci * c, c)
        rows = pl.ds(r0, c)
        q = hg_ref[rows, 0:MIX_W]
        fz = hg_ref[rows, MIX_W:2 * MIX_W]
        f = lb + (1.0 - lb) * _sigmoid(fz)
        k = 1.0 - f
        g = _cumsum(jnp.log(f), 0)
        g_ref = g[HG_REF_ROW:HG_REF_ROW + 1]
        g_last = g[c - 1:c]
        q_rel = (q * jnp.exp(jnp.minimum(g - g_ref, EXP_CLAMP))).astype(BF16)
        k_rel = (k * jnp.exp(jnp.minimum(g_ref - g, EXP_CLAMP))).astype(BF16)
        q_in = (q * jnp.exp(g)).astype(BF16)
        k_end = (k * jnp.exp(g_last - g)).astype(BF16)
        s_decay = jnp.exp(g_last)
        v = hg_ref[rows, 2 * MIX_W:3 * MIX_W].astype(BF16)
        gate = hg_ref[rows, 3 * MIX_W:4 * MIX_W]
        for h in range(HG_HEADS):
            sl = slice(h * HG_KEY, (h + 1) * HG_KEY)
            st = st_ref[h]
            scores = jnp.where(tri, _dot_nt(q_rel[:, sl], k_rel[:, sl]), 0.0)
            o = _dot(scores.astype(BF16), v[:, sl]) + _dot_nt(q_in[:, sl], st.astype(BF16))
            st_ref[h] = st * s_decay[:, sl] + _dot_tn(v[:, sl], k_end[:, sl])
            o = o * lax.rsqrt(jnp.mean(o * o, axis=-1, keepdims=True) + RMS_EPS) * nw
            gh = gate[:, sl]
            o_ref[rows, sl] = (o * (gh * _sigmoid(gh))).astype(BF16)
        return carry

    lax.fori_loop(0, hg_ref.shape[0] // c, chunk, 0)


def _hgrn2(hg, lbp, nw, layer):
    bsz, seq, _ = hg.shape
    tr = HG_ROWS
    return pl.pallas_call(
        functools.partial(_hgrn2_kernel, layer),
        grid=(bsz, seq // tr),
        in_specs=[
            pl.BlockSpec((None, tr, 4 * MIX_W), lambda b, i: (b, i, 0)),
            pl.BlockSpec(lbp.shape, lambda b, i: (0, 0)),
            pl.BlockSpec(nw.shape, lambda b, i: (0, 0)),
        ],
        out_specs=pl.BlockSpec((None, tr, MIX_W), lambda b, i: (b, i, 0)),
        out_shape=jax.ShapeDtypeStruct((bsz, seq, MIX_W), BF16),
        scratch_shapes=[pltpu.VMEM((HG_HEADS, HG_KEY, HG_KEY), F32)],
        compiler_params=_params(("arbitrary", "arbitrary")),
        name="hgrn2",
    )(hg, lbp, nw)


def _fox_cum_kernel(fz_ref, bf_ref, o_ref):
    x = fz_ref[...] + bf_ref[...]
    logsig = jnp.minimum(x, 0.0) - jnp.log(1.0 + jnp.exp(-jnp.abs(x)))
    o_ref[...] = _cumsum(logsig, 1)


def _fox_cum(fz_t, bf):
    bsz, nh, seq = fz_t.shape
    return pl.pallas_call(
        _fox_cum_kernel,
        grid=(bsz,),
        in_specs=[pl.BlockSpec((None, nh, seq), lambda b: (b, 0, 0)),
                  pl.BlockSpec(bf.shape, lambda b: (0, 0))],
        out_specs=pl.BlockSpec((None, nh, seq), lambda b: (b, 0, 0)),
        out_shape=jax.ShapeDtypeStruct((bsz, nh, seq), F32),
        compiler_params=_params(("arbitrary",)),
        name="fox_cum",
    )(fz_t, bf)


def _fox_kernel(q_ref, k_ref, v_ref, cum_ref, o_ref):
    qi = pl.program_id(1)
    tq, tk = FOX_Q, FOX_KV
    lane = lax.broadcasted_iota(jnp.int32, (1, LANES), 1)
    causal = (lax.broadcasted_iota(jnp.int32, (tq, tk), 0) >= lax.broadcasted_iota(jnp.int32, (tq, tk), 1))
    scale = FOX_HEAD_DIM ** -0.5
    for pair in range(FOX_HEADS // 2):
        sl = slice(pair * LANES, (pair + 1) * LANES)
        q_pair = q_ref[:, sl]
        outs = []
        for half in range(2):
            h = 2 * pair + half
            own = (lane < FOX_HEAD_DIM) if half == 0 else (lane >= FOX_HEAD_DIM)
            qh = jnp.where(own, q_pair.astype(F32) * scale, 0.0).astype(BF16)
            c_diag = cum_ref[qi][h:h + 1, :]
            c0 = c_diag[:, 0:1]

            def block(kb, c_row, masked):
                rows = pl.ds(pl.multiple_of(kb * tk, tk), tk)
                s = _dot_nt(qh, k_ref[rows, sl]) + (c0 - c_row)
                if masked:
                    s = jnp.where(causal, s, -1e30)
                return s, v_ref[rows, sl]

            s, vb = block(qi, c_diag, True)
            m = jnp.max(s, axis=-1, keepdims=True)
            p = jnp.exp(s - m)
            l = jnp.sum(p, axis=-1, keepdims=True)
            acc = _dot(p.astype(BF16), vb)

            def body(kb, carry):
                m, l, acc = carry
                s, vb = block(kb, cum_ref[kb][h:h + 1, :], False)
                m_new = jnp.maximum(m, jnp.max(s, axis=-1, keepdims=True))
                alpha = jnp.exp(m - m_new)
                p = jnp.exp(s - m_new)
                l = alpha * l + jnp.sum(p, axis=-1, keepdims=True)
                acc = alpha * acc + _dot(p.astype(BF16), vb)
                return m_new, l, acc

            m, l, acc = lax.fori_loop(0, qi, body, (m, l, acc))
            outs.append(acc / l)
        o_ref[:, sl] = jnp.where(lane < FOX_HEAD_DIM, outs[0], outs[1]).astype(BF16)


def _fox(fx, cum):
    bsz, seq, _ = fx.shape
    tq = FOX_Q
    return pl.pallas_call(
        _fox_kernel,
        grid=(bsz, seq // tq),
        in_specs=[
            pl.BlockSpec((None, tq, MIX_W), lambda b, i: (b, i, 0)),
            pl.BlockSpec((None, seq, MIX_W), lambda b, i: (b, 0, 1)),
            pl.BlockSpec((None, seq, MIX_W), lambda b, i: (b, 0, 2)),
            pl.BlockSpec((None,) + cum.shape[1:], lambda b, i: (b, 0, 0, 0)),
        ],
        out_specs=pl.BlockSpec((None, tq, MIX_W), lambda b, i: (b, i, 0)),
        out_shape=jax.ShapeDtypeStruct((bsz, seq, MIX_W), BF16),
        compiler_params=_params(("arbitrary", "arbitrary")),
        name="fox_attn",
    )(fx, fx, fx, cum)


def _s5_prep_kernel(lre_ref, lim_ref, ls_ref, bre_ref, bim_ref, are_ref, aim_ref, bbr_ref, bbi_ref):
    lr, li = lre_ref[...], lim_ref[...]
    dt = jnp.exp(ls_ref[...])
    mag = jnp.exp(lr * dt)
    ar = mag * jnp.cos(li * dt)
    ai = mag * jnp.sin(li * dt)
    den = lr * lr + li * li
    cr = ((ar - 1.0) * lr + ai * li) / den
    ci = (ai * lr - (ar - 1.0) * li) / den
    br, bi = bre_ref[...], bim_ref[...]
    bbr_ref[...] = cr * br - ci * bi
    bbi_ref[...] = cr * bi + ci * br
    are_ref[...] = jnp.broadcast_to(ar, are_ref.shape)
    aim_ref[...] = jnp.broadcast_to(ai, aim_ref.shape)


def _s5_prep(lam_re, lam_im, log_step, b_re, b_im):
    row = lambda a: a.reshape(1, S5_NSTATE)
    ls = jnp.repeat(log_step, S5_STATE).reshape(1, S5_NSTATE)
    to_hp = lambda b: b.transpose(2, 0, 1).reshape(S5_GROUP_CH, S5_NSTATE)
    vec = jax.ShapeDtypeStruct((8, S5_NSTATE), F32)
    mat = jax.ShapeDtypeStruct((S5_GROUP_CH, S5_NSTATE), F32)
    return pl.pallas_call(
        _s5_prep_kernel, out_shape=[vec, vec, mat, mat], name="s5_prep",
    )(row(lam_re), row(lam_im), ls, to_hp(b_re), to_hp(b_im))


def _s5_kernel(u_ref, are_ref, aim_ref, wb_ref, wc_ref, d_ref, wglu_ref, o_ref, x_ref, st_ref):
    @pl.when(pl.program_id(0) == 0)
    def _():
        st_ref[...] = jnp.zeros_like(st_ref)

    n = S5_NSTATE
    u = u_ref[...]
    ub = u.astype(BF16)
    n_tiles = 2 * n // MXU_DIM
    for j in range(n_tiles):
        kh = ((j % (n_tiles // 2)) * MXU_DIM // S5_STATE * S5_GROUP_CH) // MXU_DIM
        x_ref[:, j * MXU_DIM:(j + 1) * MXU_DIM] = _dot(ub[:, kh * MXU_DIM:(kh + 1) * MXU_DIM], wb_ref[j])

    steps = u_ref.shape[0] // 8
    width = 1024
    for c0 in range(0, n, width):
        re_sl = slice(c0, c0 + width)
        im_sl = slice(n + c0, n + c0 + width)
        ar = are_ref[:, re_sl]
        ai = aim_ref[:, re_sl]

        def step(t, carry):
            xr, xi = carry
            rows = pl.ds(pl.multiple_of(t * 8, 8), 8)
            nr = ar * xr - ai * xi + x_ref[rows, re_sl]
            ni = ar * xi + ai * xr + x_ref[rows, im_sl]
            x_ref[rows, re_sl] = nr
            x_ref[rows, im_sl] = ni
            return nr, ni

        xr, xi = lax.fori_loop(0, steps, step, (st_ref[:, re_sl], st_ref[:, im_sl]), unroll=4)
        st_ref[:, re_sl] = xr
        st_ref[:, im_sl] = xi

    halves = []
    per_half = n // MXU_DIM // 2
    for nh in range(MIX_W // MXU_DIM):
        acc_re = None
        acc_im = None
        for kk in range(per_half):
            kr = nh * per_half + kk
            ki = n // MXU_DIM + kr
            pr = _dot(x_ref[:, kr * MXU_DIM:(kr + 1) * MXU_DIM].astype(BF16), wc_ref[kr])
            pi = _dot(x_ref[:, ki * MXU_DIM:(ki + 1) * MXU_DIM].astype(BF16), wc_ref[ki])
            acc_re = pr if acc_re is None else acc_re + pr
            acc_im = pi if acc_im is None else acc_im + pi
        halves.append(acc_re - acc_im)
    y = jnp.concatenate(halves, axis=1) + d_ref[...] * u
    y = 0.5 * y * (1.0 + jnp.tanh(math.sqrt(2.0 / math.pi) * (y + 0.044715 * (y * y * y))))
    o_ref[...] = (y * _sigmoid(_dot(y.astype(BF16), wglu_ref[...]))).astype(BF16)


def _s5(u_tm, a_re, a_im, wb, wc, d, wglu):
    rows = S5_STEPS * 8
    total = u_tm.shape[0]
    const2 = lambda i: (0, 0)
    const3 = lambda i: (0, 0, 0)
    return pl.pallas_call(
        _s5_kernel,
        grid=(total // rows,),
        in_specs=[
            pl.BlockSpec((rows, MIX_W), lambda i: (i, 0)),
            pl.BlockSpec(a_re.shape, const2),
            pl.BlockSpec(a_im.shape, const2),
            pl.BlockSpec(wb.shape, const3),
            pl.BlockSpec(wc.shape, const3),
            pl.BlockSpec(d.shape, const2),
            pl.BlockSpec(wglu.shape, const2),
        ],
        out_specs=pl.BlockSpec((rows, MIX_W), lambda i: (i, 0)),
        out_shape=jax.ShapeDtypeStruct((total, MIX_W), BF16),
        scratch_shapes=[pltpu.VMEM((rows, 2 * S5_NSTATE), F32), pltpu.VMEM((8, 2 * S5_NSTATE), F32)],
        compiler_params=_params(("arbitrary",)),
        name="s5",
    )(u_tm, a_re, a_im, wb, wc, d, wglu)


def _s5_weights(bbar_re, bbar_im, c_re, c_im):
    eye = jnp.eye(S5_GROUPS, dtype=F32)

    def b_dense(bb):
        bb = bb.reshape(S5_GROUP_CH, S5_GROUPS, S5_STATE)
        return jnp.einsum('hgp,gk->ghkp', bb, eye).reshape(MIX_W, S5_NSTATE)

    def c_dense(cc):
        return jnp.einsum('ghp,gk->kpgh', cc, eye).reshape(S5_NSTATE, MIX_W)

    wb_dense = jnp.concatenate([b_dense(bbar_re), b_dense(bbar_im)], axis=1)
    wc_dense = jnp.concatenate([c_dense(c_re), c_dense(c_im)], axis=0)
    n_tiles = 2 * S5_NSTATE // MXU_DIM
    half = n_tiles // 2
    ch_per_tile = MXU_DIM // S5_STATE * S5_GROUP_CH
    wb, wc = [], []
    for j in range(n_tiles):
        kh = ((j % half) * ch_per_tile) // MXU_DIM
        wb.append(wb_dense[kh * MXU_DIM:(kh + 1) * MXU_DIM, j * MXU_DIM:(j + 1) * MXU_DIM])
        wc.append(wc_dense[j * MXU_DIM:(j + 1) * MXU_DIM, kh * MXU_DIM:(kh + 1) * MXU_DIM])
    return jnp.stack(wb).astype(BF16), jnp.stack(wc).astype(BF16)


def _merge_kernel(alpha, h_ref, ya_ref, yb_ref, yc_ref, wg_ref, wa_ref, wb_ref, wc_ref, wo_ref,
                  g_ref, b_ref, o_ref):
    h = h_ref[...]
    hb = h.astype(BF16)
    merged = None
    for i, (y_ref, w_ref) in enumerate(((ya_ref, wa_ref), (yb_ref, wb_ref), (yc_ref, wc_ref))):
        gate = _sigmoid(_dot(hb, wg_ref[:, i * D_MODEL:(i + 1) * D_MODEL]))
        term = gate * _dot(y_ref[...], w_ref[...])
        merged = term if merged is None else merged + term
    mix = _dot(merged.astype(BF16), wo_ref[...])
    o_ref[...] = _layer_norm(alpha * h + mix, g_ref[...], b_ref[...])


def _merge(h, ya, yb, yc_tm, wg, wa, wb, wc, wo, g, b, alpha):
    bsz, seq, _ = h.shape
    tm = MERGE_ROWS
    const = lambda bb, i: (0, 0)
    row_spec = lambda w: pl.BlockSpec((None, tm, w), lambda bb, i: (bb, i, 0))
    return pl.pallas_call(
        functools.partial(_merge_kernel, alpha),
        grid=(bsz, seq // tm),
        in_specs=[
            row_spec(D_MODEL), row_spec(MIX_W), row_spec(MIX_W),
            pl.BlockSpec((tm, MIX_W), lambda bb, i: (i, bb)),
            pl.BlockSpec(wg.shape, const), pl.BlockSpec(wa.shape, const), pl.BlockSpec(wb.shape, const),
            pl.BlockSpec(wc.shape, const), pl.BlockSpec(wo.shape, const),
            pl.BlockSpec(g.shape, const), pl.BlockSpec(b.shape, const),
        ],
        out_specs=row_spec(D_MODEL),
        out_shape=jax.ShapeDtypeStruct(h.shape, F32),
        compiler_params=_params(("arbitrary", "arbitrary")),
        name="merge_ln",
    )(h, ya, yb, yc_tm, wg, wa, wb, wc, wo, g, b)


def _ffn_kernel(alpha, h_ref, wg_ref, wu_ref, wd_ref, g_ref, b_ref, o_ref):
    h = h_ref[...]
    hb = h.astype(BF16)
    acc = None
    for c0 in range(0, FFN_HIDDEN, FFN_CHUNK):
        sl = slice(c0, c0 + FFN_CHUNK)
        a = _dot(hb, wg_ref[:, sl])
        hid = (a * _sigmoid(a)) * _dot(hb, wu_ref[:, sl])
        part = _dot(hid.astype(BF16), wd_ref[sl, :])
        acc = part if acc is None else acc + part
    o_ref[...] = _layer_norm(alpha * h + acc, g_ref[...], b_ref[...])


def _ffn(h, wg, wu, wd, g, b, alpha):
    bsz, seq, _ = h.shape
    tm = FFN_ROWS
    const = lambda bb, i: (0, 0)
    row_spec = pl.BlockSpec((None, tm, D_MODEL), lambda bb, i: (bb, i, 0))
    return pl.pallas_call(
        functools.partial(_ffn_kernel, alpha),
        grid=(bsz, seq // tm),
        in_specs=[row_spec, pl.BlockSpec(wg.shape, const), pl.BlockSpec(wu.shape, const),
                  pl.BlockSpec(wd.shape, const), pl.BlockSpec(g.shape, const), pl.BlockSpec(b.shape, const)],
        out_specs=row_spec,
        out_shape=jax.ShapeDtypeStruct(h.shape, F32),
        compiler_params=_params(("arbitrary", "arbitrary")),
        name="ffn_ln",
    )(h, wg, wu, wd, g, b)


def kernel(x, w_in, hg_lower_bounds, hg_norm_w, fox_b_f, s5_lambda_re, s5_lambda_im, s5_log_step,
           s5_b_re, s5_b_im, s5_c_re, s5_c_im, s5_d, s5_w_glu, w_br_a, w_br_b, w_br_c, w_out,
           ln1_g, ln1_b, w_ffn_gate, w_ffn_up, w_ffn_down, ln2_g, ln2_b):
    depth = w_in.shape[0]
    bsz, seq, _ = x.shape
    alpha = (2 * depth) ** 0.25
    o_fx = 4 * MIX_W
    o_ff = o_fx + 3 * MIX_W
    o_s5 = o_ff + FOX_HEADS
    o_gate = o_s5 + MIX_W
    row = lambda a: a.reshape(1, -1)

    h = x
    for l in range(depth):
        w = w_in[l]
        whg = w[:, :o_fx].astype(BF16)
        wfx = w[:, o_fx:o_ff].astype(BF16)
        wff = jnp.pad(w[:, o_ff:o_s5], ((0, 0), (0, LANES - FOX_HEADS))).astype(BF16)
        ws5 = w[:, o_s5:o_gate].astype(BF16)
        wgate = w[:, o_gate:].astype(BF16)

        hg, fx, ff, u_tm = _in_proj(h, whg, wfx, wff, ws5)

        ya = _hgrn2(hg, hg_lower_bounds, row(hg_norm_w[l]), l)

        fz_t = ff[:, :, :FOX_HEADS].transpose(0, 2, 1)
        cum = _fox_cum(fz_t, fox_b_f[l].reshape(FOX_HEADS, 1))
        cum = cum.reshape(bsz, FOX_HEADS, seq // FOX_KV, FOX_KV).transpose(0, 2, 1, 3)
        yb = _fox(fx, cum)

        a_re, a_im, bbar_re, bbar_im = _s5_prep(s5_lambda_re[l], s5_lambda_im[l], s5_log_step[l],
                                                s5_b_re[l], s5_b_im[l])
        wb, wc = _s5_weights(bbar_re, bbar_im, s5_c_re[l], s5_c_im[l])
        yc_tm = _s5(u_tm.reshape(seq * bsz, MIX_W), a_re, a_im, wb, wc, row(s5_d[l]),
                    s5_w_glu[l].astype(BF16))
        yc_tm = yc_tm.reshape(seq, bsz * MIX_W)

        h = _merge(h, ya, yb, yc_tm, wgate, w_br_a[l].astype(BF16), w_br_b[l].astype(BF16),
                   w_br_c[l].astype(BF16), w_out[l].astype(BF16), row(ln1_g[l]), row(ln1_b[l]), alpha)
        h = _ffn(h, w_ffn_gate[l].astype(BF16), w_ffn_up[l].astype(BF16), w_ffn_down[l].astype(BF16),
                 row(ln2_g[l]), row(ln2_b[l]), alpha)
    return h
```

```python
import functools
import math

import jax
import jax.numpy as jnp
from jax import lax
from jax.experimental import pallas as pl
from jax.experimental.pallas import tpu as pltpu

F32 = jnp.float32
BF16 = jnp.bfloat16

D_MODEL = 1024
MIX_W = 512
HG_HEADS = 4
HG_KEY = 128
HG_CHUNK = 64
HG_REF_ROW = HG_CHUNK // 2 - 1
FOX_HEADS = 8
FOX_HEAD_DIM = 64
S5_GROUPS = 32
S5_STATE = 64
S5_GROUP_CH = 16
S5_NSTATE = S5_GROUPS * S5_STATE
FFN_HIDDEN = 2816
LN_EPS = 1e-5
RMS_EPS = 1e-6
EXP_CLAMP = 80.0

LANES = 128
MXU_DIM = 256
VMEM_LIMIT = 56 * 1024 * 1024

PROJ_ROWS = 512
HG_ROWS = 256
FOX_Q = 512
FOX_KV = 512
S5_STEPS = 32
MERGE_ROWS = 512
FFN_ROWS = 512
FFN_CHUNK = 1408


def _dot(a, b):
    return jnp.dot(a, b, preferred_element_type=F32)


def _dot_nt(a, b):
    return lax.dot_general(a, b, (((1,), (1,)), ((), ())), preferred_element_type=F32)


def _dot_tn(a, b):
    return lax.dot_general(a, b, (((0,), (0,)), ((), ())), preferred_element_type=F32)


def _sigmoid(x):
    return 1.0 / (1.0 + jnp.exp(-x))


def _layer_norm(z, g, b):
    mu = jnp.mean(z, axis=-1, keepdims=True)
    zc = z - mu
    var = jnp.mean(zc * zc, axis=-1, keepdims=True)
    return zc * lax.rsqrt(var + LN_EPS) * g + b


def _cumsum(x, axis):
    n = x.shape[axis]
    idx = lax.broadcasted_iota(jnp.int32, x.shape, axis)
    d = 1
    while d < n:
        x = x + jnp.where(idx >= d, pltpu.roll(x, d, axis=axis), 0.0)
        d *= 2
    return x


def _params(sem):
    return pltpu.CompilerParams(dimension_semantics=sem, vmem_limit_bytes=VMEM_LIMIT)


def _proj_kernel(x_ref, whg_ref, wfx_ref, wff_ref, ws5_ref, hg_ref, fx_ref, ff_ref, s5_ref):
    xb = x_ref[...].astype(BF16)
    hg_ref[...] = _dot(xb, whg_ref[...])
    fx_ref[...] = _dot(xb, wfx_ref[...]).astype(BF16)
    ff_ref[...] = _dot(xb, wff_ref[...])
    s5_ref[...] = _dot(xb, ws5_ref[...])


def _in_proj(h, whg, wfx, wff, ws5):
    bsz, seq, _ = h.shape
    tm = PROJ_ROWS
    const = lambda b, i: (0, 0)
    return pl.pallas_call(
        _proj_kernel,
        grid=(bsz, seq // tm),
        in_specs=[
            pl.BlockSpec((None, tm, D_MODEL), lambda b, i: (b, i, 0)),
            pl.BlockSpec(whg.shape, const),
            pl.BlockSpec(wfx.shape, const),
            pl.BlockSpec(wff.shape, const),
            pl.BlockSpec(ws5.shape, const),
        ],
        out_specs=[
            pl.BlockSpec((None, tm, 4 * MIX_W), lambda b, i: (b, i, 0)),
            pl.BlockSpec((None, tm, 3 * MIX_W), lambda b, i: (b, i, 0)),
            pl.BlockSpec((None, tm, LANES), lambda b, i: (b, i, 0)),
            pl.BlockSpec((tm, MIX_W), lambda b, i: (i, b)),
        ],
        out_shape=[
            jax.ShapeDtypeStruct((bsz, seq, 4 * MIX_W), F32),
            jax.ShapeDtypeStruct((bsz, seq, 3 * MIX_W), BF16),
            jax.ShapeDtypeStruct((bsz, seq, LANES), F32),
            jax.ShapeDtypeStruct((seq, bsz * MIX_W), F32),
        ],
        compiler_params=_params(("arbitrary", "arbitrary")),
        name="in_proj",
    )(h, whg, wfx, wff, ws5)


def _hgrn2_kernel(layer, hg_ref, lbp_ref, nw_ref, o_ref, st_ref):
    @pl.when(pl.program_id(1) == 0)
    def _():
        st_ref[...] = jnp.zeros_like(st_ref)

    lbp = lbp_ref[...]
    e = jnp.exp(lbp - jnp.max(lbp, axis=0, keepdims=True))
    sm = e / jnp.sum(e, axis=0, keepdims=True)
    cum = sm[0:1]
    for j in range(1, layer + 1):
        cum = cum + sm[j:j + 1]
    lb = cum - sm[0:1]

    nw = nw_ref[...]
    c = HG_CHUNK
    tri = (lax.broadcasted_iota(jnp.int32, (c, c), 0) >= lax.broadcasted_iota(jnp.int32, (c, c), 1))

    def chunk(ci, carry):
        r0 = pl.multiple_of(ci * c, c)
        rows = pl.ds(r0, c)
        q = hg_ref[rows, 0:MIX_W]
        fz = hg_ref[rows, MIX_W:2 * MIX_W]
        f = lb + (1.0 - lb) * _sigmoid(fz)
        k = 1.0 - f
        g = _cumsum(jnp.log(f), 0)
        g_ref = g[HG_REF_ROW:HG_REF_ROW + 1]
        g_last = g[c - 1:c]
        q_rel = (q * jnp.exp(jnp.minimum(g - g_ref, EXP_CLAMP))).astype(BF16)
        k_rel = (k * jnp.exp(jnp.minimum(g_ref - g, EXP_CLAMP))).astype(BF16)
        q_in = (q * jnp.exp(g)).astype(BF16)
        k_end = (k * jnp.exp(g_last - g)).astype(BF16)
        s_decay = jnp.exp(g_last)
        v = hg_ref[rows, 2 * MIX_W:3 * MIX_W].astype(BF16)
        gate = hg_ref[rows, 3 * MIX_W:4 * MIX_W]
        for h in range(HG_HEADS):
            sl = slice(h * HG_KEY, (h + 1) * HG_KEY)
            st = st_ref[h]
            scores = jnp.where(tri, _dot_nt(q_rel[:, sl], k_rel[:, sl]), 0.0)
            o = _dot(scores.astype(BF16), v[:, sl]) + _dot_nt(q_in[:, sl], st.astype(BF16))
            st_ref[h] = st * s_decay[:, sl] + _dot_tn(v[:, sl], k_end[:, sl])
            o = o * lax.rsqrt(jnp.mean(o * o, axis=-1, keepdims=True) + RMS_EPS) * nw
            gh = gate[:, sl]
            o_ref[rows, sl] = (o * (gh * _sigmoid(gh))).astype(BF16)
        return carry

    lax.fori_loop(0, hg_ref.shape[0] // c, chunk, 0)


def _hgrn2(hg, lbp, nw, layer):
    bsz, seq, _ = hg.shape
    tr = HG_ROWS
    return pl.pallas_call(
        functools.partial(_hgrn2_kernel, layer),
        grid=(bsz, seq // tr),
        in_specs=[
            pl.BlockSpec((None, tr, 4 * MIX_W), lambda b, i: (b, i, 0)),
            pl.BlockSpec(lbp.shape, lambda b, i: (0, 0)),
            pl.BlockSpec(nw.shape, lambda b, i: (0, 0)),
        ],
        out_specs=pl.BlockSpec((None, tr, MIX_W), lambda b, i: (b, i, 0)),
        out_shape=jax.ShapeDtypeStruct((bsz, seq, MIX_W), BF16),
        scratch_shapes=[pltpu.VMEM((HG_HEADS, HG_KEY, HG_KEY), F32)],
        compiler_params=_params(("arbitrary", "arbitrary")),
        name="hgrn2",
    )(hg, lbp, nw)


def _fox_cum_kernel(fz_ref, bf_ref, o_ref):
    x = fz_ref[...] + bf_ref[...]
    logsig = jnp.minimum(x, 0.0) - jnp.log(1.0 + jnp.exp(-jnp.abs(x)))
    o_ref[...] = _cumsum(logsig, 1)


def _fox_cum(fz_t, bf):
    bsz, nh, seq = fz_t.shape
    return pl.pallas_call(
        _fox_cum_kernel,
        grid=(bsz,),
        in_specs=[pl.BlockSpec((None, nh, seq), lambda b: (b, 0, 0)),
                  pl.BlockSpec(bf.shape, lambda b: (0, 0))],
        out_specs=pl.BlockSpec((None, nh, seq), lambda b: (b, 0, 0)),
        out_shape=jax.ShapeDtypeStruct((bsz, nh, seq), F32),
        compiler_params=_params(("arbitrary",)),
        name="fox_cum",
    )(fz_t, bf)


def _fox_lanes(h, lane):
    if h % 2 == 0:
        return lane < FOX_HEAD_DIM, FOX_HEAD_DIM
    return lane >= FOX_HEAD_DIM, 0


def _fox_kernel(q_ref, k_ref, v_ref, cum_ref, o_ref, qa_ref, ka_ref, va_ref, m_ref, acc_ref):
    qi = pl.program_id(1)
    tq, tk = FOX_Q, FOX_KV
    seq = k_ref.shape[0]
    lane = lax.broadcasted_iota(jnp.int32, (1, LANES), 1)
    causal = (lax.broadcasted_iota(jnp.int32, (tq, tk), 0) >= lax.broadcasted_iota(jnp.int32, (tq, tk), 1))
    log2e = math.log2(math.e)

    @pl.when(qi == 0)
    def _():
        def build(r, carry):
            rows = pl.ds(pl.multiple_of(r * tk, tk), tk)
            bias = cum_ref[rows, :] * (-log2e)
            for h in range(FOX_HEADS):
                sl = slice((h // 2) * LANES, (h // 2 + 1) * LANES)
                own, aux = _fox_lanes(h, lane)
                b0 = jnp.broadcast_to(bias[:, h:h + 1], (tk, LANES))
                hi = b0.astype(BF16).astype(F32)
                mid = (b0 - hi).astype(BF16).astype(F32)
                lo = b0 - hi - mid
                extra = jnp.where(lane == aux, hi, jnp.where(lane == aux + 1, mid,
                                                             jnp.where(lane == aux + 2, lo, 0.0)))
                ka_ref[h, rows, :] = jnp.where(own, k_ref[rows, sl].astype(F32), extra).astype(BF16)
                ones = jnp.where(lane == aux, 1.0, 0.0)
                va_ref[h, rows, :] = jnp.where(own, v_ref[rows, sl].astype(F32), ones).astype(BF16)
            return carry

        lax.fori_loop(0, seq // tk, build, 0)

    for h in range(FOX_HEADS):
        sl = slice((h // 2) * LANES, (h // 2 + 1) * LANES)
        own, aux = _fox_lanes(h, lane)
        ones = jnp.where((lane >= aux) & (lane < aux + 3), 1.0, 0.0)
        qa_ref[h] = jnp.where(own, q_ref[:, sl].astype(F32) * (FOX_HEAD_DIM ** -0.5 * log2e), ones).astype(BF16)

    def block(kb, first):
        rows = pl.ds(pl.multiple_of(kb * tk, tk), tk)
        for h in range(FOX_HEADS):
            s = _dot_nt(qa_ref[h], ka_ref[h, rows, :])
            if first:
                s = jnp.where(causal, s, -1e30)
            m_cur = jnp.max(s, axis=-1, keepdims=True)
            if first:
                m_new = jnp.broadcast_to(m_cur, (tq, LANES))
            else:
                m_prev = m_ref[h]
                m_new = jnp.maximum(m_prev, m_cur)
            p = jnp.concatenate([jnp.exp2(s[:, j * LANES:(j + 1) * LANES] - m_new)
                                 for j in range(tk // LANES)], axis=1)
            pv = _dot(p.astype(BF16), va_ref[h, rows, :])
            if first:
                acc_ref[h] = pv
            else:
                acc_ref[h] = jnp.exp2(m_prev - m_new) * acc_ref[h] + pv
            m_ref[h] = m_new

    block(qi, True)

    def body(kb, carry):
        block(kb, False)
        return carry

    lax.fori_loop(0, qi, body, 0)
    for pair in range(FOX_HEADS // 2):
        a = acc_ref[2 * pair]
        b = acc_ref[2 * pair + 1]
        o = jnp.where(lane < FOX_HEAD_DIM, a / a[:, FOX_HEAD_DIM:FOX_HEAD_DIM + 1], b / b[:, 0:1])
        o_ref[:, pair * LANES:(pair + 1) * LANES] = o.astype(BF16)


def _fox(fx, cum):
    bsz, seq, _ = fx.shape
    tq = FOX_Q
    state = pltpu.VMEM((FOX_HEADS, tq, LANES), F32)
    full = pltpu.VMEM((FOX_HEADS, seq, LANES), BF16)
    return pl.pallas_call(
        _fox_kernel,
        grid=(bsz, seq // tq),
        in_specs=[
            pl.BlockSpec((None, tq, MIX_W), lambda b, i: (b, i, 0)),
            pl.BlockSpec((None, seq, MIX_W), lambda b, i: (b, 0, 1)),
            pl.BlockSpec((None, seq, MIX_W), lambda b, i: (b, 0, 2)),
            pl.BlockSpec((None, seq, FOX_HEADS), lambda b, i: (b, 0, 0)),
        ],
        out_specs=pl.BlockSpec((None, tq, MIX_W), lambda b, i: (b, i, 0)),
        out_shape=jax.ShapeDtypeStruct((bsz, seq, MIX_W), BF16),
        scratch_shapes=[pltpu.VMEM((FOX_HEADS, tq, LANES), BF16), full, full, state, state],
        compiler_params=_params(("arbitrary", "arbitrary")),
        name="fox_attn",
    )(fx, fx, fx, cum)


def _s5_prep_kernel(lre_ref, lim_ref, ls_ref, bre_ref, bim_ref, are_ref, aim_ref, bbr_ref, bbi_ref):
    lr, li = lre_ref[...], lim_ref[...]
    dt = jnp.exp(ls_ref[...])
    mag = jnp.exp(lr * dt)
    ar = mag * jnp.cos(li * dt)
    ai = mag * jnp.sin(li * dt)
    den = lr * lr + li * li
    cr = ((ar - 1.0) * lr + ai * li) / den
    ci = (ai * lr - (ar - 1.0) * li) / den
    br, bi = bre_ref[...], bim_ref[...]
    bbr_ref[...] = cr * br - ci * bi
    bbi_ref[...] = cr * bi + ci * br
    are_ref[...] = jnp.broadcast_to(ar, are_ref.shape)
    aim_ref[...] = jnp.broadcast_to(ai, aim_ref.shape)


def _s5_prep(lam_re, lam_im, log_step, b_re, b_im):
    row = lambda a: a.reshape(1, S5_NSTATE)
    ls = jnp.repeat(log_step, S5_STATE).reshape(1, S5_NSTATE)
    to_hp = lambda b: b.transpose(2, 0, 1).reshape(S5_GROUP_CH, S5_NSTATE)
    vec = jax.ShapeDtypeStruct((8, S5_NSTATE), F32)
    mat = jax.ShapeDtypeStruct((S5_GROUP_CH, S5_NSTATE), F32)
    return pl.pallas_call(
        _s5_prep_kernel, out_shape=[vec, vec, mat, mat], name="s5_prep",
    )(row(lam_re), row(lam_im), ls, to_hp(b_re), to_hp(b_im))


def _s5_kernel(u_ref, are_ref, aim_ref, wb_ref, wc_ref, d_ref, wglu_ref, o_ref, x_ref, st_ref):
    @pl.when(pl.program_id(0) == 0)
    def _():
        st_ref[...] = jnp.zeros_like(st_ref)

    n = S5_NSTATE
    u = u_ref[...]
    ub = u.astype(BF16)
    n_tiles = 2 * n // MXU_DIM
    for j in range(n_tiles):
        kh = ((j % (n_tiles // 2)) * MXU_DIM // S5_STATE * S5_GROUP_CH) // MXU_DIM
        x_ref[:, j * MXU_DIM:(j + 1) * MXU_DIM] = _dot(ub[:, kh * MXU_DIM:(kh + 1) * MXU_DIM], wb_ref[j])

    steps = u_ref.shape[0] // 8
    width = 1024
    for c0 in range(0, n, width):
        re_sl = slice(c0, c0 + width)
        im_sl = slice(n + c0, n + c0 + width)
        ar = are_ref[:, re_sl]
        ai = aim_ref[:, re_sl]

        def step(t, carry):
            xr, xi = carry
            rows = pl.ds(pl.multiple_of(t * 8, 8), 8)
            nr = ar * xr - ai * xi + x_ref[rows, re_sl]
            ni = ar * xi + ai * xr + x_ref[rows, im_sl]
            x_ref[rows, re_sl] = nr
            x_ref[rows, im_sl] = ni
            return nr, ni

        xr, xi = lax.fori_loop(0, steps, step, (st_ref[:, re_sl], st_ref[:, im_sl]), unroll=4)
        st_ref[:, re_sl] = xr
        st_ref[:, im_sl] = xi

    halves = []
    per_half = n // MXU_DIM // 2
    for nh in range(MIX_W // MXU_DIM):
        acc_re = None
        acc_im = None
        for kk in range(per_half):
            kr = nh * per_half + kk
            ki = n // MXU_DIM + kr
            pr = _dot(x_ref[:, kr * MXU_DIM:(kr + 1) * MXU_DIM].astype(BF16), wc_ref[kr])
            pi = _dot(x_ref[:, ki * MXU_DIM:(ki + 1) * MXU_DIM].astype(BF16), wc_ref[ki])
            acc_re = pr if acc_re is None else acc_re + pr
            acc_im = pi if acc_im is None else acc_im + pi
        halves.append(acc_re - acc_im)
    y = jnp.concatenate(halves, axis=1) + d_ref[...] * u
    y = 0.5 * y * (1.0 + jnp.tanh(math.sqrt(2.0 / math.pi) * (y + 0.044715 * (y * y * y))))
    o_ref[...] = (y * _sigmoid(_dot(y.astype(BF16), wglu_ref[...]))).astype(BF16)


def _s5(u_tm, a_re, a_im, wb, wc, d, wglu):
    rows = S5_STEPS * 8
    total = u_tm.shape[0]
    const2 = lambda i: (0, 0)
    const3 = lambda i: (0, 0, 0)
    return pl.pallas_call(
        _s5_kernel,
        grid=(total // rows,),
        in_specs=[
            pl.BlockSpec((rows, MIX_W), lambda i: (i, 0)),
            pl.BlockSpec(a_re.shape, const2),
            pl.BlockSpec(a_im.shape, const2),
            pl.BlockSpec(wb.shape, const3),
            pl.BlockSpec(wc.shape, const3),
            pl.BlockSpec(d.shape, const2),
            pl.BlockSpec(wglu.shape, const2),
        ],
        out_specs=pl.BlockSpec((rows, MIX_W), lambda i: (i, 0)),
        out_shape=jax.ShapeDtypeStruct((total, MIX_W), BF16),
        scratch_shapes=[pltpu.VMEM((rows, 2 * S5_NSTATE), F32), pltpu.VMEM((8, 2 * S5_NSTATE), F32)],
        compiler_params=_params(("arbitrary",)),
        name="s5",
    )(u_tm, a_re, a_im, wb, wc, d, wglu)


def _s5_weights(bbar_re, bbar_im, c_re, c_im):
    eye = jnp.eye(S5_GROUPS, dtype=F32)

    def b_dense(bb):
        bb = bb.reshape(S5_GROUP_CH, S5_GROUPS, S5_STATE)
        return jnp.einsum('hgp,gk->ghkp', bb, eye).reshape(MIX_W, S5_NSTATE)

    def c_dense(cc):
        return jnp.einsum('ghp,gk->kpgh', cc, eye).reshape(S5_NSTATE, MIX_W)

    wb_dense = jnp.concatenate([b_dense(bbar_re), b_dense(bbar_im)], axis=1)
    wc_dense = jnp.concatenate([c_dense(c_re), c_dense(c_im)], axis=0)
    n_tiles = 2 * S5_NSTATE // MXU_DIM
    half = n_tiles // 2
    ch_per_tile = MXU_DIM // S5_STATE * S5_GROUP_CH
    wb, wc = [], []
    for j in range(n_tiles):
        kh = ((j % half) * ch_per_tile) // MXU_DIM
        wb.append(wb_dense[kh * MXU_DIM:(kh + 1) * MXU_DIM, j * MXU_DIM:(j + 1) * MXU_DIM])
        wc.append(wc_dense[j * MXU_DIM:(j + 1) * MXU_DIM, kh * MXU_DIM:(kh + 1) * MXU_DIM])
    return jnp.stack(wb).astype(BF16), jnp.stack(wc).astype(BF16)


def _merge_kernel(alpha, h_ref, ya_ref, yb_ref, yc_ref, wg_ref, wa_ref, wb_ref, wc_ref, wo_ref,
                  g_ref, b_ref, o_ref):
    h = h_ref[...]
    hb = h.astype(BF16)
    merged = None
    for i, (y_ref, w_ref) in enumerate(((ya_ref, wa_ref), (yb_ref, wb_ref), (yc_ref, wc_ref))):
        gate = _sigmoid(_dot(hb, wg_ref[:, i * D_MODEL:(i + 1) * D_MODEL]))
        term = gate * _dot(y_ref[...], w_ref[...])
        merged = term if merged is None else merged + term
    mix = _dot(merged.astype(BF16), wo_ref[...])
    o_ref[...] = _layer_norm(alpha * h + mix, g_ref[...], b_ref[...])


def _merge(h, ya, yb, yc_tm, wg, wa, wb, wc, wo, g, b, alpha):
    bsz, seq, _ = h.shape
    tm = MERGE_ROWS
    const = lambda bb, i: (0, 0)
    row_spec = lambda w: pl.BlockSpec((None, tm, w), lambda bb, i: (bb, i, 0))
    return pl.pallas_call(
        functools.partial(_merge_kernel, alpha),
        grid=(bsz, seq // tm),
        in_specs=[
            row_spec(D_MODEL), row_spec(MIX_W), row_spec(MIX_W),
            pl.BlockSpec((tm, MIX_W), lambda bb, i: (i, bb)),
            pl.BlockSpec(wg.shape, const), pl.BlockSpec(wa.shape, const), pl.BlockSpec(wb.shape, const),
            pl.BlockSpec(wc.shape, const), pl.BlockSpec(wo.shape, const),
            pl.BlockSpec(g.shape, const), pl.BlockSpec(b.shape, const),
        ],
        out_specs=row_spec(D_MODEL),
        out_shape=jax.ShapeDtypeStruct(h.shape, F32),
        compiler_params=_params(("arbitrary", "arbitrary")),
        name="merge_ln",
    )(h, ya, yb, yc_tm, wg, wa, wb, wc, wo, g, b)


def _ffn_kernel(alpha, h_ref, wg_ref, wu_ref, wd_ref, g_ref, b_ref, o_ref):
    h = h_ref[...]
    hb = h.astype(BF16)
    acc = None
    for c0 in range(0, FFN_HIDDEN, FFN_CHUNK):
        sl = slice(c0, c0 + FFN_CHUNK)
        a = _dot(hb, wg_ref[:, sl])
        hid = (a * _sigmoid(a)) * _dot(hb, wu_ref[:, sl])
        part = _dot(hid.astype(BF16), wd_ref[sl, :])
        acc = part if acc is None else acc + part
    o_ref[...] = _layer_norm(alpha * h + acc, g_ref[...], b_ref[...])


def _ffn(h, wg, wu, wd, g, b, alpha):
    bsz, seq, _ = h.shape
    tm = FFN_ROWS
    const = lambda bb, i: (0, 0)
    row_spec = pl.BlockSpec((None, tm, D_MODEL), lambda bb, i: (bb, i, 0))
    return pl.pallas_call(
        functools.partial(_ffn_kernel, alpha),
        grid=(bsz, seq // tm),
        in_specs=[row_spec, pl.BlockSpec(wg.shape, const), pl.BlockSpec(wu.shape, const),
                  pl.BlockSpec(wd.shape, const), pl.BlockSpec(g.shape, const), pl.BlockSpec(b.shape, const)],
        out_specs=row_spec,
        out_shape=jax.ShapeDtypeStruct(h.shape, F32),
        compiler_params=_params(("arbitrary", "arbitrary")),
        name="ffn_ln",
    )(h, wg, wu, wd, g, b)


def kernel(x, w_in, hg_lower_bounds, hg_norm_w, fox_b_f, s5_lambda_re, s5_lambda_im, s5_log_step,
           s5_b_re, s5_b_im, s5_c_re, s5_c_im, s5_d, s5_w_glu, w_br_a, w_br_b, w_br_c, w_out,
           ln1_g, ln1_b, w_ffn_gate, w_ffn_up, w_ffn_down, ln2_g, ln2_b):
    depth = w_in.shape[0]
    bsz, seq, _ = x.shape
    alpha = (2 * depth) ** 0.25
    o_fx = 4 * MIX_W
    o_ff = o_fx + 3 * MIX_W
    o_s5 = o_ff + FOX_HEADS
    o_gate = o_s5 + MIX_W
    row = lambda a: a.reshape(1, -1)

    h = x
    for l in range(depth):
        w = w_in[l]
        whg = w[:, :o_fx].astype(BF16)
        wfx = w[:, o_fx:o_ff].astype(BF16)
        wff = jnp.pad(w[:, o_ff:o_s5], ((0, 0), (0, LANES - FOX_HEADS))).astype(BF16)
        ws5 = w[:, o_s5:o_gate].astype(BF16)
        wgate = w[:, o_gate:].astype(BF16)

        hg, fx, ff, u_tm = _in_proj(h, whg, wfx, wff, ws5)

        ya = _hgrn2(hg, hg_lower_bounds, row(hg_norm_w[l]), l)

        fz_t = ff[:, :, :FOX_HEADS].transpose(0, 2, 1)
        cum = _fox_cum(fz_t, fox_b_f[l].reshape(FOX_HEADS, 1))
        yb = _fox(fx, cum.transpose(0, 2, 1))

        a_re, a_im, bbar_re, bbar_im = _s5_prep(s5_lambda_re[l], s5_lambda_im[l], s5_log_step[l],
                                                s5_b_re[l], s5_b_im[l])
        wb, wc = _s5_weights(bbar_re, bbar_im, s5_c_re[l], s5_c_im[l])
        yc_tm = _s5(u_tm.reshape(seq * bsz, MIX_W), a_re, a_im, wb, wc, row(s5_d[l]),
                    s5_w_glu[l].astype(BF16))
        yc_tm = yc_tm.reshape(seq, bsz * MIX_W)

        h = _merge(h, ya, yb, yc_tm, wgate, w_br_a[l].astype(BF16), w_br_b[l].astype(BF16),
                   w_br_c[l].astype(BF16), w_out[l].astype(BF16), row(ln1_g[l]), row(ln1_b[l]), alpha)
        h = _ffn(h, w_ffn_gate[l].astype(BF16), w_ffn_up[l].astype(BF16), w_ffn_down[l].astype(BF16),
                 row(ln2_g[l]), row(ln2_b[l]), alpha)
    return h
```

```python
import functools
import math

import jax
import jax.numpy as jnp
from jax import lax
from jax.experimental import pallas as pl
from jax.experimental.pallas import tpu as pltpu

F32 = jnp.float32
BF16 = jnp.bfloat16

D_MODEL = 1024
MIX_W = 512
HG_HEADS = 4
HG_KEY = 128
HG_CHUNK = 64
HG_REF_ROW = HG_CHUNK // 2 - 1
FOX_HEADS = 8
FOX_HEAD_DIM = 64
S5_GROUPS = 32
S5_STATE = 64
S5_GROUP_CH = 16
S5_NSTATE = S5_GROUPS * S5_STATE
FFN_HIDDEN = 2816
LN_EPS = 1e-5
RMS_EPS = 1e-6
EXP2_CLAMP = 115.0
LOG2E = math.log2(math.e)

LANES = 128
MXU_DIM = 256
VMEM_LIMIT = 56 * 1024 * 1024

PROJ_ROWS = 512
HG_ROWS = 256
HG_UNROLL = 4
FOX_Q = 512
FOX_KV = 512
S5_STEPS = 64
MERGE_ROWS = 512
FFN_ROWS = 512
FFN_CHUNK = 6 * MXU_DIM


def _dot(a, b):
    return jnp.dot(a, b, preferred_element_type=F32)


def _dot_nt(a, b):
    return lax.dot_general(a, b, (((1,), (1,)), ((), ())), preferred_element_type=F32)


def _dot_tn(a, b):
    return lax.dot_general(a, b, (((0,), (0,)), ((), ())), preferred_element_type=F32)


def _sigmoid(x):
    return 1.0 / (1.0 + jnp.exp2(x * -LOG2E))


def _layer_norm(z, g, b):
    mu = jnp.mean(z, axis=-1, keepdims=True)
    zc = z - mu
    var = jnp.mean(zc * zc, axis=-1, keepdims=True)
    return zc * lax.rsqrt(var + LN_EPS) * g + b


def _cumsum(x, axis):
    n = x.shape[axis]
    idx = lax.broadcasted_iota(jnp.int32, x.shape, axis)
    d = 1
    while d < n:
        x = x + jnp.where(idx >= d, pltpu.roll(x, d, axis=axis), 0.0)
        d *= 2
    return x


def _params(sem):
    return pltpu.CompilerParams(dimension_semantics=sem, vmem_limit_bytes=VMEM_LIMIT)


def _proj_kernel(x_ref, whg_ref, wfx_ref, wff_ref, ws5_ref, hg_ref, fx_ref, ff_ref, s5_ref):
    xb = x_ref[...].astype(BF16)
    hg_ref[...] = _dot(xb, whg_ref[...])
    fx_ref[...] = _dot(xb, wfx_ref[...]).astype(BF16)
    ff_ref[...] = _dot(xb, wff_ref[...])
    s5_ref[...] = _dot(xb, ws5_ref[...])


def _in_proj(h, whg, wfx, wff, ws5):
    bsz, seq, _ = h.shape
    tm = PROJ_ROWS
    const = lambda b, i: (0, 0)
    return pl.pallas_call(
        _proj_kernel,
        grid=(bsz, seq // tm),
        in_specs=[
            pl.BlockSpec((None, tm, D_MODEL), lambda b, i: (b, i, 0)),
            pl.BlockSpec(whg.shape, const),
            pl.BlockSpec(wfx.shape, const),
            pl.BlockSpec(wff.shape, const),
            pl.BlockSpec(ws5.shape, const),
        ],
        out_specs=[
            pl.BlockSpec((None, tm, 4 * MIX_W), lambda b, i: (b, i, 0)),
            pl.BlockSpec((None, tm, 3 * MIX_W), lambda b, i: (b, i, 0)),
            pl.BlockSpec((None, tm, LANES), lambda b, i: (b, i, 0)),
            pl.BlockSpec((tm, MIX_W), lambda b, i: (i, b)),
        ],
        out_shape=[
            jax.ShapeDtypeStruct((bsz, seq, 4 * MIX_W), F32),
            jax.ShapeDtypeStruct((bsz, seq, 3 * MIX_W), BF16),
            jax.ShapeDtypeStruct((bsz, seq, LANES), F32),
            jax.ShapeDtypeStruct((seq, bsz * MIX_W), F32),
        ],
        compiler_params=_params(("arbitrary", "arbitrary")),
        name="in_proj",
    )(h, whg, wfx, wff, ws5)


def _hgrn2_kernel(layer, hg_ref, lbp_ref, nw_ref, o_ref, st_ref):
    @pl.when(pl.program_id(1) == 0)
    def _():
        st_ref[...] = jnp.zeros_like(st_ref)

    lbp = lbp_ref[...]
    e = jnp.exp(lbp - jnp.max(lbp, axis=0, keepdims=True))
    sm = e / jnp.sum(e, axis=0, keepdims=True)
    cum = sm[0:1]
    for j in range(1, layer + 1):
        cum = cum + sm[j:j + 1]
    lb = cum - sm[0:1]

    nw = nw_ref[...]
    c = HG_CHUNK
    tri = (lax.broadcasted_iota(jnp.int32, (c, c), 0) >= lax.broadcasted_iota(jnp.int32, (c, c), 1))

    def chunk(ci, carry):
        r0 = pl.multiple_of(ci * c, c)
        rows = pl.ds(r0, c)
        q = hg_ref[rows, 0:MIX_W]
        fz = hg_ref[rows, MIX_W:2 * MIX_W]
        f = lb + (1.0 - lb) * _sigmoid(fz)
        k = 1.0 - f
        g = _cumsum(jnp.log2(f), 0)
        g_ref = g[HG_REF_ROW:HG_REF_ROW + 1]
        g_last = g[c - 1:c]
        d_ref = g - g_ref
        q_rel = (q * jnp.exp2(jnp.minimum(d_ref, EXP2_CLAMP))).astype(BF16)
        k_rel = (k * jnp.exp2(jnp.minimum(-d_ref, EXP2_CLAMP))).astype(BF16)
        q_in = (q * jnp.exp2(g)).astype(BF16)
        k_end = (k * jnp.exp2(g_last - g)).astype(BF16)
        s_decay = jnp.exp2(g_last)
        v = hg_ref[rows, 2 * MIX_W:3 * MIX_W]
        gate = hg_ref[rows, 3 * MIX_W:4 * MIX_W]
        for h in range(HG_HEADS):
            sl = slice(h * HG_KEY, (h + 1) * HG_KEY)
            st = st_ref[h]
            v_t = v[:, sl].T.astype(BF16)
            scores = jnp.where(tri, _dot_nt(q_rel[:, sl], k_rel[:, sl]), 0.0)
            lhs = jnp.concatenate([q_in[:, sl], scores.astype(BF16)], axis=1)
            rhs_t = jnp.concatenate([st.astype(BF16), v_t], axis=1)
            o = _dot_nt(lhs, rhs_t)
            st_ref[h] = st * s_decay[:, sl] + _dot(v_t, k_end[:, sl])
            o = o * lax.rsqrt(jnp.mean(o * o, axis=-1, keepdims=True) + RMS_EPS) * nw
            gh = gate[:, sl]
            o_ref[rows, sl] = (o * (gh * _sigmoid(gh))).astype(BF16)
        return carry

    lax.fori_loop(0, hg_ref.shape[0] // c, chunk, 0, unroll=HG_UNROLL)


def _hgrn2(hg, lbp, nw, layer):
    bsz, seq, _ = hg.shape
    tr = HG_ROWS
    return pl.pallas_call(
        functools.partial(_hgrn2_kernel, layer),
        grid=(bsz, seq // tr),
        in_specs=[
            pl.BlockSpec((None, tr, 4 * MIX_W), lambda b, i: (b, i, 0)),
            pl.BlockSpec(lbp.shape, lambda b, i: (0, 0)),
            pl.BlockSpec(nw.shape, lambda b, i: (0, 0)),
        ],
        out_specs=pl.BlockSpec((None, tr, MIX_W), lambda b, i: (b, i, 0)),
        out_shape=jax.ShapeDtypeStruct((bsz, seq, MIX_W), BF16),
        scratch_shapes=[pltpu.VMEM((HG_HEADS, HG_KEY, HG_KEY), F32)],
        compiler_params=_params(("arbitrary", "arbitrary")),
        name="hgrn2",
    )(hg, lbp, nw)


def _fox_cum_kernel(fz_ref, bf_ref, o_ref):
    x = fz_ref[...] + bf_ref[...]
    logsig = jnp.minimum(x, 0.0) - jnp.log(1.0 + jnp.exp(-jnp.abs(x)))
    o_ref[...] = _cumsum(logsig, 1)


def _fox_cum(fz_t, bf):
    bsz, nh, seq = fz_t.shape
    return pl.pallas_call(
        _fox_cum_kernel,
        grid=(bsz,),
        in_specs=[pl.BlockSpec((None, nh, seq), lambda b: (b, 0, 0)),
                  pl.BlockSpec(bf.shape, lambda b: (0, 0))],
        out_specs=pl.BlockSpec((None, nh, seq), lambda b: (b, 0, 0)),
        out_shape=jax.ShapeDtypeStruct((bsz, nh, seq), F32),
        compiler_params=_params(("arbitrary",)),
        name="fox_cum",
    )(fz_t, bf)


def _fox_lanes(h, lane):
    if h % 2 == 0:
        return lane < FOX_HEAD_DIM, FOX_HEAD_DIM
    return lane >= FOX_HEAD_DIM, 0


def _fox_kernel(q_ref, k_ref, v_ref, cum_ref, o_ref, qa_ref, ka_ref, va_ref, m_ref, acc_ref):
    qi = pl.program_id(1)
    tq, tk = FOX_Q, FOX_KV
    seq = k_ref.shape[0]
    lane = lax.broadcasted_iota(jnp.int32, (1, LANES), 1)
    causal = (lax.broadcasted_iota(jnp.int32, (tq, tk), 0) >= lax.broadcasted_iota(jnp.int32, (tq, tk), 1))
    log2e = math.log2(math.e)

    @pl.when(qi == 0)
    def _():
        def build(r, carry):
            rows = pl.ds(pl.multiple_of(r * tk, tk), tk)
            bias = cum_ref[rows, :] * (-log2e)
            for h in range(FOX_HEADS):
                sl = slice((h // 2) * LANES, (h // 2 + 1) * LANES)
                own, aux = _fox_lanes(h, lane)
                b0 = jnp.broadcast_to(bias[:, h:h + 1], (tk, LANES))
                hi = b0.astype(BF16).astype(F32)
                mid = (b0 - hi).astype(BF16).astype(F32)
                lo = b0 - hi - mid
                extra = jnp.where(lane == aux, hi, jnp.where(lane == aux + 1, mid,
                                                             jnp.where(lane == aux + 2, lo, 0.0)))
                ka_ref[h, rows, :] = jnp.where(own, k_ref[rows, sl].astype(F32), extra).astype(BF16)
                ones = jnp.where(lane == aux, 1.0, 0.0)
                va_ref[h, rows, :] = jnp.where(own, v_ref[rows, sl].astype(F32), ones).astype(BF16)
            return carry

        lax.fori_loop(0, seq // tk, build, 0)

    for h in range(FOX_HEADS):
        sl = slice((h // 2) * LANES, (h // 2 + 1) * LANES)
        own, aux = _fox_lanes(h, lane)
        ones = jnp.where((lane >= aux) & (lane < aux + 3), 1.0, 0.0)
        qa_ref[h] = jnp.where(own, q_ref[:, sl].astype(F32) * (FOX_HEAD_DIM ** -0.5 * log2e), ones).astype(BF16)

    def block(kb, first):
        rows = pl.ds(pl.multiple_of(kb * tk, tk), tk)
        for h in range(FOX_HEADS):
            s = _dot_nt(qa_ref[h], ka_ref[h, rows, :])
            if first:
                s = jnp.where(causal, s, -1e30)
            m_cur = jnp.max(s, axis=-1, keepdims=True)
            if first:
                m_new = jnp.broadcast_to(m_cur, (tq, LANES))
            else:
                m_prev = m_ref[h]
                m_new = jnp.maximum(m_prev, m_cur)
            p = jnp.concatenate([jnp.exp2(s[:, j * LANES:(j + 1) * LANES] - m_new)
                                 for j in range(tk // LANES)], axis=1)
            pv = _dot(p.astype(BF16), va_ref[h, rows, :])
            if first:
                acc_ref[h] = pv
            else:
                acc_ref[h] = jnp.exp2(m_prev - m_new) * acc_ref[h] + pv
            m_ref[h] = m_new

    block(qi, True)

    def body(kb, carry):
        block(kb, False)
        return carry

    lax.fori_loop(0, qi, body, 0)
    for pair in range(FOX_HEADS // 2):
        a = acc_ref[2 * pair]
        b = acc_ref[2 * pair + 1]
        o = jnp.where(lane < FOX_HEAD_DIM, a / a[:, FOX_HEAD_DIM:FOX_HEAD_DIM + 1], b / b[:, 0:1])
        o_ref[:, pair * LANES:(pair + 1) * LANES] = o.astype(BF16)


def _fox(fx, cum):
    bsz, seq, _ = fx.shape
    tq = FOX_Q
    state = pltpu.VMEM((FOX_HEADS, tq, LANES), F32)
    full = pltpu.VMEM((FOX_HEADS, seq, LANES), BF16)
    return pl.pallas_call(
        _fox_kernel,
        grid=(bsz, seq // tq),
        in_specs=[
            pl.BlockSpec((None, tq, MIX_W), lambda b, i: (b, i, 0)),
            pl.BlockSpec((None, seq, MIX_W), lambda b, i: (b, 0, 1)),
            pl.BlockSpec((None, seq, MIX_W), lambda b, i: (b, 0, 2)),
            pl.BlockSpec((None, seq, FOX_HEADS), lambda b, i: (b, 0, 0)),
        ],
        out_specs=pl.BlockSpec((None, tq, MIX_W), lambda b, i: (b, i, 0)),
        out_shape=jax.ShapeDtypeStruct((bsz, seq, MIX_W), BF16),
        scratch_shapes=[pltpu.VMEM((FOX_HEADS, tq, LANES), BF16), full, full, state, state],
        compiler_params=_params(("arbitrary", "arbitrary")),
        name="fox_attn",
    )(fx, fx, fx, cum)


def _s5_prep_kernel(lre_ref, lim_ref, ls_ref, bre_ref, bim_ref, are_ref, aim_ref, bbr_ref, bbi_ref):
    lr, li = lre_ref[...], lim_ref[...]
    dt = jnp.exp(ls_ref[...])
    mag = jnp.exp(lr * dt)
    ar = mag * jnp.cos(li * dt)
    ai = mag * jnp.sin(li * dt)
    den = lr * lr + li * li
    cr = ((ar - 1.0) * lr + ai * li) / den
    ci = (ai * lr - (ar - 1.0) * li) / den
    br, bi = bre_ref[...], bim_ref[...]
    bbr_ref[...] = cr * br - ci * bi
    bbi_ref[...] = cr * bi + ci * br
    are_ref[...] = jnp.broadcast_to(ar, are_ref.shape)
    aim_ref[...] = jnp.broadcast_to(ai, aim_ref.shape)


def _s5_prep(lam_re, lam_im, log_step, b_re, b_im):
    row = lambda a: a.reshape(1, S5_NSTATE)
    ls = jnp.repeat(log_step, S5_STATE).reshape(1, S5_NSTATE)
    to_hp = lambda b: b.transpose(2, 0, 1).reshape(S5_GROUP_CH, S5_NSTATE)
    vec = jax.ShapeDtypeStruct((8, S5_NSTATE), F32)
    mat = jax.ShapeDtypeStruct((S5_GROUP_CH, S5_NSTATE), F32)
    return pl.pallas_call(
        _s5_prep_kernel, out_shape=[vec, vec, mat, mat], name="s5_prep",
    )(row(lam_re), row(lam_im), ls, to_hp(b_re), to_hp(b_im))


def _s5_kernel(u_ref, are_ref, aim_ref, wb_ref, wc_ref, d_ref, wglu_ref, o_ref, x_ref, st_ref):
    @pl.when(pl.program_id(0) == 0)
    def _():
        st_ref[...] = jnp.zeros_like(st_ref)

    n = S5_NSTATE
    u = u_ref[...]
    ub = u.astype(BF16)
    n_tiles = 2 * n // MXU_DIM
    for j in range(n_tiles):
        kh = ((j % (n_tiles // 2)) * MXU_DIM // S5_STATE * S5_GROUP_CH) // MXU_DIM
        x_ref[:, j * MXU_DIM:(j + 1) * MXU_DIM] = _dot(ub[:, kh * MXU_DIM:(kh + 1) * MXU_DIM], wb_ref[j])

    steps = u_ref.shape[0] // 8
    width = 1024
    for c0 in range(0, n, width):
        re_sl = slice(c0, c0 + width)
        im_sl = slice(n + c0, n + c0 + width)
        ar = are_ref[:, re_sl]
        ai = aim_ref[:, re_sl]

        def step(t, carry):
            xr, xi = carry
            rows = pl.ds(pl.multiple_of(t * 8, 8), 8)
            nr = ar * xr - ai * xi + x_ref[rows, re_sl]
            ni = ar * xi + ai * xr + x_ref[rows, im_sl]
            x_ref[rows, re_sl] = nr
            x_ref[rows, im_sl] = ni
            return nr, ni

        xr, xi = lax.fori_loop(0, steps, step, (st_ref[:, re_sl], st_ref[:, im_sl]), unroll=4)
        st_ref[:, re_sl] = xr
        st_ref[:, im_sl] = xi

    halves = []
    per_half = n // MXU_DIM // 2
    for nh in range(MIX_W // MXU_DIM):
        acc_re = None
        acc_im = None
        for kk in range(per_half):
            kr = nh * per_half + kk
            ki = n // MXU_DIM + kr
            pr = _dot(x_ref[:, kr * MXU_DIM:(kr + 1) * MXU_DIM].astype(BF16), wc_ref[kr])
            pi = _dot(x_ref[:, ki * MXU_DIM:(ki + 1) * MXU_DIM].astype(BF16), wc_ref[ki])
            acc_re = pr if acc_re is None else acc_re + pr
            acc_im = pi if acc_im is None else acc_im + pi
        halves.append(acc_re - acc_im)
    y = jnp.concatenate(halves, axis=1) + d_ref[...] * u
    y = 0.5 * y * (1.0 + jnp.tanh(math.sqrt(2.0 / math.pi) * (y + 0.044715 * (y * y * y))))
    o_ref[...] = (y * _sigmoid(_dot(y.astype(BF16), wglu_ref[...]))).astype(BF16)


def _s5(u_tm, a_re, a_im, wb, wc, d, wglu):
    rows = S5_STEPS * 8
    total = u_tm.shape[0]
    const2 = lambda i: (0, 0)
    const3 = lambda i: (0, 0, 0)
    return pl.pallas_call(
        _s5_kernel,
        grid=(total // rows,),
        in_specs=[
            pl.BlockSpec((rows, MIX_W), lambda i: (i, 0)),
            pl.BlockSpec(a_re.shape, const2),
            pl.BlockSpec(a_im.shape, const2),
            pl.BlockSpec(wb.shape, const3),
            pl.BlockSpec(wc.shape, const3),
            pl.BlockSpec(d.shape, const2),
            pl.BlockSpec(wglu.shape, const2),
        ],
        out_specs=pl.BlockSpec((rows, MIX_W), lambda i: (i, 0)),
        out_shape=jax.ShapeDtypeStruct((total, MIX_W), BF16),
        scratch_shapes=[pltpu.VMEM((rows, 2 * S5_NSTATE), F32), pltpu.VMEM((8, 2 * S5_NSTATE), F32)],
        compiler_params=_params(("arbitrary",)),
        name="s5",
    )(u_tm, a_re, a_im, wb, wc, d, wglu)


def _s5_weights(bbar_re, bbar_im, c_re, c_im):
    half = S5_NSTATE // MXU_DIM
    g_per_tile = MXU_DIM // S5_STATE
    g_per_slab = MXU_DIM // S5_GROUP_CH
    t = jnp.arange(half)[:, None, None]
    r = jnp.arange(MXU_DIM)[None, :, None]
    c = jnp.arange(MXU_DIM)[None, None, :]
    slab = (t * g_per_tile) // g_per_slab
    b_keep = slab * g_per_slab + r // S5_GROUP_CH == t * g_per_tile + c // S5_STATE
    c_keep = t * g_per_tile + r // S5_STATE == slab * g_per_slab + c // S5_GROUP_CH

    def b_tiles(bb):
        bb = bb.reshape(S5_GROUP_CH, half, MXU_DIM).transpose(1, 0, 2)
        return jnp.where(b_keep, jnp.tile(bb, (1, g_per_slab, 1)), 0.0)

    def c_tiles(cc):
        cc = cc.transpose(0, 2, 1).reshape(half, MXU_DIM, S5_GROUP_CH)
        return jnp.where(c_keep, jnp.tile(cc, (1, 1, g_per_slab)), 0.0)

    wb = jnp.concatenate([b_tiles(bbar_re), b_tiles(bbar_im)], axis=0).astype(BF16)
    wc = jnp.concatenate([c_tiles(c_re), c_tiles(c_im)], axis=0).astype(BF16)
    return wb, wc


def _merge_kernel(alpha, h_ref, ya_ref, yb_ref, yc_ref, wg_ref, wa_ref, wb_ref, wc_ref, wo_ref,
                  g_ref, b_ref, o_ref):
    h = h_ref[...]
    hb = h.astype(BF16)
    merged = None
    for i, (y_ref, w_ref) in enumerate(((ya_ref, wa_ref), (yb_ref, wb_ref), (yc_ref, wc_ref))):
        gate = _sigmoid(_dot(hb, wg_ref[:, i * D_MODEL:(i + 1) * D_MODEL]))
        term = gate * _dot(y_ref[...], w_ref[...])
        merged = term if merged is None else merged + term
    mix = _dot(merged.astype(BF16), wo_ref[...])
    o_ref[...] = _layer_norm(alpha * h + mix, g_ref[...], b_ref[...])


def _merge(h, ya, yb, yc_tm, wg, wa, wb, wc, wo, g, b, alpha):
    bsz, seq, _ = h.shape
    tm = MERGE_ROWS
    const = lambda bb, i: (0, 0)
    row_spec = lambda w: pl.BlockSpec((None, tm, w), lambda bb, i: (bb, i, 0))
    return pl.pallas_call(
        functools.partial(_merge_kernel, alpha),
        grid=(bsz, seq // tm),
        in_specs=[
            row_spec(D_MODEL), row_spec(MIX_W), row_spec(MIX_W),
            pl.BlockSpec((tm, MIX_W), lambda bb, i: (i, bb)),
            pl.BlockSpec(wg.shape, const), pl.BlockSpec(wa.shape, const), pl.BlockSpec(wb.shape, const),
            pl.BlockSpec(wc.shape, const), pl.BlockSpec(wo.shape, const),
            pl.BlockSpec(g.shape, const), pl.BlockSpec(b.shape, const),
        ],
        out_specs=row_spec(D_MODEL),
        out_shape=jax.ShapeDtypeStruct(h.shape, F32),
        compiler_params=_params(("arbitrary", "arbitrary")),
        name="merge_ln",
    )(h, ya, yb, yc_tm, wg, wa, wb, wc, wo, g, b)


def _ffn_kernel(alpha, h_ref, wg_ref, wu_ref, wd_ref, g_ref, b_ref, o_ref):
    h = h_ref[...]
    hb = h.astype(BF16)
    acc = None
    for c0 in range(0, FFN_HIDDEN, FFN_CHUNK):
        sl = slice(c0, min(c0 + FFN_CHUNK, FFN_HIDDEN))
        a = _dot(hb, wg_ref[:, sl])
        hid = (a * _sigmoid(a)) * _dot(hb, wu_ref[:, sl])
        part = _dot(hid.astype(BF16), wd_ref[sl, :])
        acc = part if acc is None else acc + part
    o_ref[...] = _layer_norm(alpha * h + acc, g_ref[...], b_ref[...])


def _ffn(h, wg, wu, wd, g, b, alpha):
    bsz, seq, _ = h.shape
    tm = FFN_ROWS
    const = lambda bb, i: (0, 0)
    row_spec = pl.BlockSpec((None, tm, D_MODEL), lambda bb, i: (bb, i, 0))
    return pl.pallas_call(
        functools.partial(_ffn_kernel, alpha),
        grid=(bsz, seq // tm),
        in_specs=[row_spec, pl.BlockSpec(wg.shape, const), pl.BlockSpec(wu.shape, const),
                  pl.BlockSpec(wd.shape, const), pl.BlockSpec(g.shape, const), pl.BlockSpec(b.shape, const)],
        out_specs=row_spec,
        out_shape=jax.ShapeDtypeStruct(h.shape, F32),
        compiler_params=_params(("arbitrary", "arbitrary")),
        name="ffn_ln",
    )(h, wg, wu, wd, g, b)


def kernel(x, w_in, hg_lower_bounds, hg_norm_w, fox_b_f, s5_lambda_re, s5_lambda_im, s5_log_step,
           s5_b_re, s5_b_im, s5_c_re, s5_c_im, s5_d, s5_w_glu, w_br_a, w_br_b, w_br_c, w_out,
           ln1_g, ln1_b, w_ffn_gate, w_ffn_up, w_ffn_down, ln2_g, ln2_b):
    depth = w_in.shape[0]
    bsz, seq, _ = x.shape
    alpha = (2 * depth) ** 0.25
    o_fx = 4 * MIX_W
    o_ff = o_fx + 3 * MIX_W
    o_s5 = o_ff + FOX_HEADS
    o_gate = o_s5 + MIX_W
    row = lambda a: a.reshape(1, -1)

    h = x
    for l in range(depth):
        w = w_in[l]
        whg = w[:, :o_fx].astype(BF16)
        wfx = w[:, o_fx:o_ff].astype(BF16)
        wff = jnp.pad(w[:, o_ff:o_s5], ((0, 0), (0, LANES - FOX_HEADS))).astype(BF16)
        ws5 = w[:, o_s5:o_gate].astype(BF16)
        wgate = w[:, o_gate:].astype(BF16)

        hg, fx, ff, u_tm = _in_proj(h, whg, wfx, wff, ws5)

        ya = _hgrn2(hg, hg_lower_bounds, row(hg_norm_w[l]), l)

        fz_t = ff[:, :, :FOX_HEADS].transpose(0, 2, 1)
        cum = _fox_cum(fz_t, fox_b_f[l].reshape(FOX_HEADS, 1))
        yb = _fox(fx, cum.transpose(0, 2, 1))

        a_re, a_im, bbar_re, bbar_im = _s5_prep(s5_lambda_re[l], s5_lambda_im[l], s5_log_step[l],
                                                s5_b_re[l], s5_b_im[l])
        wb, wc = _s5_weights(bbar_re, bbar_im, s5_c_re[l], s5_c_im[l])
        yc_tm = _s5(u_tm.reshape(seq * bsz, MIX_W), a_re, a_im, wb, wc, row(s5_d[l]),
                    s5_w_glu[l].astype(BF16))
        yc_tm = yc_tm.reshape(seq, bsz * MIX_W)

        h = _merge(h, ya, yb, yc_tm, wgate, w_br_a[l].astype(BF16), w_br_b[l].astype(BF16),
                   w_br_c[l].astype(BF16), w_out[l].astype(BF16), row(ln1_g[l]), row(ln1_b[l]), alpha)
        h = _ffn(h, w_ffn_gate[l].astype(BF16), w_ffn_up[l].astype(BF16), w_ffn_down[l].astype(BF16),
                 row(ln2_g[l]), row(ln2_b[l]), alpha)
    return h
```

```python
import functools
import math

import jax
import jax.numpy as jnp
from jax import lax
from jax.experimental import pallas as pl
from jax.experimental.pallas import tpu as pltpu

F32 = jnp.float32
BF16 = jnp.bfloat16

D_MODEL = 1024
MIX_W = 512
HG_HEADS = 4
HG_KEY = 128
HG_CHUNK = 64
HG_REF_ROW = HG_CHUNK // 2 - 1
FOX_HEADS = 8
FOX_HEAD_DIM = 64
S5_GROUPS = 32
S5_STATE = 64
S5_GROUP_CH = 16
S5_NSTATE = S5_GROUPS * S5_STATE
FFN_HIDDEN = 2816
LN_EPS = 1e-5
RMS_EPS = 1e-6
EXP2_CLAMP = 115.0
LOG2E = math.log2(math.e)

LANES = 128
MXU_DIM = 256
VMEM_LIMIT = 56 * 1024 * 1024

PROJ_ROWS = 512
HG_ROWS = 256
HG_UNROLL = 4
FOX_Q = 512
FOX_KV = 512
S5_STEPS = 64
MERGE_ROWS = 512
FFN_ROWS = 512
FFN_CHUNK = 6 * MXU_DIM


def _dot(a, b):
    return jnp.dot(a, b, preferred_element_type=F32)


def _dot_nt(a, b):
    return lax.dot_general(a, b, (((1,), (1,)), ((), ())), preferred_element_type=F32)


def _dot_tn(a, b):
    return lax.dot_general(a, b, (((0,), (0,)), ((), ())), preferred_element_type=F32)


def _sigmoid(x):
    return 1.0 / (1.0 + jnp.exp2(x * -LOG2E))


def _layer_norm(z, g, b):
    mu = jnp.mean(z, axis=-1, keepdims=True)
    zc = z - mu
    var = jnp.mean(zc * zc, axis=-1, keepdims=True)
    return zc * lax.rsqrt(var + LN_EPS) * g + b


def _cumsum(x, axis):
    n = x.shape[axis]
    idx = lax.broadcasted_iota(jnp.int32, x.shape, axis)
    d = 1
    while d < n:
        x = x + jnp.where(idx >= d, pltpu.roll(x, d, axis=axis), 0.0)
        d *= 2
    return x


def _params(sem):
    return pltpu.CompilerParams(dimension_semantics=sem, vmem_limit_bytes=VMEM_LIMIT)


def _proj_kernel(x_ref, wmain_ref, wff_ref, ws5_ref, hg_ref, fx_ref, ff_ref, s5_ref):
    bsz, tt, _ = x_ref.shape
    xb = x_ref[...].reshape(bsz * tt, D_MODEL).astype(BF16)
    hg_ref[...] = _dot(xb, wmain_ref[:, :4 * MIX_W]).reshape(hg_ref.shape)
    fx_ref[...] = _dot(xb, wmain_ref[:, 4 * MIX_W:]).astype(BF16).reshape(fx_ref.shape)
    ff_ref[...] = _dot(xb, wff_ref[...]).reshape(ff_ref.shape)
    u = _dot(xb, ws5_ref[...])
    for b in range(bsz):
        for s in range(MIX_W // LANES):
            s5_ref[s, pl.ds(b, tt, stride=bsz), :] = u[b * tt:(b + 1) * tt, s * LANES:(s + 1) * LANES]


def _in_proj(h, w_all, ws5):
    bsz, seq, _ = h.shape
    tt = PROJ_ROWS // bsz
    n_main = 7 * MIX_W
    return pl.pallas_call(
        _proj_kernel,
        grid=(seq // tt,),
        in_specs=[
            pl.BlockSpec((bsz, tt, D_MODEL), lambda i: (0, i, 0)),
            pl.BlockSpec((D_MODEL, n_main), lambda i: (0, 0)),
            pl.BlockSpec((D_MODEL, LANES), lambda i: (0, n_main // LANES)),
            pl.BlockSpec(ws5.shape, lambda i: (0, 0)),
        ],
        out_specs=[
            pl.BlockSpec((bsz, tt, 4 * MIX_W), lambda i: (0, i, 0)),
            pl.BlockSpec((bsz, tt, 3 * MIX_W), lambda i: (0, i, 0)),
            pl.BlockSpec((bsz, tt, LANES), lambda i: (0, i, 0)),
            pl.BlockSpec((MIX_W // LANES, bsz * tt, LANES), lambda i: (0, i, 0)),
        ],
        out_shape=[
            jax.ShapeDtypeStruct((bsz, seq, 4 * MIX_W), F32),
            jax.ShapeDtypeStruct((bsz, seq, 3 * MIX_W), BF16),
            jax.ShapeDtypeStruct((bsz, seq, LANES), F32),
            jax.ShapeDtypeStruct((MIX_W // LANES, seq * bsz, LANES), F32),
        ],
        compiler_params=_params(("arbitrary",)),
        name="in_proj",
    )(h, w_all, w_all, ws5)


def _hgrn2_kernel(layer, hg_ref, lbp_ref, nw_ref, o_ref, st_ref):
    @pl.when(pl.program_id(1) == 0)
    def _():
        st_ref[...] = jnp.zeros_like(st_ref)

    lbp = lbp_ref[...]
    e = jnp.exp(lbp - jnp.max(lbp, axis=0, keepdims=True))
    sm = e / jnp.sum(e, axis=0, keepdims=True)
    cum = sm[0:1]
    for j in range(1, layer + 1):
        cum = cum + sm[j:j + 1]
    lb = cum - sm[0:1]

    nw = nw_ref[...]
    c = HG_CHUNK
    tri = (lax.broadcasted_iota(jnp.int32, (c, c), 0) >= lax.broadcasted_iota(jnp.int32, (c, c), 1))

    def chunk(ci, carry):
        r0 = pl.multiple_of(ci * c, c)
        rows = pl.ds(r0, c)
        q = hg_ref[rows, 0:MIX_W]
        fz = hg_ref[rows, MIX_W:2 * MIX_W]
        f = lb + (1.0 - lb) * _sigmoid(fz)
        k = 1.0 - f
        g = _cumsum(jnp.log2(f), 0)
        g_ref = g[HG_REF_ROW:HG_REF_ROW + 1]
        g_last = g[c - 1:c]
        d_ref = g - g_ref
        q_rel = (q * jnp.exp2(jnp.minimum(d_ref, EXP2_CLAMP))).astype(BF16)
        k_rel = (k * jnp.exp2(jnp.minimum(-d_ref, EXP2_CLAMP))).astype(BF16)
        q_in = (q * jnp.exp2(g)).astype(BF16)
        k_end = (k * jnp.exp2(g_last - g)).astype(BF16)
        s_decay = jnp.exp2(g_last)
        v = hg_ref[rows, 2 * MIX_W:3 * MIX_W]
        gate = hg_ref[rows, 3 * MIX_W:4 * MIX_W]
        for h in range(HG_HEADS):
            sl = slice(h * HG_KEY, (h + 1) * HG_KEY)
            st = st_ref[h]
            v_t = v[:, sl].T.astype(BF16)
            scores = jnp.where(tri, _dot_nt(q_rel[:, sl], k_rel[:, sl]), 0.0)
            lhs = jnp.concatenate([q_in[:, sl], scores.astype(BF16)], axis=1)
            rhs_t = jnp.concatenate([st.astype(BF16), v_t], axis=1)
            o = _dot_nt(lhs, rhs_t)
            st_ref[h] = st * s_decay[:, sl] + _dot(v_t, k_end[:, sl])
            o = o * lax.rsqrt(jnp.mean(o * o, axis=-1, keepdims=True) + RMS_EPS) * nw
            gh = gate[:, sl]
            o_ref[rows, sl] = (o * (gh * _sigmoid(gh))).astype(BF16)
        return carry

    lax.fori_loop(0, hg_ref.shape[0] // c, chunk, 0, unroll=HG_UNROLL)


def _hgrn2(hg, lbp, nw, layer):
    bsz, seq, _ = hg.shape
    tr = HG_ROWS
    return pl.pallas_call(
        functools.partial(_hgrn2_kernel, layer),
        grid=(bsz, seq // tr),
        in_specs=[
            pl.BlockSpec((None, tr, 4 * MIX_W), lambda b, i: (b, i, 0)),
            pl.BlockSpec(lbp.shape, lambda b, i: (0, 0)),
            pl.BlockSpec(nw.shape, lambda b, i: (0, 0)),
        ],
        out_specs=pl.BlockSpec((None, tr, MIX_W), lambda b, i: (b, i, 0)),
        out_shape=jax.ShapeDtypeStruct((bsz, seq, MIX_W), BF16),
        scratch_shapes=[pltpu.VMEM((HG_HEADS, HG_KEY, HG_KEY), F32)],
        compiler_params=_params(("arbitrary", "arbitrary")),
        name="hgrn2",
    )(hg, lbp, nw)


def _fox_cum_kernel(fz_ref, bf_ref, o_ref):
    x = fz_ref[...] + bf_ref[...]
    logsig = jnp.minimum(x, 0.0) - jnp.log(1.0 + jnp.exp(-jnp.abs(x)))
    o_ref[...] = _cumsum(logsig, 1)


def _fox_cum(fz_t, bf):
    bsz, nh, seq = fz_t.shape
    return pl.pallas_call(
        _fox_cum_kernel,
        grid=(bsz,),
        in_specs=[pl.BlockSpec((None, nh, seq), lambda b: (b, 0, 0)),
                  pl.BlockSpec(bf.shape, lambda b: (0, 0))],
        out_specs=pl.BlockSpec((None, nh, seq), lambda b: (b, 0, 0)),
        out_shape=jax.ShapeDtypeStruct((bsz, nh, seq), F32),
        compiler_params=_params(("arbitrary",)),
        name="fox_cum",
    )(fz_t, bf)


def _fox_lanes(h, lane):
    if h % 2 == 0:
        return lane < FOX_HEAD_DIM, FOX_HEAD_DIM
    return lane >= FOX_HEAD_DIM, 0


def _fox_kernel(q_ref, k_ref, v_ref, cum_ref, o_ref, qa_ref, ka_ref, va_ref, m_ref, acc_ref, s_ref):
    qi = pl.program_id(1)
    tq, tk = FOX_Q, FOX_KV
    seq = k_ref.shape[0]
    lane = lax.broadcasted_iota(jnp.int32, (1, LANES), 1)
    causal = (lax.broadcasted_iota(jnp.int32, (tq, tk), 0) >= lax.broadcasted_iota(jnp.int32, (tq, tk), 1))
    log2e = math.log2(math.e)

    @pl.when(qi == 0)
    def _():
        def build(r, carry):
            rows = pl.ds(pl.multiple_of(r * tk, tk), tk)
            bias = cum_ref[rows, :] * (-log2e)
            for h in range(FOX_HEADS):
                sl = slice((h // 2) * LANES, (h // 2 + 1) * LANES)
                own, aux = _fox_lanes(h, lane)
                b0 = jnp.broadcast_to(bias[:, h:h + 1], (tk, LANES))
                hi = b0.astype(BF16).astype(F32)
                mid = (b0 - hi).astype(BF16).astype(F32)
                lo = b0 - hi - mid
                extra = jnp.where(lane == aux, hi, jnp.where(lane == aux + 1, mid,
                                                             jnp.where(lane == aux + 2, lo, 0.0)))
                ka_ref[h, rows, :] = jnp.where(own, k_ref[rows, sl].astype(F32), extra).astype(BF16)
                ones = jnp.where(lane == aux, 1.0, 0.0)
                va_ref[h, rows, :] = jnp.where(own, v_ref[rows, sl].astype(F32), ones).astype(BF16)
            return carry

        lax.fori_loop(0, seq // tk, build, 0)

    for h in range(FOX_HEADS):
        sl = slice((h // 2) * LANES, (h // 2 + 1) * LANES)
        own, aux = _fox_lanes(h, lane)
        ones = jnp.where((lane >= aux) & (lane < aux + 3), 1.0, 0.0)
        qa_ref[h] = jnp.where(own, q_ref[:, sl].astype(F32) * (FOX_HEAD_DIM ** -0.5 * log2e), ones).astype(BF16)

    m_ref[...] = jnp.full(m_ref.shape, -1e30, F32)
    acc_ref[...] = jnp.zeros(acc_ref.shape, F32)

    def kv_rows(kb):
        return pl.ds(pl.multiple_of(kb * tk, tk), tk)

    def logits(h, kb, buf):
        s_ref[buf, h] = _dot_nt(qa_ref[h], ka_ref[h, kv_rows(kb), :])

    def update(h, kb, buf, masked):
        s = s_ref[buf, h]
        if masked:
            s = jnp.where(causal, s, -1e30)
        m_prev = m_ref[h]
        m_new = jnp.maximum(m_prev, jnp.max(s, axis=-1, keepdims=True))
        p = jnp.concatenate([jnp.exp2(s[:, j * LANES:(j + 1) * LANES] - m_new)
                             for j in range(tk // LANES)], axis=1)
        pv = _dot(p.astype(BF16), va_ref[h, kv_rows(kb), :])
        acc_ref[h] = jnp.exp2(m_prev - m_new) * acc_ref[h] + pv
        m_ref[h] = m_new

    for h in range(FOX_HEADS):
        logits(h, 0, 0)

    def step(j, cur):
        for h in range(FOX_HEADS):
            logits(h, j + 1, 1 - cur)
            update(h, j, cur, False)

    def body(i, carry):
        step(2 * i, 0)
        step(2 * i + 1, 1)
        return carry

    lax.fori_loop(0, qi // 2, body, 0)

    @pl.when(qi % 2 == 1)
    def _():
        step(qi - 1, 0)
        for h in range(FOX_HEADS):
            update(h, qi, 1, True)

    @pl.when(qi % 2 == 0)
    def _():
        for h in range(FOX_HEADS):
            update(h, qi, 0, True)

    for pair in range(FOX_HEADS // 2):
        a = acc_ref[2 * pair]
        b = acc_ref[2 * pair + 1]
        o = jnp.where(lane < FOX_HEAD_DIM, a / a[:, FOX_HEAD_DIM:FOX_HEAD_DIM + 1], b / b[:, 0:1])
        o_ref[:, pair * LANES:(pair + 1) * LANES] = o.astype(BF16)


def _fox(fx, cum):
    bsz, seq, _ = fx.shape
    tq = FOX_Q
    state = pltpu.VMEM((FOX_HEADS, tq, LANES), F32)
    full = pltpu.VMEM((FOX_HEADS, seq, LANES), BF16)
    return pl.pallas_call(
        _fox_kernel,
        grid=(bsz, seq // tq),
        in_specs=[
            pl.BlockSpec((None, tq, MIX_W), lambda b, i: (b, i, 0)),
            pl.BlockSpec((None, seq, MIX_W), lambda b, i: (b, 0, 1)),
            pl.BlockSpec((None, seq, MIX_W), lambda b, i: (b, 0, 2)),
            pl.BlockSpec((None, seq, FOX_HEADS), lambda b, i: (b, 0, 0)),
        ],
        out_specs=pl.BlockSpec((None, tq, MIX_W), lambda b, i: (b, i, 0)),
        out_shape=jax.ShapeDtypeStruct((bsz, seq, MIX_W), BF16),
        scratch_shapes=[pltpu.VMEM((FOX_HEADS, tq, LANES), BF16), full, full, state, state,
                        pltpu.VMEM((2, FOX_HEADS, tq, FOX_KV), F32)],
        compiler_params=_params(("arbitrary", "arbitrary")),
        name="fox_attn",
    )(fx, fx, fx, cum)


def _s5_prep_kernel(lre_ref, lim_ref, ls_ref, bre_ref, bim_ref, are_ref, aim_ref, bbr_ref, bbi_ref):
    lr, li = lre_ref[...], lim_ref[...]
    dt = jnp.exp(ls_ref[...])
    mag = jnp.exp(lr * dt)
    ar = mag * jnp.cos(li * dt)
    ai = mag * jnp.sin(li * dt)
    den = lr * lr + li * li
    cr = ((ar - 1.0) * lr + ai * li) / den
    ci = (ai * lr - (ar - 1.0) * li) / den
    br, bi = bre_ref[...], bim_ref[...]
    bbr_ref[...] = cr * br - ci * bi
    bbi_ref[...] = cr * bi + ci * br
    are_ref[...] = jnp.broadcast_to(ar, are_ref.shape)
    aim_ref[...] = jnp.broadcast_to(ai, aim_ref.shape)


def _s5_prep(lam_re, lam_im, log_step, b_re, b_im):
    row = lambda a: a.reshape(1, S5_NSTATE)
    ls = jnp.repeat(log_step, S5_STATE).reshape(1, S5_NSTATE)
    to_hp = lambda b: b.transpose(2, 0, 1).reshape(S5_GROUP_CH, S5_NSTATE)
    vec = jax.ShapeDtypeStruct((8, S5_NSTATE), F32)
    mat = jax.ShapeDtypeStruct((S5_GROUP_CH, S5_NSTATE), F32)
    return pl.pallas_call(
        _s5_prep_kernel, out_shape=[vec, vec, mat, mat], name="s5_prep",
    )(row(lam_re), row(lam_im), ls, to_hp(b_re), to_hp(b_im))


def _s5_kernel(u_ref, are_ref, aim_ref, wb_ref, wc_ref, d_ref, wglu_ref, o_ref, x_ref, st_ref, y_ref):
    @pl.when(pl.program_id(0) == 0)
    def _():
        st_ref[...] = jnp.zeros_like(st_ref)

    n = S5_NSTATE
    u = jnp.concatenate([u_ref[s] for s in range(MIX_W // LANES)], axis=1)
    ub = u.astype(BF16)
    n_tiles = 2 * n // MXU_DIM
    for j in range(n_tiles):
        kh = ((j % (n_tiles // 2)) * MXU_DIM // S5_STATE * S5_GROUP_CH) // MXU_DIM
        x_ref[:, j * MXU_DIM:(j + 1) * MXU_DIM] = _dot(ub[:, kh * MXU_DIM:(kh + 1) * MXU_DIM], wb_ref[j])

    steps = u_ref.shape[1] // 8
    width = 1024
    for c0 in range(0, n, width):
        re_sl = slice(c0, c0 + width)
        im_sl = slice(n + c0, n + c0 + width)
        ar = are_ref[:, re_sl]
        ai = aim_ref[:, re_sl]

        def step(t, carry):
            xr, xi = carry
            rows = pl.ds(pl.multiple_of(t * 8, 8), 8)
            nr = ar * xr - ai * xi + x_ref[rows, re_sl]
            ni = ar * xi + ai * xr + x_ref[rows, im_sl]
            x_ref[rows, re_sl] = nr
            x_ref[rows, im_sl] = ni
            return nr, ni

        xr, xi = lax.fori_loop(0, steps, step, (st_ref[:, re_sl], st_ref[:, im_sl]), unroll=4)
        st_ref[:, re_sl] = xr
        st_ref[:, im_sl] = xi

    halves = []
    per_half = n // MXU_DIM // 2
    for nh in range(MIX_W // MXU_DIM):
        acc_re = None
        acc_im = None
        for kk in range(per_half):
            kr = nh * per_half + kk
            ki = n // MXU_DIM + kr
            pr = _dot(x_ref[:, kr * MXU_DIM:(kr + 1) * MXU_DIM].astype(BF16), wc_ref[kr])
            pi = _dot(x_ref[:, ki * MXU_DIM:(ki + 1) * MXU_DIM].astype(BF16), wc_ref[ki])
            acc_re = pr if acc_re is None else acc_re + pr
            acc_im = pi if acc_im is None else acc_im + pi
        halves.append(acc_re - acc_im)
    y = jnp.concatenate(halves, axis=1) + d_ref[...] * u
    y = 0.5 * y * (1.0 + jnp.tanh(math.sqrt(2.0 / math.pi) * (y + 0.044715 * (y * y * y))))
    y = y * _sigmoid(_dot(y.astype(BF16), wglu_ref[...]))
    for s in range(MIX_W // LANES):
        y_ref[s] = y[:, s * LANES:(s + 1) * LANES]
    bsz = o_ref.shape[0]
    for b in range(bsz):
        o_ref[b] = jnp.concatenate([y_ref[s, pl.ds(b, steps, stride=bsz), :]
                                    for s in range(MIX_W // LANES)], axis=1).astype(BF16)


def _s5(u_tm, a_re, a_im, wb, wc, d, wglu, bsz):
    rows = S5_STEPS * bsz
    total = u_tm.shape[1]
    const2 = lambda i: (0, 0)
    const3 = lambda i: (0, 0, 0)
    return pl.pallas_call(
        _s5_kernel,
        grid=(total // rows,),
        in_specs=[
            pl.BlockSpec((MIX_W // LANES, rows, LANES), lambda i: (0, i, 0)),
            pl.BlockSpec(a_re.shape, const2),
            pl.BlockSpec(a_im.shape, const2),
            pl.BlockSpec(wb.shape, const3),
            pl.BlockSpec(wc.shape, const3),
            pl.BlockSpec(d.shape, const2),
            pl.BlockSpec(wglu.shape, const2),
        ],
        out_specs=pl.BlockSpec((bsz, S5_STEPS, MIX_W), lambda i: (0, i, 0)),
        out_shape=jax.ShapeDtypeStruct((bsz, total // bsz, MIX_W), BF16),
        scratch_shapes=[pltpu.VMEM((rows, 2 * S5_NSTATE), F32), pltpu.VMEM((8, 2 * S5_NSTATE), F32),
                        pltpu.VMEM((MIX_W // LANES, rows, LANES), F32)],
        compiler_params=_params(("arbitrary",)),
        name="s5",
    )(u_tm, a_re, a_im, wb, wc, d, wglu)


def _s5_weights(bbar_re, bbar_im, c_re, c_im):
    half = S5_NSTATE // MXU_DIM
    g_per_tile = MXU_DIM // S5_STATE
    g_per_slab = MXU_DIM // S5_GROUP_CH
    t = jnp.arange(half)[:, None, None]
    r = jnp.arange(MXU_DIM)[None, :, None]
    c = jnp.arange(MXU_DIM)[None, None, :]
    slab = (t * g_per_tile) // g_per_slab
    b_keep = slab * g_per_slab + r // S5_GROUP_CH == t * g_per_tile + c // S5_STATE
    c_keep = t * g_per_tile + r // S5_STATE == slab * g_per_slab + c // S5_GROUP_CH

    def b_tiles(bb):
        bb = bb.reshape(S5_GROUP_CH, half, MXU_DIM).transpose(1, 0, 2)
        return jnp.where(b_keep, jnp.tile(bb, (1, g_per_slab, 1)), 0.0)

    def c_tiles(cc):
        cc = cc.transpose(0, 2, 1).reshape(half, MXU_DIM, S5_GROUP_CH)
        return jnp.where(c_keep, jnp.tile(cc, (1, 1, g_per_slab)), 0.0)

    wb = jnp.concatenate([b_tiles(bbar_re), b_tiles(bbar_im)], axis=0).astype(BF16)
    wc = jnp.concatenate([c_tiles(c_re), c_tiles(c_im)], axis=0).astype(BF16)
    return wb, wc


def _merge_kernel(alpha, h_ref, ya_ref, yb_ref, yc_ref, wg_ref, wa_ref, wb_ref, wc_ref, wo_ref,
                  g_ref, b_ref, o_ref):
    h = h_ref[...]
    hb = h.astype(BF16)
    merged = None
    for i, (y_ref, w_ref) in enumerate(((ya_ref, wa_ref), (yb_ref, wb_ref), (yc_ref, wc_ref))):
        gate = _sigmoid(_dot(hb, wg_ref[:, i * D_MODEL:(i + 1) * D_MODEL]))
        term = gate * _dot(y_ref[...], w_ref[...])
        merged = term if merged is None else merged + term
    mix = _dot(merged.astype(BF16), wo_ref[...])
    o_ref[...] = _layer_norm(alpha * h + mix, g_ref[...], b_ref[...])


def _merge(h, ya, yb, yc, wg, wa, wb, wc, wo, g, b, alpha):
    bsz, seq, _ = h.shape
    tm = MERGE_ROWS
    const = lambda bb, i: (0, 0)
    row_spec = lambda w: pl.BlockSpec((None, tm, w), lambda bb, i: (bb, i, 0))
    return pl.pallas_call(
        functools.partial(_merge_kernel, alpha),
        grid=(bsz, seq // tm),
        in_specs=[
            row_spec(D_MODEL), row_spec(MIX_W), row_spec(MIX_W), row_spec(MIX_W),
            pl.BlockSpec(wg.shape, const), pl.BlockSpec(wa.shape, const), pl.BlockSpec(wb.shape, const),
            pl.BlockSpec(wc.shape, const), pl.BlockSpec(wo.shape, const),
            pl.BlockSpec(g.shape, const), pl.BlockSpec(b.shape, const),
        ],
        out_specs=row_spec(D_MODEL),
        out_shape=jax.ShapeDtypeStruct(h.shape, F32),
        compiler_params=_params(("arbitrary", "arbitrary")),
        name="merge_ln",
    )(h, ya, yb, yc, wg, wa, wb, wc, wo, g, b)


def _ffn_kernel(alpha, h_ref, wg_ref, wu_ref, wd_ref, g_ref, b_ref, o_ref):
    h = h_ref[...]
    hb = h.astype(BF16)
    acc = None
    for c0 in range(0, FFN_HIDDEN, FFN_CHUNK):
        sl = slice(c0, min(c0 + FFN_CHUNK, FFN_HIDDEN))
        a = _dot(hb, wg_ref[:, sl])
        hid = (a * _sigmoid(a)) * _dot(hb, wu_ref[:, sl])
        part = _dot(hid.astype(BF16), wd_ref[sl, :])
        acc = part if acc is None else acc + part
    o_ref[...] = _layer_norm(alpha * h + acc, g_ref[...], b_ref[...])


def _ffn(h, wg, wu, wd, g, b, alpha):
    bsz, seq, _ = h.shape
    tm = FFN_ROWS
    const = lambda bb, i: (0, 0)
    row_spec = pl.BlockSpec((None, tm, D_MODEL), lambda bb, i: (bb, i, 0))
    return pl.pallas_call(
        functools.partial(_ffn_kernel, alpha),
        grid=(bsz, seq // tm),
        in_specs=[row_spec, pl.BlockSpec(wg.shape, const), pl.BlockSpec(wu.shape, const),
                  pl.BlockSpec(wd.shape, const), pl.BlockSpec(g.shape, const), pl.BlockSpec(b.shape, const)],
        out_specs=row_spec,
        out_shape=jax.ShapeDtypeStruct(h.shape, F32),
        compiler_params=_params(("arbitrary", "arbitrary")),
        name="ffn_ln",
    )(h, wg, wu, wd, g, b)


def kernel(x, w_in, hg_lower_bounds, hg_norm_w, fox_b_f, s5_lambda_re, s5_lambda_im, s5_log_step,
           s5_b_re, s5_b_im, s5_c_re, s5_c_im, s5_d, s5_w_glu, w_br_a, w_br_b, w_br_c, w_out,
           ln1_g, ln1_b, w_ffn_gate, w_ffn_up, w_ffn_down, ln2_g, ln2_b):
    depth = w_in.shape[0]
    bsz, seq, _ = x.shape
    alpha = (2 * depth) ** 0.25
    o_s5 = 7 * MIX_W + FOX_HEADS
    o_gate = o_s5 + MIX_W
    row = lambda a: a.reshape(1, -1)

    h = x
    for l in range(depth):
        w = w_in[l].astype(BF16)
        ws5 = w[:, o_s5:o_gate]
        wgate = w[:, o_gate:]

        hg, fx, ff, u_tm = _in_proj(h, w, ws5)

        ya = _hgrn2(hg, hg_lower_bounds, row(hg_norm_w[l]), l)

        fz_t = ff[:, :, :FOX_HEADS].transpose(0, 2, 1)
        cum = _fox_cum(fz_t, fox_b_f[l].reshape(FOX_HEADS, 1))
        yb = _fox(fx, cum.transpose(0, 2, 1))

        a_re, a_im, bbar_re, bbar_im = _s5_prep(s5_lambda_re[l], s5_lambda_im[l], s5_log_step[l],
                                                s5_b_re[l], s5_b_im[l])
        wb, wc = _s5_weights(bbar_re, bbar_im, s5_c_re[l], s5_c_im[l])
        yc = _s5(u_tm, a_re, a_im, wb, wc, row(s5_d[l]), s5_w_glu[l].astype(BF16), bsz)

        h = _merge(h, ya, yb, yc, wgate, w_br_a[l].astype(BF16), w_br_b[l].astype(BF16),
                   w_br_c[l].astype(BF16), w_out[l].astype(BF16), row(ln1_g[l]), row(ln1_b[l]), alpha)
        h = _ffn(h, w_ffn_gate[l].astype(BF16), w_ffn_up[l].astype(BF16), w_ffn_down[l].astype(BF16),
                 row(ln2_g[l]), row(ln2_b[l]), alpha)
    return h
```

```python
import functools
import math

import jax
import jax.numpy as jnp
from jax import lax
from jax.experimental import pallas as pl
from jax.experimental.pallas import tpu as pltpu

F32 = jnp.float32
BF16 = jnp.bfloat16

D_MODEL = 1024
MIX_W = 512
HG_HEADS = 4
HG_KEY = 128
HG_CHUNK = 64
HG_REF_ROW = HG_CHUNK // 2 - 1
FOX_HEADS = 8
FOX_COL0 = 4 * MIX_W
FOX_HEAD_DIM = 64
S5_GROUPS = 32
S5_STATE = 64
S5_GROUP_CH = 16
S5_NSTATE = S5_GROUPS * S5_STATE
FFN_HIDDEN = 2816
LN_EPS = 1e-5
RMS_EPS = 1e-6
EXP2_CLAMP = 115.0
LOG2E = math.log2(math.e)

LANES = 128
MXU_DIM = 256
VMEM_LIMIT = 56 * 1024 * 1024

PROJ_ROWS = 512
HG_ROWS = 256
FOX_Q = 512
FOX_KV = 512
S5_PHASES = 2
S5_STEPS = 64
MERGE_ROWS = 512
FFN_ROWS = 512
FFN_CHUNK = 6 * MXU_DIM


def _dot(a, b):
    return jnp.dot(a, b, preferred_element_type=F32)


def _dot_nt(a, b):
    return lax.dot_general(a, b, (((1,), (1,)), ((), ())), preferred_element_type=F32)


def _dot_tn(a, b):
    return lax.dot_general(a, b, (((0,), (0,)), ((), ())), preferred_element_type=F32)


def _sigmoid(x):
    return 1.0 / (1.0 + jnp.exp2(x * -LOG2E))


def _layer_norm(z, g, b):
    mu = jnp.mean(z, axis=-1, keepdims=True)
    zc = z - mu
    var = jnp.mean(zc * zc, axis=-1, keepdims=True)
    return zc * lax.rsqrt(var + LN_EPS) * g + b


def _cumsum(x, axis):
    n = x.shape[axis]
    idx = lax.broadcasted_iota(jnp.int32, x.shape, axis)
    d = 1
    while d < n:
        x = x + jnp.where(idx >= d, pltpu.roll(x, d, axis=axis), 0.0)
        d *= 2
    return x


def _layer_spec(w, layer):
    return pl.BlockSpec((None,) + w.shape[1:], lambda *_: (layer, 0, 0))


def _params(sem):
    return pltpu.CompilerParams(dimension_semantics=sem, vmem_limit_bytes=VMEM_LIMIT)


def _proj_kernel(x_ref, wq_ref, wk_ref, wv_ref, wff_ref, ws5_ref, fx_ref, ff_ref, s5_ref):
    bsz, tt, _ = x_ref.shape
    xb = x_ref[...].reshape(bsz * tt, D_MODEL).astype(BF16)
    for j, w_ref in enumerate((wq_ref, wk_ref, wv_ref)):
        fx_ref[:, :, j * MIX_W:(j + 1) * MIX_W] = _dot(xb, w_ref[...]).astype(BF16).reshape(bsz, tt, MIX_W)
    ff_ref[...] = _dot(xb, wff_ref[...]).reshape(ff_ref.shape)
    u = _dot(xb, ws5_ref[...])
    for b in range(bsz):
        for s in range(MIX_W // LANES):
            s5_ref[s, pl.ds(b, tt, stride=bsz), :] = u[b * tt:(b + 1) * tt, s * LANES:(s + 1) * LANES]


def _in_proj(h, w_all, ws5, layer):
    bsz, seq, _ = h.shape
    tt = PROJ_ROWS // bsz
    fox_spec = lambda j: pl.BlockSpec((None, D_MODEL, MIX_W), lambda i: (layer, 0, FOX_COL0 // MIX_W + j))
    return pl.pallas_call(
        _proj_kernel,
        grid=(seq // tt,),
        in_specs=[
            pl.BlockSpec((bsz, tt, D_MODEL), lambda i: (0, i, 0)),
            fox_spec(0), fox_spec(1), fox_spec(2),
            pl.BlockSpec((None, D_MODEL, LANES), lambda i: (layer, 0, (FOX_COL0 + 3 * MIX_W) // LANES)),
            pl.BlockSpec(ws5.shape, lambda i: (0, 0)),
        ],
        out_specs=[
            pl.BlockSpec((bsz, tt, 3 * MIX_W), lambda i: (0, i, 0)),
            pl.BlockSpec((bsz, tt, LANES), lambda i: (0, i, 0)),
            pl.BlockSpec((MIX_W // LANES, bsz * tt, LANES), lambda i: (0, i, 0)),
        ],
        out_shape=[
            jax.ShapeDtypeStruct((bsz, seq, 3 * MIX_W), BF16),
            jax.ShapeDtypeStruct((bsz, seq, LANES), F32),
            jax.ShapeDtypeStruct((MIX_W // LANES, seq * bsz, LANES), F32),
        ],
        compiler_params=_params(("arbitrary",)),
        name="in_proj",
    )(h, w_all, w_all, w_all, w_all, ws5)


def _hgrn2_kernel(layer, h_ref, hn_ref, w_ref, lbp_ref, nw_ref, o_ref, st_ref, hg0_ref, hg1_ref):
    i = pl.program_id(1)

    @pl.when(i == 0)
    def _():
        st_ref[...] = jnp.zeros_like(st_ref)
        hg0_ref[...] = _dot(h_ref[...].astype(BF16), w_ref[...])

    lbp = lbp_ref[...]
    e = jnp.exp(lbp - jnp.max(lbp, axis=0, keepdims=True))
    sm = e / jnp.sum(e, axis=0, keepdims=True)
    cum = sm[0:1]
    for j in range(1, layer + 1):
        cum = cum + sm[j:j + 1]
    lb = cum - sm[0:1]

    nw = nw_ref[...]
    c = HG_CHUNK
    n_chunks = h_ref.shape[0] // c
    tri = (lax.broadcasted_iota(jnp.int32, (c, c), 0) >= lax.broadcasted_iota(jnp.int32, (c, c), 1))

    def chunk(ci, hg_ref):
        rows = slice(ci * c, (ci + 1) * c)
        q = hg_ref[rows, 0:MIX_W]
        fz = hg_ref[rows, MIX_W:2 * MIX_W]
        f = lb + (1.0 - lb) * _sigmoid(fz)
        k = 1.0 - f
        g = _cumsum(jnp.log2(f), 0)
        g_ref = g[HG_REF_ROW:HG_REF_ROW + 1]
        g_last = g[c - 1:c]
        d_ref = g - g_ref
        q_rel = (q * jnp.exp2(jnp.minimum(d_ref, EXP2_CLAMP))).astype(BF16)
        k_rel = (k * jnp.exp2(jnp.minimum(-d_ref, EXP2_CLAMP))).astype(BF16)
        q_in = (q * jnp.exp2(g)).astype(BF16)
        k_end = (k * jnp.exp2(g_last - g)).astype(BF16)
        s_decay = jnp.exp2(g_last)
        v = hg_ref[rows, 2 * MIX_W:3 * MIX_W]
        gate = hg_ref[rows, 3 * MIX_W:4 * MIX_W]
        for h in range(HG_HEADS):
            sl = slice(h * HG_KEY, (h + 1) * HG_KEY)
            st = st_ref[h]
            v_t = v[:, sl].T.astype(BF16)
            scores = jnp.where(tri, _dot_nt(q_rel[:, sl], k_rel[:, sl]), 0.0)
            lhs = jnp.concatenate([q_in[:, sl], scores.astype(BF16)], axis=1)
            rhs_t = jnp.concatenate([st.astype(BF16), v_t], axis=1)
            o = _dot_nt(lhs, rhs_t)
            st_ref[h] = st * s_decay[:, sl] + _dot(v_t, k_end[:, sl])
            o = o * lax.rsqrt(jnp.mean(o * o, axis=-1, keepdims=True) + RMS_EPS) * nw
            gh = gate[:, sl]
            o_ref[rows, sl] = (o * (gh * _sigmoid(gh))).astype(BF16)

    def tile(cur_ref, nxt_ref):
        hn = hn_ref[...].astype(BF16)
        cols = w_ref.shape[1] // n_chunks
        for ci in range(n_chunks):
            chunk(ci, cur_ref)
            nxt_ref[:, ci * cols:(ci + 1) * cols] = _dot(hn, w_ref[:, ci * cols:(ci + 1) * cols])

    @pl.when(i % 2 == 0)
    def _():
        tile(hg0_ref, hg1_ref)

    @pl.when(i % 2 == 1)
    def _():
        tile(hg1_ref, hg0_ref)


def _hgrn2(h, w_all, lbp, nw, layer):
    bsz, seq, _ = h.shape
    tr = HG_ROWS
    n_tiles = seq // tr
    hg = pltpu.VMEM((tr, 4 * MIX_W), F32)
    return pl.pallas_call(
        functools.partial(_hgrn2_kernel, layer),
        grid=(bsz, n_tiles),
        in_specs=[
            pl.BlockSpec((None, tr, D_MODEL), lambda b, i: (b, i, 0)),
            pl.BlockSpec((None, tr, D_MODEL), lambda b, i: (b, jnp.minimum(i + 1, n_tiles - 1), 0)),
            pl.BlockSpec((None, D_MODEL, 4 * MIX_W), lambda b, i: (layer, 0, 0)),
            pl.BlockSpec(lbp.shape, lambda b, i: (0, 0)),
            pl.BlockSpec(nw.shape, lambda b, i: (0, 0)),
        ],
        out_specs=pl.BlockSpec((None, tr, MIX_W), lambda b, i: (b, i, 0)),
        out_shape=jax.ShapeDtypeStruct((bsz, seq, MIX_W), BF16),
        scratch_shapes=[pltpu.VMEM((HG_HEADS, HG_KEY, HG_KEY), F32), hg, hg],
        compiler_params=_params(("arbitrary", "arbitrary")),
        name="hgrn2",
    )(h, h, w_all, lbp, nw)


def _fox_cum_kernel(fz_ref, bf_ref, o_ref):
    x = fz_ref[...] + bf_ref[...]
    logsig = jnp.minimum(x, 0.0) - jnp.log(1.0 + jnp.exp(-jnp.abs(x)))
    o_ref[...] = _cumsum(logsig, 1)


def _fox_cum(fz_t, bf):
    bsz, nh, seq = fz_t.shape
    return pl.pallas_call(
        _fox_cum_kernel,
        grid=(bsz,),
        in_specs=[pl.BlockSpec((None, nh, seq), lambda b: (b, 0, 0)),
                  pl.BlockSpec(bf.shape, lambda b: (0, 0))],
        out_specs=pl.BlockSpec((None, nh, seq), lambda b: (b, 0, 0)),
        out_shape=jax.ShapeDtypeStruct((bsz, nh, seq), F32),
        compiler_params=_params(("arbitrary",)),
        name="fox_cum",
    )(fz_t, bf)


def _fox_lanes(h, lane):
    if h % 2 == 0:
        return lane < FOX_HEAD_DIM, FOX_HEAD_DIM
    return lane >= FOX_HEAD_DIM, 0


def _fox_kernel(q_ref, k_ref, v_ref, cum_ref, o_ref, qa_ref, ka_ref, va_ref, m_ref, acc_ref, s_ref):
    qi = pl.program_id(1)
    tq, tk = FOX_Q, FOX_KV
    seq = k_ref.shape[0]
    lane = lax.broadcasted_iota(jnp.int32, (1, LANES), 1)
    causal = (lax.broadcasted_iota(jnp.int32, (tq, tk), 0) >= lax.broadcasted_iota(jnp.int32, (tq, tk), 1))
    log2e = math.log2(math.e)

    @pl.when(qi == 0)
    def _():
        def build(r, carry):
            rows = pl.ds(pl.multiple_of(r * tk, tk), tk)
            bias = cum_ref[rows, :] * (-log2e)
            for h in range(FOX_HEADS):
                sl = slice((h // 2) * LANES, (h // 2 + 1) * LANES)
                own, aux = _fox_lanes(h, lane)
                b0 = jnp.broadcast_to(bias[:, h:h + 1], (tk, LANES))
                hi = b0.astype(BF16).astype(F32)
                mid = (b0 - hi).astype(BF16).astype(F32)
                lo = b0 - hi - mid
                extra = jnp.where(lane == aux, hi, jnp.where(lane == aux + 1, mid,
                                                             jnp.where(lane == aux + 2, lo, 0.0)))
                ka_ref[h, rows, :] = jnp.where(own, k_ref[rows, sl].astype(F32), extra).astype(BF16)
                ones = jnp.where(lane == aux, 1.0, 0.0)
                va_ref[h, rows, :] = jnp.where(own, v_ref[rows, sl].astype(F32), ones).astype(BF16)
            return carry

        lax.fori_loop(0, seq // tk, build, 0)

    for h in range(FOX_HEADS):
        sl = slice((h // 2) * LANES, (h // 2 + 1) * LANES)
        own, aux = _fox_lanes(h, lane)
        ones = jnp.where((lane >= aux) & (lane < aux + 3), 1.0, 0.0)
        qa_ref[h] = jnp.where(own, q_ref[:, sl].astype(F32) * (FOX_HEAD_DIM ** -0.5 * log2e), ones).astype(BF16)

    m_ref[...] = jnp.full(m_ref.shape, -1e30, F32)
    acc_ref[...] = jnp.zeros(acc_ref.shape, F32)

    def kv_rows(kb):
        return pl.ds(pl.multiple_of(kb * tk, tk), tk)

    def logits(h, kb, buf):
        s_ref[buf, h] = _dot_nt(qa_ref[h], ka_ref[h, kv_rows(kb), :])

    def update(h, kb, buf, masked):
        s = s_ref[buf, h]
        if masked:
            s = jnp.where(causal, s, -1e30)
        m_prev = m_ref[h]
        m_new = jnp.maximum(m_prev, jnp.max(s, axis=-1, keepdims=True))
        p = jnp.concatenate([jnp.exp2(s[:, j * LANES:(j + 1) * LANES] - m_new)
                             for j in range(tk // LANES)], axis=1)
        pv = _dot(p.astype(BF16), va_ref[h, kv_rows(kb), :])
        acc_ref[h] = jnp.exp2(m_prev - m_new) * acc_ref[h] + pv
        m_ref[h] = m_new

    for h in range(FOX_HEADS):
        logits(h, 0, 0)

    def step(j, cur):
        for h in range(FOX_HEADS):
            logits(h, j + 1, 1 - cur)
            update(h, j, cur, False)

    def body(i, carry):
        step(2 * i, 0)
        step(2 * i + 1, 1)
        return carry

    lax.fori_loop(0, qi // 2, body, 0)

    @pl.when(qi % 2 == 1)
    def _():
        step(qi - 1, 0)
        for h in range(FOX_HEADS):
            update(h, qi, 1, True)

    @pl.when(qi % 2 == 0)
    def _():
        for h in range(FOX_HEADS):
            update(h, qi, 0, True)

    for pair in range(FOX_HEADS // 2):
        a = acc_ref[2 * pair]
        b = acc_ref[2 * pair + 1]
        o = jnp.where(lane < FOX_HEAD_DIM, a / a[:, FOX_HEAD_DIM:FOX_HEAD_DIM + 1], b / b[:, 0:1])
        o_ref[:, pair * LANES:(pair + 1) * LANES] = o.astype(BF16)


def _fox(fx, cum):
    bsz, seq, _ = fx.shape
    tq = FOX_Q
    state = pltpu.VMEM((FOX_HEADS, tq, LANES), F32)
    full = pltpu.VMEM((FOX_HEADS, seq, LANES), BF16)
    return pl.pallas_call(
        _fox_kernel,
        grid=(bsz, seq // tq),
        in_specs=[
            pl.BlockSpec((None, tq, MIX_W), lambda b, i: (b, i, 0)),
            pl.BlockSpec((None, seq, MIX_W), lambda b, i: (b, 0, 1)),
            pl.BlockSpec((None, seq, MIX_W), lambda b, i: (b, 0, 2)),
            pl.BlockSpec((None, seq, FOX_HEADS), lambda b, i: (b, 0, 0)),
        ],
        out_specs=pl.BlockSpec((None, tq, MIX_W), lambda b, i: (b, i, 0)),
        out_shape=jax.ShapeDtypeStruct((bsz, seq, MIX_W), BF16),
        scratch_shapes=[pltpu.VMEM((FOX_HEADS, tq, LANES), BF16), full, full, state, state,
                        pltpu.VMEM((2, FOX_HEADS, tq, FOX_KV), F32)],
        compiler_params=_params(("arbitrary", "arbitrary")),
        name="fox_attn",
    )(fx, fx, fx, cum)


def _s5_prep_kernel(lre_ref, lim_ref, ls_ref, bre_ref, bim_ref, are_ref, aim_ref, bbr_ref, bbi_ref):
    lr, li = lre_ref[...], lim_ref[...]
    dt = jnp.exp(ls_ref[...])
    mag = jnp.exp(lr * dt)
    ar = mag * jnp.cos(li * dt)
    ai = mag * jnp.sin(li * dt)
    den = lr * lr + li * li
    cr = ((ar - 1.0) * lr + ai * li) / den
    ci = (ai * lr - (ar - 1.0) * li) / den
    br, bi = bre_ref[...], bim_ref[...]
    bbr_ref[...] = cr * br - ci * bi
    bbi_ref[...] = cr * bi + ci * br
    are_ref[...] = jnp.broadcast_to(ar, are_ref.shape)
    aim_ref[...] = jnp.broadcast_to(ai, aim_ref.shape)


def _s5_prep(lam_re, lam_im, log_step, b_re, b_im):
    row = lambda a: a.reshape(1, S5_NSTATE)
    ls = jnp.repeat(log_step, S5_STATE).reshape(1, S5_NSTATE)
    to_hp = lambda b: b.transpose(2, 0, 1).reshape(S5_GROUP_CH, S5_NSTATE)
    vec = jax.ShapeDtypeStruct((8, S5_NSTATE), F32)
    mat = jax.ShapeDtypeStruct((S5_GROUP_CH, S5_NSTATE), F32)
    return pl.pallas_call(
        _s5_prep_kernel, out_shape=[vec, vec, mat, mat], name="s5_prep",
    )(row(lam_re), row(lam_im), ls, to_hp(b_re), to_hp(b_im))


def _s5_kernel(u_ref, are_ref, aim_ref, wb_ref, wc_ref, d_ref, wglu_ref, o_ref, x_ref, st_ref, y_ref):
    @pl.when(pl.program_id(0) == 0)
    def _():
        st_ref[...] = jnp.zeros_like(st_ref)

    n = S5_NSTATE
    n_tiles = 2 * n // MXU_DIM
    total_rows = u_ref.shape[1]
    steps = total_rows // 8
    u = jnp.concatenate([u_ref[s] for s in range(MIX_W // LANES)], axis=1)
    ub = u.astype(BF16)
    width = 1024

    def project_in(rows):
        for j in range(n_tiles):
            kh = ((j % (n_tiles // 2)) * MXU_DIM // S5_STATE * S5_GROUP_CH) // MXU_DIM
            x_ref[rows, j * MXU_DIM:(j + 1) * MXU_DIM] = _dot(ub[rows, kh * MXU_DIM:(kh + 1) * MXU_DIM],
                                                              wb_ref[j])

    def scan(t0, t1, state):
        for i, c0 in enumerate(range(0, n, width)):
            re_sl = slice(c0, c0 + width)
            im_sl = slice(n + c0, n + c0 + width)
            ar = are_ref[:, re_sl]
            ai = aim_ref[:, re_sl]
            xr, xi = state[i]
            for t in range(t0, t1):
                rows = slice(t * 8, t * 8 + 8)
                xr, xi = (ar * xr - ai * xi + x_ref[rows, re_sl], ar * xi + ai * xr + x_ref[rows, im_sl])
                x_ref[rows, re_sl] = xr
                x_ref[rows, im_sl] = xi
            state[i] = (xr, xi)

    def project_out(rows):
        halves = []
        per_half = n // MXU_DIM // 2
        for nh in range(MIX_W // MXU_DIM):
            acc_re = None
            acc_im = None
            for kk in range(per_half):
                kr = nh * per_half + kk
                ki = n // MXU_DIM + kr
                pr = _dot(x_ref[rows, kr * MXU_DIM:(kr + 1) * MXU_DIM].astype(BF16), wc_ref[kr])
                pi = _dot(x_ref[rows, ki * MXU_DIM:(ki + 1) * MXU_DIM].astype(BF16), wc_ref[ki])
                acc_re = pr if acc_re is None else acc_re + pr
                acc_im = pi if acc_im is None else acc_im + pi
            halves.append(acc_re - acc_im)
        return jnp.concatenate(halves, axis=1)

    phase_rows = total_rows // S5_PHASES
    phases = [slice(p * phase_rows, (p + 1) * phase_rows) for p in range(S5_PHASES)]
    for rows in phases:
        project_in(rows)
    state = [(st_ref[:, c0:c0 + width], st_ref[:, n + c0:n + c0 + width]) for c0 in range(0, n, width)]
    outs = []
    for p, rows in enumerate(phases):
        scan(p * steps // S5_PHASES, (p + 1) * steps // S5_PHASES, state)
        outs.append(project_out(rows))
    for i, c0 in enumerate(range(0, n, width)):
        st_ref[:, c0:c0 + width] = state[i][0]
        st_ref[:, n + c0:n + c0 + width] = state[i][1]
    y = jnp.concatenate(outs, axis=0) + d_ref[...] * u
    y = 0.5 * y * (1.0 + jnp.tanh(math.sqrt(2.0 / math.pi) * (y + 0.044715 * (y * y * y))))
    y = y * _sigmoid(_dot(y.astype(BF16), wglu_ref[...]))
    for s in range(MIX_W // LANES):
        y_ref[s] = y[:, s * LANES:(s + 1) * LANES]
    bsz = o_ref.shape[0]
    for b in range(bsz):
        o_ref[b] = jnp.concatenate([y_ref[s, pl.ds(b, steps, stride=bsz), :]
                                    for s in range(MIX_W // LANES)], axis=1).astype(BF16)


def _s5(u_tm, a_re, a_im, wb, wc, d, wglu, bsz, layer):
    rows = S5_STEPS * bsz
    total = u_tm.shape[1]
    const2 = lambda i: (0, 0)
    const3 = lambda i: (0, 0, 0)
    return pl.pallas_call(
        _s5_kernel,
        grid=(total // rows,),
        in_specs=[
            pl.BlockSpec((MIX_W // LANES, rows, LANES), lambda i: (0, i, 0)),
            pl.BlockSpec(a_re.shape, const2),
            pl.BlockSpec(a_im.shape, const2),
            pl.BlockSpec(wb.shape, const3),
            pl.BlockSpec(wc.shape, const3),
            pl.BlockSpec(d.shape, const2),
            _layer_spec(wglu, layer),
        ],
        out_specs=pl.BlockSpec((bsz, S5_STEPS, MIX_W), lambda i: (0, i, 0)),
        out_shape=jax.ShapeDtypeStruct((bsz, total // bsz, MIX_W), BF16),
        scratch_shapes=[pltpu.VMEM((rows, 2 * S5_NSTATE), F32), pltpu.VMEM((8, 2 * S5_NSTATE), F32),
                        pltpu.VMEM((MIX_W // LANES, rows, LANES), F32)],
        compiler_params=_params(("arbitrary",)),
        name="s5",
    )(u_tm, a_re, a_im, wb, wc, d, wglu)


def _s5_weights(bbar_re, bbar_im, c_re, c_im):
    half = S5_NSTATE // MXU_DIM
    g_per_tile = MXU_DIM // S5_STATE
    g_per_slab = MXU_DIM // S5_GROUP_CH
    t = jnp.arange(half)[:, None, None]
    r = jnp.arange(MXU_DIM)[None, :, None]
    c = jnp.arange(MXU_DIM)[None, None, :]
    slab = (t * g_per_tile) // g_per_slab
    b_keep = slab * g_per_slab + r // S5_GROUP_CH == t * g_per_tile + c // S5_STATE
    c_keep = t * g_per_tile + r // S5_STATE == slab * g_per_slab + c // S5_GROUP_CH

    def b_tiles(bb):
        bb = bb.reshape(S5_GROUP_CH, half, MXU_DIM).transpose(1, 0, 2)
        return jnp.where(b_keep, jnp.tile(bb, (1, g_per_slab, 1)), 0.0)

    def c_tiles(cc):
        cc = cc.transpose(0, 2, 1).reshape(half, MXU_DIM, S5_GROUP_CH)
        return jnp.where(c_keep, jnp.tile(cc, (1, 1, g_per_slab)), 0.0)

    wb = jnp.concatenate([b_tiles(bbar_re), b_tiles(bbar_im)], axis=0).astype(BF16)
    wc = jnp.concatenate([c_tiles(c_re), c_tiles(c_im)], axis=0).astype(BF16)
    return wb, wc


def _merge_kernel(alpha, h_ref, ya_ref, yb_ref, yc_ref, wg_ref, wa_ref, wb_ref, wc_ref, wo_ref,
                  g_ref, b_ref, o_ref):
    h = h_ref[...]
    hb = h.astype(BF16)
    merged = None
    for i, (y_ref, w_ref) in enumerate(((ya_ref, wa_ref), (yb_ref, wb_ref), (yc_ref, wc_ref))):
        gate = _sigmoid(_dot(hb, wg_ref[:, i * D_MODEL:(i + 1) * D_MODEL]))
        term = gate * _dot(y_ref[...], w_ref[...])
        merged = term if merged is None else merged + term
    mix = _dot(merged.astype(BF16), wo_ref[...])
    o_ref[...] = _layer_norm(alpha * h + mix, g_ref[...], b_ref[...])


def _merge(h, ya, yb, yc, wg, wa, wb, wc, wo, g, b, alpha, layer):
    bsz, seq, _ = h.shape
    tm = MERGE_ROWS
    const = lambda bb, i: (0, 0)
    row_spec = lambda w: pl.BlockSpec((None, tm, w), lambda bb, i: (bb, i, 0))
    return pl.pallas_call(
        functools.partial(_merge_kernel, alpha),
        grid=(bsz, seq // tm),
        in_specs=[
            row_spec(D_MODEL), row_spec(MIX_W), row_spec(MIX_W), row_spec(MIX_W),
            pl.BlockSpec(wg.shape, const), _layer_spec(wa, layer), _layer_spec(wb, layer),
            _layer_spec(wc, layer), _layer_spec(wo, layer),
            pl.BlockSpec(g.shape, const), pl.BlockSpec(b.shape, const),
        ],
        out_specs=row_spec(D_MODEL),
        out_shape=jax.ShapeDtypeStruct(h.shape, F32),
        compiler_params=_params(("arbitrary", "arbitrary")),
        name="merge_ln",
    )(h, ya, yb, yc, wg, wa, wb, wc, wo, g, b)


def _ffn_kernel(alpha, h_ref, wg_ref, wu_ref, wd_ref, g_ref, b_ref, o_ref):
    h = h_ref[...]
    hb = h.astype(BF16)
    acc = None
    for c0 in range(0, FFN_HIDDEN, FFN_CHUNK):
        sl = slice(c0, min(c0 + FFN_CHUNK, FFN_HIDDEN))
        a = _dot(hb, wg_ref[:, sl])
        hid = (a * _sigmoid(a)) * _dot(hb, wu_ref[:, sl])
        part = _dot(hid.astype(BF16), wd_ref[sl, :])
        acc = part if acc is None else acc + part
    o_ref[...] = _layer_norm(alpha * h + acc, g_ref[...], b_ref[...])


def _ffn(h, wg, wu, wd, g, b, alpha, layer):
    bsz, seq, _ = h.shape
    tm = FFN_ROWS
    const = lambda bb, i: (0, 0)
    row_spec = pl.BlockSpec((None, tm, D_MODEL), lambda bb, i: (bb, i, 0))
    return pl.pallas_call(
        functools.partial(_ffn_kernel, alpha),
        grid=(bsz, seq // tm),
        in_specs=[row_spec, _layer_spec(wg, layer), _layer_spec(wu, layer), _layer_spec(wd, layer),
                  pl.BlockSpec(g.shape, const), pl.BlockSpec(b.shape, const)],
        out_specs=row_spec,
        out_shape=jax.ShapeDtypeStruct(h.shape, F32),
        compiler_params=_params(("arbitrary", "arbitrary")),
        name="ffn_ln",
    )(h, wg, wu, wd, g, b)


def kernel(x, w_in, hg_lower_bounds, hg_norm_w, fox_b_f, s5_lambda_re, s5_lambda_im, s5_log_step,
           s5_b_re, s5_b_im, s5_c_re, s5_c_im, s5_d, s5_w_glu, w_br_a, w_br_b, w_br_c, w_out,
           ln1_g, ln1_b, w_ffn_gate, w_ffn_up, w_ffn_down, ln2_g, ln2_b):
    depth = w_in.shape[0]
    bsz, seq, _ = x.shape
    alpha = (2 * depth) ** 0.25
    o_s5 = FOX_COL0 + 3 * MIX_W + FOX_HEADS
    o_gate = o_s5 + MIX_W
    row = lambda a: a.reshape(1, -1)

    bf = lambda a: a.astype(BF16)
    w_in_b, w_glu_b = bf(w_in), bf(s5_w_glu)
    w_a, w_b, w_c, w_o = bf(w_br_a), bf(w_br_b), bf(w_br_c), bf(w_out)
    w_fg, w_fu, w_fd = bf(w_ffn_gate), bf(w_ffn_up), bf(w_ffn_down)

    h = x
    for l in range(depth):
        ws5 = w_in_b[l, :, o_s5:o_gate]
        wgate = w_in_b[l, :, o_gate:]

        fx, ff, u_tm = _in_proj(h, w_in_b, ws5, l)

        ya = _hgrn2(h, w_in_b, hg_lower_bounds, row(hg_norm_w[l]), l)

        fz_t = ff[:, :, :FOX_HEADS].transpose(0, 2, 1)
        cum = _fox_cum(fz_t, fox_b_f[l].reshape(FOX_HEADS, 1))
        yb = _fox(fx, cum.transpose(0, 2, 1))

        a_re, a_im, bbar_re, bbar_im = _s5_prep(s5_lambda_re[l], s5_lambda_im[l], s5_log_step[l],
                                                s5_b_re[l], s5_b_im[l])
        wb, wc = _s5_weights(bbar_re, bbar_im, s5_c_re[l], s5_c_im[l])
        yc = _s5(u_tm, a_re, a_im, wb, wc, row(s5_d[l]), w_glu_b, bsz, l)

        h = _merge(h, ya, yb, yc, wgate, w_a, w_b, w_c, w_o, row(ln1_g[l]), row(ln1_b[l]), alpha, l)
        h = _ffn(h, w_fg, w_fu, w_fd, row(ln2_g[l]), row(ln2_b[l]), alpha, l)
    return h
```

```python
import functools
import math

import jax
import jax.numpy as jnp
from jax import lax
from jax.experimental import pallas as pl
from jax.experimental.pallas import tpu as pltpu

F32 = jnp.float32
BF16 = jnp.bfloat16

D_MODEL = 1024
MIX_W = 512
HG_HEADS = 4
HG_KEY = 128
HG_CHUNK = 64
HG_REF_ROW = HG_CHUNK // 2 - 1
FOX_HEADS = 8
FOX_COL0 = 4 * MIX_W
FOX_HEAD_DIM = 64
S5_GROUPS = 32
S5_STATE = 64
S5_GROUP_CH = 16
S5_NSTATE = S5_GROUPS * S5_STATE
FFN_HIDDEN = 2816
LN_EPS = 1e-5
RMS_EPS = 1e-6
EXP2_CLAMP = 115.0
LOG2E = math.log2(math.e)

LANES = 128
MXU_DIM = 256
VMEM_LIMIT = 56 * 1024 * 1024

PROJ_ROWS = 512
HG_ROWS = 256
FOX_Q = 512
FOX_KV = 512
S5_PHASES = 2
S5_STEPS = 64
MERGE_ROWS = 1024
FFN_ROWS = 1024
ROW_SUB = 512
FFN_CHUNK = 6 * MXU_DIM


def _dot(a, b):
    return jnp.dot(a, b, preferred_element_type=F32)


def _dot_nt(a, b):
    return lax.dot_general(a, b, (((1,), (1,)), ((), ())), preferred_element_type=F32)


def _dot_tn(a, b):
    return lax.dot_general(a, b, (((0,), (0,)), ((), ())), preferred_element_type=F32)


def _sigmoid(x):
    return 1.0 / (1.0 + jnp.exp2(x * -LOG2E))


def _layer_norm(z, g, b):
    mu = jnp.mean(z, axis=-1, keepdims=True)
    zc = z - mu
    var = jnp.mean(zc * zc, axis=-1, keepdims=True)
    return zc * lax.rsqrt(var + LN_EPS) * g + b


def _cumsum(x, axis):
    n = x.shape[axis]
    idx = lax.broadcasted_iota(jnp.int32, x.shape, axis)
    d = 1
    while d < n:
        x = x + jnp.where(idx >= d, pltpu.roll(x, d, axis=axis), 0.0)
        d *= 2
    return x


def _layer_spec(w, layer):
    return pl.BlockSpec((None,) + w.shape[1:], lambda *_: (layer, 0, 0), pipeline_mode=pl.Buffered(1))


def _params(sem):
    return pltpu.CompilerParams(dimension_semantics=sem, vmem_limit_bytes=VMEM_LIMIT)


def _proj_kernel(x_ref, wq_ref, wk_ref, wv_ref, wff_ref, ws5_ref, fx_ref, ff_ref, s5_ref):
    bsz, tt, _ = x_ref.shape
    xb = x_ref[...].reshape(bsz * tt, D_MODEL).astype(BF16)
    for j, w_ref in enumerate((wq_ref, wk_ref, wv_ref)):
        fx_ref[:, :, j * MIX_W:(j + 1) * MIX_W] = _dot(xb, w_ref[...]).astype(BF16).reshape(bsz, tt, MIX_W)
    ff_ref[...] = _dot(xb, wff_ref[...]).reshape(ff_ref.shape)
    u = _dot(xb, ws5_ref[...])
    for b in range(bsz):
        for s in range(MIX_W // LANES):
            s5_ref[s, pl.ds(b, tt, stride=bsz), :] = u[b * tt:(b + 1) * tt, s * LANES:(s + 1) * LANES]


def _in_proj(h, w_all, ws5, layer):
    bsz, seq, _ = h.shape
    tt = PROJ_ROWS // bsz
    fox_spec = lambda j: pl.BlockSpec((None, D_MODEL, MIX_W), lambda i: (layer, 0, FOX_COL0 // MIX_W + j))
    return pl.pallas_call(
        _proj_kernel,
        grid=(seq // tt,),
        in_specs=[
            pl.BlockSpec((bsz, tt, D_MODEL), lambda i: (0, i, 0)),
            fox_spec(0), fox_spec(1), fox_spec(2),
            pl.BlockSpec((None, D_MODEL, LANES), lambda i: (layer, 0, (FOX_COL0 + 3 * MIX_W) // LANES)),
            pl.BlockSpec(ws5.shape, lambda i: (0, 0)),
        ],
        out_specs=[
            pl.BlockSpec((bsz, tt, 3 * MIX_W), lambda i: (0, i, 0)),
            pl.BlockSpec((bsz, tt, LANES), lambda i: (0, i, 0)),
            pl.BlockSpec((MIX_W // LANES, bsz * tt, LANES), lambda i: (0, i, 0)),
        ],
        out_shape=[
            jax.ShapeDtypeStruct((bsz, seq, 3 * MIX_W), BF16),
            jax.ShapeDtypeStruct((bsz, seq, LANES), F32),
            jax.ShapeDtypeStruct((MIX_W // LANES, seq * bsz, LANES), F32),
        ],
        compiler_params=_params(("arbitrary",)),
        name="in_proj",
    )(h, w_all, w_all, w_all, w_all, ws5)


def _hgrn2_kernel(layer, h_ref, hn_ref, w_ref, lbp_ref, nw_ref, o_ref, st_ref, hg0_ref, hg1_ref):
    i = pl.program_id(1)

    @pl.when(i == 0)
    def _():
        st_ref[...] = jnp.zeros_like(st_ref)
        hg0_ref[...] = _dot(h_ref[...].astype(BF16), w_ref[...])

    lbp = lbp_ref[...]
    e = jnp.exp(lbp - jnp.max(lbp, axis=0, keepdims=True))
    sm = e / jnp.sum(e, axis=0, keepdims=True)
    cum = sm[0:1]
    for j in range(1, layer + 1):
        cum = cum + sm[j:j + 1]
    lb = cum - sm[0:1]

    nw = nw_ref[...]
    c = HG_CHUNK
    n_chunks = h_ref.shape[0] // c
    tri = (lax.broadcasted_iota(jnp.int32, (c, c), 0) >= lax.broadcasted_iota(jnp.int32, (c, c), 1))

    def gates(ci, hg_ref):
        rows = slice(ci * c, (ci + 1) * c)
        q = hg_ref[rows, 0:MIX_W]
        fz = hg_ref[rows, MIX_W:2 * MIX_W]
        f = lb + (1.0 - lb) * _sigmoid(fz)
        k = 1.0 - f
        g = _cumsum(jnp.log2(f), 0)
        g_ref = g[HG_REF_ROW:HG_REF_ROW + 1]
        g_last = g[c - 1:c]
        d_ref = g - g_ref
        q_rel = (q * jnp.exp2(jnp.minimum(d_ref, EXP2_CLAMP))).astype(BF16)
        k_rel = (k * jnp.exp2(jnp.minimum(-d_ref, EXP2_CLAMP))).astype(BF16)
        q_in = (q * jnp.exp2(g)).astype(BF16)
        k_end = (k * jnp.exp2(g_last - g)).astype(BF16)
        s_decay = jnp.exp2(g_last)
        return q_rel, k_rel, q_in, k_end, s_decay

    def mix(ci, hg_ref, operands):
        rows = slice(ci * c, (ci + 1) * c)
        q_rel, k_rel, q_in, k_end, s_decay = operands
        v = hg_ref[rows, 2 * MIX_W:3 * MIX_W]
        gate = hg_ref[rows, 3 * MIX_W:4 * MIX_W]
        for h in range(HG_HEADS):
            sl = slice(h * HG_KEY, (h + 1) * HG_KEY)
            st = st_ref[h]
            v_t = v[:, sl].T.astype(BF16)
            scores = jnp.where(tri, _dot_nt(q_rel[:, sl], k_rel[:, sl]), 0.0)
            lhs = jnp.concatenate([q_in[:, sl], scores.astype(BF16)], axis=1)
            rhs_t = jnp.concatenate([st.astype(BF16), v_t], axis=1)
            o = _dot_nt(lhs, rhs_t)
            st_ref[h] = st * s_decay[:, sl] + _dot(v_t, k_end[:, sl])
            o = o * lax.rsqrt(jnp.mean(o * o, axis=-1, keepdims=True) + RMS_EPS) * nw
            gh = gate[:, sl]
            o_ref[rows, sl] = (o * (gh * _sigmoid(gh))).astype(BF16)

    def tile(cur_ref, nxt_ref):
        hn = hn_ref[...].astype(BF16)
        cols = w_ref.shape[1] // n_chunks
        operands = gates(0, cur_ref)
        for ci in range(n_chunks):
            ahead = gates(ci + 1, cur_ref) if ci + 1 < n_chunks else None
            mix(ci, cur_ref, operands)
            nxt_ref[:, ci * cols:(ci + 1) * cols] = _dot(hn, w_ref[:, ci * cols:(ci + 1) * cols])
            operands = ahead

    @pl.when(i % 2 == 0)
    def _():
        tile(hg0_ref, hg1_ref)

    @pl.when(i % 2 == 1)
    def _():
        tile(hg1_ref, hg0_ref)


def _hgrn2(h, w_all, lbp, nw, layer):
    bsz, seq, _ = h.shape
    tr = HG_ROWS
    n_tiles = seq // tr
    hg = pltpu.VMEM((tr, 4 * MIX_W), F32)
    return pl.pallas_call(
        functools.partial(_hgrn2_kernel, layer),
        grid=(bsz, n_tiles),
        in_specs=[
            pl.BlockSpec((None, tr, D_MODEL), lambda b, i: (b, i, 0)),
            pl.BlockSpec((None, tr, D_MODEL), lambda b, i: (b, jnp.minimum(i + 1, n_tiles - 1), 0)),
            pl.BlockSpec((None, D_MODEL, 4 * MIX_W), lambda b, i: (layer, 0, 0)),
            pl.BlockSpec(lbp.shape, lambda b, i: (0, 0)),
            pl.BlockSpec(nw.shape, lambda b, i: (0, 0)),
        ],
        out_specs=pl.BlockSpec((None, tr, MIX_W), lambda b, i: (b, i, 0)),
        out_shape=jax.ShapeDtypeStruct((bsz, seq, MIX_W), BF16),
        scratch_shapes=[pltpu.VMEM((HG_HEADS, HG_KEY, HG_KEY), F32), hg, hg],
        compiler_params=_params(("arbitrary", "arbitrary")),
        name="hgrn2",
    )(h, h, w_all, lbp, nw)


def _fox_cum_kernel(fz_ref, bf_ref, o_ref):
    x = fz_ref[...] + bf_ref[...]
    logsig = jnp.minimum(x, 0.0) - jnp.log(1.0 + jnp.exp(-jnp.abs(x)))
    o_ref[...] = _cumsum(logsig, 1)


def _fox_cum(fz_t, bf):
    bsz, nh, seq = fz_t.shape
    return pl.pallas_call(
        _fox_cum_kernel,
        grid=(bsz,),
        in_specs=[pl.BlockSpec((None, nh, seq), lambda b: (b, 0, 0)),
                  pl.BlockSpec(bf.shape, lambda b: (0, 0))],
        out_specs=pl.BlockSpec((None, nh, seq), lambda b: (b, 0, 0)),
        out_shape=jax.ShapeDtypeStruct((bsz, nh, seq), F32),
        compiler_params=_params(("arbitrary",)),
        name="fox_cum",
    )(fz_t, bf)


def _fox_lanes(h, lane):
    if h % 2 == 0:
        return lane < FOX_HEAD_DIM, FOX_HEAD_DIM
    return lane >= FOX_HEAD_DIM, 0


def _fox_kernel(q_ref, k_ref, v_ref, cum_ref, o_ref, qa_ref, ka_ref, va_ref, m_ref, acc_ref, s_ref):
    qi = pl.program_id(1)
    tq, tk = FOX_Q, FOX_KV
    seq = k_ref.shape[0]
    lane = lax.broadcasted_iota(jnp.int32, (1, LANES), 1)
    causal = (lax.broadcasted_iota(jnp.int32, (tq, tk), 0) >= lax.broadcasted_iota(jnp.int32, (tq, tk), 1))
    log2e = math.log2(math.e)

    @pl.when(qi == 0)
    def _():
        def build(r, carry):
            rows = pl.ds(pl.multiple_of(r * tk, tk), tk)
            bias = cum_ref[rows, :] * (-log2e)
            for h in range(FOX_HEADS):
                sl = slice((h // 2) * LANES, (h // 2 + 1) * LANES)
                own, aux = _fox_lanes(h, lane)
                b0 = jnp.broadcast_to(bias[:, h:h + 1], (tk, LANES))
                hi = b0.astype(BF16).astype(F32)
                mid = (b0 - hi).astype(BF16).astype(F32)
                lo = b0 - hi - mid
                extra = jnp.where(lane == aux, hi, jnp.where(lane == aux + 1, mid,
                                                             jnp.where(lane == aux + 2, lo, 0.0)))
                ka_ref[h, rows, :] = jnp.where(own, k_ref[rows, sl].astype(F32), extra).astype(BF16)
                ones = jnp.where(lane == aux, 1.0, 0.0)
                va_ref[h, rows, :] = jnp.where(own, v_ref[rows, sl].astype(F32), ones).astype(BF16)
            return carry

        lax.fori_loop(0, seq // tk, build, 0)

    for h in range(FOX_HEADS):
        sl = slice((h // 2) * LANES, (h // 2 + 1) * LANES)
        own, aux = _fox_lanes(h, lane)
        ones = jnp.where((lane >= aux) & (lane < aux + 3), 1.0, 0.0)
        qa_ref[h] = jnp.where(own, q_ref[:, sl].astype(F32) * (FOX_HEAD_DIM ** -0.5 * log2e), ones).astype(BF16)

    m_ref[...] = jnp.full(m_ref.shape, -1e30, F32)
    acc_ref[...] = jnp.zeros(acc_ref.shape, F32)

    def kv_rows(kb):
        return pl.ds(pl.multiple_of(kb * tk, tk), tk)

    def logits(h, kb, buf):
        s_ref[buf, h] = _dot_nt(qa_ref[h], ka_ref[h, kv_rows(kb), :])

    def update(h, kb, buf, masked):
        s = s_ref[buf, h]
        if masked:
            s = jnp.where(causal, s, -1e30)
        m_prev = m_ref[h]
        m_new = jnp.maximum(m_prev, jnp.max(s, axis=-1, keepdims=True))
        p = jnp.concatenate([jnp.exp2(s[:, j * LANES:(j + 1) * LANES] - m_new)
                             for j in range(tk // LANES)], axis=1)
        pv = _dot(p.astype(BF16), va_ref[h, kv_rows(kb), :])
        acc_ref[h] = jnp.exp2(m_prev - m_new) * acc_ref[h] + pv
        m_ref[h] = m_new

    for h in range(FOX_HEADS):
        logits(h, 0, 0)

    def step(j, cur):
        for h in range(FOX_HEADS):
            logits(h, j + 1, 1 - cur)
            update(h, j, cur, False)

    def body(i, carry):
        step(2 * i, 0)
        step(2 * i + 1, 1)
        return carry

    lax.fori_loop(0, qi // 2, body, 0)

    @pl.when(qi % 2 == 1)
    def _():
        step(qi - 1, 0)
        for h in range(FOX_HEADS):
            update(h, qi, 1, True)

    @pl.when(qi % 2 == 0)
    def _():
        for h in range(FOX_HEADS):
            update(h, qi, 0, True)

    for pair in range(FOX_HEADS // 2):
        a = acc_ref[2 * pair]
        b = acc_ref[2 * pair + 1]
        o = jnp.where(lane < FOX_HEAD_DIM, a / a[:, FOX_HEAD_DIM:FOX_HEAD_DIM + 1], b / b[:, 0:1])
        o_ref[:, pair * LANES:(pair + 1) * LANES] = o.astype(BF16)


def _fox(fx, cum):
    bsz, seq, _ = fx.shape
    tq = FOX_Q
    state = pltpu.VMEM((FOX_HEADS, tq, LANES), F32)
    full = pltpu.VMEM((FOX_HEADS, seq, LANES), BF16)
    return pl.pallas_call(
        _fox_kernel,
        grid=(bsz, seq // tq),
        in_specs=[
            pl.BlockSpec((None, tq, MIX_W), lambda b, i: (b, i, 0)),
            pl.BlockSpec((None, seq, MIX_W), lambda b, i: (b, 0, 1)),
            pl.BlockSpec((None, seq, MIX_W), lambda b, i: (b, 0, 2)),
            pl.BlockSpec((None, seq, FOX_HEADS), lambda b, i: (b, 0, 0)),
        ],
        out_specs=pl.BlockSpec((None, tq, MIX_W), lambda b, i: (b, i, 0)),
        out_shape=jax.ShapeDtypeStruct((bsz, seq, MIX_W), BF16),
        scratch_shapes=[pltpu.VMEM((FOX_HEADS, tq, LANES), BF16), full, full, state, state,
                        pltpu.VMEM((2, FOX_HEADS, tq, FOX_KV), F32)],
        compiler_params=_params(("arbitrary", "arbitrary")),
        name="fox_attn",
    )(fx, fx, fx, cum)


def _s5_prep_kernel(lre_ref, lim_ref, ls_ref, bre_ref, bim_ref, are_ref, aim_ref, bbr_ref, bbi_ref):
    lr, li = lre_ref[...], lim_ref[...]
    dt = jnp.exp(ls_ref[...])
    mag = jnp.exp(lr * dt)
    ar = mag * jnp.cos(li * dt)
    ai = mag * jnp.sin(li * dt)
    den = lr * lr + li * li
    cr = ((ar - 1.0) * lr + ai * li) / den
    ci = (ai * lr - (ar - 1.0) * li) / den
    br, bi = bre_ref[...], bim_ref[...]
    bbr_ref[...] = cr * br - ci * bi
    bbi_ref[...] = cr * bi + ci * br
    are_ref[...] = jnp.broadcast_to(ar, are_ref.shape)
    aim_ref[...] = jnp.broadcast_to(ai, aim_ref.shape)


def _s5_prep(lam_re, lam_im, log_step, b_re, b_im):
    row = lambda a: a.reshape(1, S5_NSTATE)
    ls = jnp.repeat(log_step, S5_STATE).reshape(1, S5_NSTATE)
    to_hp = lambda b: b.transpose(2, 0, 1).reshape(S5_GROUP_CH, S5_NSTATE)
    vec = jax.ShapeDtypeStruct((8, S5_NSTATE), F32)
    mat = jax.ShapeDtypeStruct((S5_GROUP_CH, S5_NSTATE), F32)
    return pl.pallas_call(
        _s5_prep_kernel, out_shape=[vec, vec, mat, mat], name="s5_prep",
    )(row(lam_re), row(lam_im), ls, to_hp(b_re), to_hp(b_im))


def _s5_kernel(u_ref, are_ref, aim_ref, wb_ref, wc_ref, d_ref, wglu_ref, o_ref, x_ref, st_ref, y_ref):
    @pl.when(pl.program_id(0) == 0)
    def _():
        st_ref[...] = jnp.zeros_like(st_ref)

    n = S5_NSTATE
    n_tiles = 2 * n // MXU_DIM
    total_rows = u_ref.shape[1]
    steps = total_rows // 8
    u = jnp.concatenate([u_ref[s] for s in range(MIX_W // LANES)], axis=1)
    ub = u.astype(BF16)
    width = 1024

    def project_in(rows):
        for j in range(n_tiles):
            kh = ((j % (n_tiles // 2)) * MXU_DIM // S5_STATE * S5_GROUP_CH) // MXU_DIM
            x_ref[rows, j * MXU_DIM:(j + 1) * MXU_DIM] = _dot(ub[rows, kh * MXU_DIM:(kh + 1) * MXU_DIM],
                                                              wb_ref[j])

    def scan(t0, t1, state):
        for i, c0 in enumerate(range(0, n, width)):
            re_sl = slice(c0, c0 + width)
            im_sl = slice(n + c0, n + c0 + width)
            ar = are_ref[:, re_sl]
            ai = aim_ref[:, re_sl]
            xr, xi = state[i]
            for t in range(t0, t1):
                rows = slice(t * 8, t * 8 + 8)
                xr, xi = (ar * xr - ai * xi + x_ref[rows, re_sl], ar * xi + ai * xr + x_ref[rows, im_sl])
                x_ref[rows, re_sl] = xr
                x_ref[rows, im_sl] = xi
            state[i] = (xr, xi)

    def project_out(rows):
        halves = []
        per_half = n // MXU_DIM // 2
        for nh in range(MIX_W // MXU_DIM):
            acc_re = None
            acc_im = None
            for kk in range(per_half):
                kr = nh * per_half + kk
                ki = n // MXU_DIM + kr
                pr = _dot(x_ref[rows, kr * MXU_DIM:(kr + 1) * MXU_DIM].astype(BF16), wc_ref[kr])
                pi = _dot(x_ref[rows, ki * MXU_DIM:(ki + 1) * MXU_DIM].astype(BF16), wc_ref[ki])
                acc_re = pr if acc_re is None else acc_re + pr
                acc_im = pi if acc_im is None else acc_im + pi
            halves.append(acc_re - acc_im)
        return jnp.concatenate(halves, axis=1)

    phase_rows = total_rows // S5_PHASES
    phases = [slice(p * phase_rows, (p + 1) * phase_rows) for p in range(S5_PHASES)]
    for rows in phases:
        project_in(rows)
    state = [(st_ref[:, c0:c0 + width], st_ref[:, n + c0:n + c0 + width]) for c0 in range(0, n, width)]
    outs = []
    for p, rows in enumerate(phases):
        scan(p * steps // S5_PHASES, (p + 1) * steps // S5_PHASES, state)
        outs.append(project_out(rows))
    for i, c0 in enumerate(range(0, n, width)):
        st_ref[:, c0:c0 + width] = state[i][0]
        st_ref[:, n + c0:n + c0 + width] = state[i][1]
    y = jnp.concatenate(outs, axis=0) + d_ref[...] * u
    y = 0.5 * y * (1.0 + jnp.tanh(math.sqrt(2.0 / math.pi) * (y + 0.044715 * (y * y * y))))
    y = y * _sigmoid(_dot(y.astype(BF16), wglu_ref[...]))
    for s in range(MIX_W // LANES):
        y_ref[s] = y[:, s * LANES:(s + 1) * LANES]
    bsz = o_ref.shape[0]
    for b in range(bsz):
        o_ref[b] = jnp.concatenate([y_ref[s, pl.ds(b, steps, stride=bsz), :]
                                    for s in range(MIX_W // LANES)], axis=1).astype(BF16)


def _s5(u_tm, a_re, a_im, wb, wc, d, wglu, bsz, layer):
    rows = S5_STEPS * bsz
    total = u_tm.shape[1]
    const2 = lambda i: (0, 0)
    const3 = lambda i: (0, 0, 0)
    return pl.pallas_call(
        _s5_kernel,
        grid=(total // rows,),
        in_specs=[
            pl.BlockSpec((MIX_W // LANES, rows, LANES), lambda i: (0, i, 0)),
            pl.BlockSpec(a_re.shape, const2),
            pl.BlockSpec(a_im.shape, const2),
            pl.BlockSpec(wb.shape, const3),
            pl.BlockSpec(wc.shape, const3),
            pl.BlockSpec(d.shape, const2),
            _layer_spec(wglu, layer),
        ],
        out_specs=pl.BlockSpec((bsz, S5_STEPS, MIX_W), lambda i: (0, i, 0)),
        out_shape=jax.ShapeDtypeStruct((bsz, total // bsz, MIX_W), BF16),
        scratch_shapes=[pltpu.VMEM((rows, 2 * S5_NSTATE), F32), pltpu.VMEM((8, 2 * S5_NSTATE), F32),
                        pltpu.VMEM((MIX_W // LANES, rows, LANES), F32)],
        compiler_params=_params(("arbitrary",)),
        name="s5",
    )(u_tm, a_re, a_im, wb, wc, d, wglu)


def _s5_weights(bbar_re, bbar_im, c_re, c_im):
    half = S5_NSTATE // MXU_DIM
    g_per_tile = MXU_DIM // S5_STATE
    g_per_slab = MXU_DIM // S5_GROUP_CH
    t = jnp.arange(half)[:, None, None]
    r = jnp.arange(MXU_DIM)[None, :, None]
    c = jnp.arange(MXU_DIM)[None, None, :]
    slab = (t * g_per_tile) // g_per_slab
    b_keep = slab * g_per_slab + r // S5_GROUP_CH == t * g_per_tile + c // S5_STATE
    c_keep = t * g_per_tile + r // S5_STATE == slab * g_per_slab + c // S5_GROUP_CH

    def b_tiles(bb):
        bb = bb.reshape(S5_GROUP_CH, half, MXU_DIM).transpose(1, 0, 2)
        return jnp.where(b_keep, jnp.tile(bb, (1, g_per_slab, 1)), 0.0)

    def c_tiles(cc):
        cc = cc.transpose(0, 2, 1).reshape(half, MXU_DIM, S5_GROUP_CH)
        return jnp.where(c_keep, jnp.tile(cc, (1, 1, g_per_slab)), 0.0)

    wb = jnp.concatenate([b_tiles(bbar_re), b_tiles(bbar_im)], axis=0).astype(BF16)
    wc = jnp.concatenate([c_tiles(c_re), c_tiles(c_im)], axis=0).astype(BF16)
    return wb, wc


def _merge_kernel(alpha, h_ref, ya_ref, yb_ref, yc_ref, wg_ref, wa_ref, wb_ref, wc_ref, wo_ref,
                  g_ref, b_ref, o_ref):
    for r0 in range(0, h_ref.shape[0], ROW_SUB):
        rows = slice(r0, r0 + ROW_SUB)
        h = h_ref[rows, :]
        hb = h.astype(BF16)
        merged = None
        for i, (y_ref, w_ref) in enumerate(((ya_ref, wa_ref), (yb_ref, wb_ref), (yc_ref, wc_ref))):
            gate = _sigmoid(_dot(hb, wg_ref[:, i * D_MODEL:(i + 1) * D_MODEL]))
            term = gate * _dot(y_ref[rows, :], w_ref[...])
            merged = term if merged is None else merged + term
        mix = _dot(merged.astype(BF16), wo_ref[...])
        o_ref[rows, :] = _layer_norm(alpha * h + mix, g_ref[...], b_ref[...])


def _merge(h, ya, yb, yc, wg, wa, wb, wc, wo, g, b, alpha, layer):
    bsz, seq, _ = h.shape
    tm = MERGE_ROWS
    const = lambda bb, i: (0, 0)
    row_spec = lambda w: pl.BlockSpec((None, tm, w), lambda bb, i: (bb, i, 0))
    return pl.pallas_call(
        functools.partial(_merge_kernel, alpha),
        grid=(bsz, seq // tm),
        in_specs=[
            row_spec(D_MODEL), row_spec(MIX_W), row_spec(MIX_W), row_spec(MIX_W),
            pl.BlockSpec(wg.shape, const, pipeline_mode=pl.Buffered(1)), _layer_spec(wa, layer),
            _layer_spec(wb, layer),
            _layer_spec(wc, layer), _layer_spec(wo, layer),
            pl.BlockSpec(g.shape, const), pl.BlockSpec(b.shape, const),
        ],
        out_specs=row_spec(D_MODEL),
        out_shape=jax.ShapeDtypeStruct(h.shape, F32),
        compiler_params=_params(("arbitrary", "arbitrary")),
        name="merge_ln",
    )(h, ya, yb, yc, wg, wa, wb, wc, wo, g, b)


def _ffn_kernel(alpha, h_ref, wg_ref, wu_ref, wd_ref, g_ref, b_ref, o_ref):
    for r0 in range(0, h_ref.shape[0], ROW_SUB):
        rows = slice(r0, r0 + ROW_SUB)
        h = h_ref[rows, :]
        hb = h.astype(BF16)
        acc = None
        for c0 in range(0, FFN_HIDDEN, FFN_CHUNK):
            sl = slice(c0, min(c0 + FFN_CHUNK, FFN_HIDDEN))
            a = _dot(hb, wg_ref[:, sl])
            hid = (a * _sigmoid(a)) * _dot(hb, wu_ref[:, sl])
            part = _dot(hid.astype(BF16), wd_ref[sl, :])
            acc = part if acc is None else acc + part
        o_ref[rows, :] = _layer_norm(alpha * h + acc, g_ref[...], b_ref[...])


def _ffn(h, wg, wu, wd, g, b, alpha, layer):
    bsz, seq, _ = h.shape
    tm = FFN_ROWS
    const = lambda bb, i: (0, 0)
    row_spec = pl.BlockSpec((None, tm, D_MODEL), lambda bb, i: (bb, i, 0))
    return pl.pallas_call(
        functools.partial(_ffn_kernel, alpha),
        grid=(bsz, seq // tm),
        in_specs=[row_spec, _layer_spec(wg, layer), _layer_spec(wu, layer), _layer_spec(wd, layer),
                  pl.BlockSpec(g.shape, const), pl.BlockSpec(b.shape, const)],
        out_specs=row_spec,
        out_shape=jax.ShapeDtypeStruct(h.shape, F32),
        compiler_params=_params(("arbitrary", "arbitrary")),
        name="ffn_ln",
    )(h, wg, wu, wd, g, b)


def kernel(x, w_in, hg_lower_bounds, hg_norm_w, fox_b_f, s5_lambda_re, s5_lambda_im, s5_log_step,
           s5_b_re, s5_b_im, s5_c_re, s5_c_im, s5_d, s5_w_glu, w_br_a, w_br_b, w_br_c, w_out,
           ln1_g, ln1_b, w_ffn_gate, w_ffn_up, w_ffn_down, ln2_g, ln2_b):
    depth = w_in.shape[0]
    bsz, seq, _ = x.shape
    alpha = (2 * depth) ** 0.25
    o_s5 = FOX_COL0 + 3 * MIX_W + FOX_HEADS
    o_gate = o_s5 + MIX_W
    row = lambda a: a.reshape(1, -1)

    bf = lambda a: a.astype(BF16)
    w_in_b, w_glu_b = bf(w_in), bf(s5_w_glu)
    w_a, w_b, w_c, w_o = bf(w_br_a), bf(w_br_b), bf(w_br_c), bf(w_out)
    w_fg, w_fu, w_fd = bf(w_ffn_gate), bf(w_ffn_up), bf(w_ffn_down)

    h = x
    for l in range(depth):
        ws5 = w_in_b[l, :, o_s5:o_gate]
        wgate = w_in_b[l, :, o_gate:]

        fx, ff, u_tm = _in_proj(h, w_in_b, ws5, l)

        ya = _hgrn2(h, w_in_b, hg_lower_bounds, row(hg_norm_w[l]), l)

        fz_t = ff[:, :, :FOX_HEADS].transpose(0, 2, 1)
        cum = _fox_cum(fz_t, fox_b_f[l].reshape(FOX_HEADS, 1))
        yb = _fox(fx, cum.transpose(0, 2, 1))

        a_re, a_im, bbar_re, bbar_im = _s5_prep(s5_lambda_re[l], s5_lambda_im[l], s5_log_step[l],
                                                s5_b_re[l], s5_b_im[l])
        wb, wc = _s5_weights(bbar_re, bbar_im, s5_c_re[l], s5_c_im[l])
        yc = _s5(u_tm, a_re, a_im, wb, wc, row(s5_d[l]), w_glu_b, bsz, l)

        h = _merge(h, ya, yb, yc, wgate, w_a, w_b, w_c, w_o, row(ln1_g[l]), row(ln1_b[l]), alpha, l)
        h = _ffn(h, w_fg, w_fu, w_fd, row(ln2_g[l]), row(ln2_b[l]), alpha, l)
    return h
```

```python
import functools
import math

import jax
import jax.numpy as jnp
from jax import lax
from jax.experimental import pallas as pl
from jax.experimental.pallas import tpu as pltpu

F32 = jnp.float32
BF16 = jnp.bfloat16

D_MODEL = 1024
MIX_W = 512
HG_HEADS = 4
HG_KEY = 128
HG_CHUNK = 64
HG_REF_ROW = HG_CHUNK // 2 - 1
FOX_HEADS = 8
FOX_COL0 = 4 * MIX_W
FOX_HEAD_DIM = 64
S5_GROUPS = 32
S5_STATE = 64
S5_GROUP_CH = 16
S5_NSTATE = S5_GROUPS * S5_STATE
FFN_HIDDEN = 2816
LN_EPS = 1e-5
RMS_EPS = 1e-6
EXP2_CLAMP = 115.0
LOG2E = math.log2(math.e)

LANES = 128
MXU_DIM = 256
VMEM_LIMIT = 56 * 1024 * 1024

PROJ_ROWS = 1024
PROJ_GROUP = 4
HG_ROWS = 256
FOX_Q = 512
FOX_KV = 512
S5_PHASES = 2
S5_STEPS = 64
MERGE_ROWS = 1024
FFN_ROWS = 1024
ROW_SUB = 512
FFN_CHUNK = 6 * MXU_DIM


def _dot(a, b):
    return jnp.dot(a, b, preferred_element_type=F32)


def _dot_nt(a, b):
    return lax.dot_general(a, b, (((1,), (1,)), ((), ())), preferred_element_type=F32)


def _dot_tn(a, b):
    return lax.dot_general(a, b, (((0,), (0,)), ((), ())), preferred_element_type=F32)


def _sigmoid(x):
    return 1.0 / (1.0 + jnp.exp2(x * -LOG2E))


def _layer_norm(z, g, b):
    mu = jnp.mean(z, axis=-1, keepdims=True)
    zc = z - mu
    var = jnp.mean(zc * zc, axis=-1, keepdims=True)
    return zc * lax.rsqrt(var + LN_EPS) * g + b


def _cumsum(x, axis):
    n = x.shape[axis]
    idx = lax.broadcasted_iota(jnp.int32, x.shape, axis)
    d = 1
    while d < n:
        x = x + jnp.where(idx >= d, pltpu.roll(x, d, axis=axis), 0.0)
        d *= 2
    return x


def _layer_spec(w, layer):
    return pl.BlockSpec((None,) + w.shape[1:], lambda *_: (layer, 0, 0), pipeline_mode=pl.Buffered(1))


def _params(sem):
    return pltpu.CompilerParams(dimension_semantics=sem, vmem_limit_bytes=VMEM_LIMIT)


def _fox_lanes(h, lane):
    if h % 2 == 0:
        return lane < FOX_HEAD_DIM, FOX_HEAD_DIM
    return lane >= FOX_HEAD_DIM, 0


def _proj_kernel(x_ref, wq_ref, wk_ref, wv_ref, wff_ref, bf_ref, ws5_ref, q_ref, ka_ref, va_ref, s5_ref,
                 cum_ref):
    bsz, tt, _ = x_ref.shape

    @pl.when(pl.program_id(0) == 0)
    def _():
        cum_ref[...] = jnp.zeros_like(cum_ref)

    lane = lax.broadcasted_iota(jnp.int32, (1, LANES), 1)
    for b0 in range(0, bsz, PROJ_GROUP):
        nb = PROJ_GROUP
        rows = nb * tt
        xb = x_ref[b0:b0 + nb].reshape(rows, D_MODEL).astype(BF16)
        k = _dot(xb, wk_ref[...])
        v = _dot(xb, wv_ref[...])
        z = _dot(xb, wff_ref[...]) + bf_ref[...]
        q_ref[b0:b0 + nb] = _dot(xb, wq_ref[...]).astype(BF16).reshape(nb, tt, MIX_W)
        u = _dot(xb, ws5_ref[...])

        logsig = jnp.minimum(z, 0.0) - jnp.log(1.0 + jnp.exp(-jnp.abs(z)))
        t_idx = lax.broadcasted_iota(jnp.int32, (rows, LANES), 0) % tt
        cum = logsig
        d = 1
        while d < tt:
            cum = cum + jnp.where(t_idx >= d, pltpu.roll(cum, d, axis=0), 0.0)
            d *= 2
        carry = cum_ref[b0:b0 + nb, :]
        cum = jnp.concatenate([cum[b * tt:(b + 1) * tt] + carry[b:b + 1] for b in range(nb)], axis=0)
        cum_ref[b0:b0 + nb, :] = jnp.concatenate([cum[(b + 1) * tt - 1:(b + 1) * tt] for b in range(nb)], axis=0)

        bias = cum * (-LOG2E)
        for h in range(FOX_HEADS):
            sl = slice((h // 2) * LANES, (h // 2 + 1) * LANES)
            own, aux = _fox_lanes(h, lane)
            bh = jnp.broadcast_to(bias[:, h:h + 1], (rows, LANES))
            hi = bh.astype(BF16).astype(F32)
            mid = (bh - hi).astype(BF16).astype(F32)
            lo = bh - hi - mid
            extra = jnp.where(lane == aux, hi, jnp.where(lane == aux + 1, mid,
                                                         jnp.where(lane == aux + 2, lo, 0.0)))
            ka = jnp.where(own, k[:, sl], extra).astype(BF16)
            va = jnp.where(own, v[:, sl], jnp.where(lane == aux, 1.0, 0.0)).astype(BF16)
            for b in range(nb):
                ka_ref[b0 + b, h] = ka[b * tt:(b + 1) * tt]
                va_ref[b0 + b, h] = va[b * tt:(b + 1) * tt]

        for b in range(nb):
            for s in range(MIX_W // LANES):
                s5_ref[s, pl.ds(b0 + b, tt, stride=bsz), :] = u[b * tt:(b + 1) * tt, s * LANES:(s + 1) * LANES]


def _in_proj(h, w_all, bf, ws5, layer):
    bsz, seq, _ = h.shape
    tt = PROJ_ROWS // bsz
    fox_spec = lambda j: pl.BlockSpec((None, D_MODEL, MIX_W), lambda i: (layer, 0, FOX_COL0 // MIX_W + j))
    head_spec = pl.BlockSpec((bsz, FOX_HEADS, tt, LANES), lambda i: (0, 0, i, 0))
    head_shape = jax.ShapeDtypeStruct((bsz, FOX_HEADS, seq, LANES), BF16)
    return pl.pallas_call(
        _proj_kernel,
        grid=(seq // tt,),
        in_specs=[
            pl.BlockSpec((bsz, tt, D_MODEL), lambda i: (0, i, 0)),
            fox_spec(0), fox_spec(1), fox_spec(2),
            pl.BlockSpec((None, D_MODEL, LANES), lambda i: (layer, 0, (FOX_COL0 + 3 * MIX_W) // LANES)),
            pl.BlockSpec((None, 1, LANES), lambda i: (layer, 0, 0)),
            pl.BlockSpec(ws5.shape, lambda i: (0, 0)),
        ],
        out_specs=[
            pl.BlockSpec((bsz, tt, MIX_W), lambda i: (0, i, 0)),
            head_spec, head_spec,
            pl.BlockSpec((MIX_W // LANES, bsz * tt, LANES), lambda i: (0, i, 0)),
        ],
        out_shape=[
            jax.ShapeDtypeStruct((bsz, seq, MIX_W), BF16),
            head_shape, head_shape,
            jax.ShapeDtypeStruct((MIX_W // LANES, seq * bsz, LANES), F32),
        ],
        scratch_shapes=[pltpu.VMEM((bsz, LANES), F32)],
        compiler_params=_params(("arbitrary",)),
        name="in_proj",
    )(h, w_all, w_all, w_all, w_all, bf, ws5)


def _hgrn2_kernel(layer, h_ref, hn_ref, w_ref, lbp_ref, nw_ref, o_ref, st_ref, hg0_ref, hg1_ref):
    i = pl.program_id(1)

    @pl.when(i == 0)
    def _():
        st_ref[...] = jnp.zeros_like(st_ref)
        hg0_ref[...] = _dot(h_ref[...].astype(BF16), w_ref[...])

    lbp = lbp_ref[...]
    e = jnp.exp(lbp - jnp.max(lbp, axis=0, keepdims=True))
    sm = e / jnp.sum(e, axis=0, keepdims=True)
    cum = sm[0:1]
    for j in range(1, layer + 1):
        cum = cum + sm[j:j + 1]
    lb = cum - sm[0:1]

    nw = nw_ref[...]
    c = HG_CHUNK
    n_chunks = h_ref.shape[0] // c
    tri = (lax.broadcasted_iota(jnp.int32, (c, c), 0) >= lax.broadcasted_iota(jnp.int32, (c, c), 1))

    def gates(ci, hg_ref):
        rows = slice(ci * c, (ci + 1) * c)
        q = hg_ref[rows, 0:MIX_W]
        fz = hg_ref[rows, MIX_W:2 * MIX_W]
        f = lb + (1.0 - lb) * _sigmoid(fz)
        k = 1.0 - f
        g = _cumsum(jnp.log2(f), 0)
        g_ref = g[HG_REF_ROW:HG_REF_ROW + 1]
        g_last = g[c - 1:c]
        d_ref = g - g_ref
        q_rel = (q * jnp.exp2(jnp.minimum(d_ref, EXP2_CLAMP))).astype(BF16)
        k_rel = (k * jnp.exp2(jnp.minimum(-d_ref, EXP2_CLAMP))).astype(BF16)
        q_in = (q * jnp.exp2(g)).astype(BF16)
        k_end = (k * jnp.exp2(g_last - g)).astype(BF16)
        s_decay = jnp.exp2(g_last)
        return q_rel, k_rel, q_in, k_end, s_decay

    def mix(ci, hg_ref, operands):
        rows = slice(ci * c, (ci + 1) * c)
        q_rel, k_rel, q_in, k_end, s_decay = operands
        v = hg_ref[rows, 2 * MIX_W:3 * MIX_W]
        gate = hg_ref[rows, 3 * MIX_W:4 * MIX_W]
        for h in range(HG_HEADS):
            sl = slice(h * HG_KEY, (h + 1) * HG_KEY)
            st = st_ref[h]
            v_t = v[:, sl].T.astype(BF16)
            scores = jnp.where(tri, _dot_nt(q_rel[:, sl], k_rel[:, sl]), 0.0)
            lhs = jnp.concatenate([q_in[:, sl], scores.astype(BF16)], axis=1)
            rhs_t = jnp.concatenate([st.astype(BF16), v_t], axis=1)
            o = _dot_nt(lhs, rhs_t)
            st_ref[h] = st * s_decay[:, sl] + _dot(v_t, k_end[:, sl])
            o = o * lax.rsqrt(jnp.mean(o * o, axis=-1, keepdims=True) + RMS_EPS) * nw
            gh = gate[:, sl]
            o_ref[rows, sl] = (o * (gh * _sigmoid(gh))).astype(BF16)

    def tile(cur_ref, nxt_ref):
        hn = hn_ref[...].astype(BF16)
        cols = w_ref.shape[1] // n_chunks
        operands = gates(0, cur_ref)
        for ci in range(n_chunks):
            ahead = gates(ci + 1, cur_ref) if ci + 1 < n_chunks else None
            mix(ci, cur_ref, operands)
            nxt_ref[:, ci * cols:(ci + 1) * cols] = _dot(hn, w_ref[:, ci * cols:(ci + 1) * cols])
            operands = ahead

    @pl.when(i % 2 == 0)
    def _():
        tile(hg0_ref, hg1_ref)

    @pl.when(i % 2 == 1)
    def _():
        tile(hg1_ref, hg0_ref)


def _hgrn2(h, w_all, lbp, nw, layer):
    bsz, seq, _ = h.shape
    tr = HG_ROWS
    n_tiles = seq // tr
    hg = pltpu.VMEM((tr, 4 * MIX_W), F32)
    return pl.pallas_call(
        functools.partial(_hgrn2_kernel, layer),
        grid=(bsz, n_tiles),
        in_specs=[
            pl.BlockSpec((None, tr, D_MODEL), lambda b, i: (b, i, 0)),
            pl.BlockSpec((None, tr, D_MODEL), lambda b, i: (b, jnp.minimum(i + 1, n_tiles - 1), 0)),
            pl.BlockSpec((None, D_MODEL, 4 * MIX_W), lambda b, i: (layer, 0, 0)),
            pl.BlockSpec(lbp.shape, lambda b, i: (0, 0)),
            pl.BlockSpec(nw.shape, lambda b, i: (0, 0)),
        ],
        out_specs=pl.BlockSpec((None, tr, MIX_W), lambda b, i: (b, i, 0)),
        out_shape=jax.ShapeDtypeStruct((bsz, seq, MIX_W), BF16),
        scratch_shapes=[pltpu.VMEM((HG_HEADS, HG_KEY, HG_KEY), F32), hg, hg],
        compiler_params=_params(("arbitrary", "arbitrary")),
        name="hgrn2",
    )(h, h, w_all, lbp, nw)


def _fox_kernel(q_ref, ka_ref, va_ref, o_ref, qa_ref, m_ref, acc_ref, s_ref):
    qi = pl.program_id(1)
    tq, tk = FOX_Q, FOX_KV
    lane = lax.broadcasted_iota(jnp.int32, (1, LANES), 1)
    causal = (lax.broadcasted_iota(jnp.int32, (tq, tk), 0) >= lax.broadcasted_iota(jnp.int32, (tq, tk), 1))

    for h in range(FOX_HEADS):
        sl = slice((h // 2) * LANES, (h // 2 + 1) * LANES)
        own, aux = _fox_lanes(h, lane)
        ones = jnp.where((lane >= aux) & (lane < aux + 3), 1.0, 0.0)
        qa_ref[h] = jnp.where(own, q_ref[:, sl].astype(F32) * (FOX_HEAD_DIM ** -0.5 * LOG2E), ones).astype(BF16)

    m_ref[...] = jnp.full(m_ref.shape, -1e30, F32)
    acc_ref[...] = jnp.zeros(acc_ref.shape, F32)

    def kv_rows(kb):
        return pl.ds(pl.multiple_of(kb * tk, tk), tk)

    def logits(h, kb, buf):
        s_ref[buf, h] = _dot_nt(qa_ref[h], ka_ref[h, kv_rows(kb), :])

    def update(h, kb, buf, masked):
        s = s_ref[buf, h]
        if masked:
            s = jnp.where(causal, s, -1e30)
        m_prev = m_ref[h]
        m_new = jnp.maximum(m_prev, jnp.max(s, axis=-1, keepdims=True))
        p = jnp.concatenate([jnp.exp2(s[:, j * LANES:(j + 1) * LANES] - m_new)
                             for j in range(tk // LANES)], axis=1)
        pv = _dot(p.astype(BF16), va_ref[h, kv_rows(kb), :])
        acc_ref[h] = jnp.exp2(m_prev - m_new) * acc_ref[h] + pv
        m_ref[h] = m_new

    for h in range(FOX_HEADS):
        logits(h, 0, 0)

    def step(j, cur):
        for h in range(FOX_HEADS):
            logits(h, j + 1, 1 - cur)
            update(h, j, cur, False)

    def body(i, carry):
        step(2 * i, 0)
        step(2 * i + 1, 1)
        return carry

    lax.fori_loop(0, qi // 2, body, 0)

    @pl.when(qi % 2 == 1)
    def _():
        step(qi - 1, 0)
        for h in range(FOX_HEADS):
            update(h, qi, 1, True)

    @pl.when(qi % 2 == 0)
    def _():
        for h in range(FOX_HEADS):
            update(h, qi, 0, True)

    for pair in range(FOX_HEADS // 2):
        a = acc_ref[2 * pair]
        b = acc_ref[2 * pair + 1]
        o = jnp.where(lane < FOX_HEAD_DIM, a / a[:, FOX_HEAD_DIM:FOX_HEAD_DIM + 1], b / b[:, 0:1])
        o_ref[:, pair * LANES:(pair + 1) * LANES] = o.astype(BF16)


def _fox(q, ka, va):
    bsz, seq, _ = q.shape
    tq = FOX_Q
    state = pltpu.VMEM((FOX_HEADS, tq, LANES), F32)
    full_spec = pl.BlockSpec((None, FOX_HEADS, seq, LANES), lambda b, i: (b, 0, 0, 0))
    return pl.pallas_call(
        _fox_kernel,
        grid=(bsz, seq // tq),
        in_specs=[pl.BlockSpec((None, tq, MIX_W), lambda b, i: (b, i, 0)), full_spec, full_spec],
        out_specs=pl.BlockSpec((None, tq, MIX_W), lambda b, i: (b, i, 0)),
        out_shape=jax.ShapeDtypeStruct((bsz, seq, MIX_W), BF16),
        scratch_shapes=[pltpu.VMEM((FOX_HEADS, tq, LANES), BF16), state, state,
                        pltpu.VMEM((2, FOX_HEADS, tq, FOX_KV), F32)],
        compiler_params=_params(("arbitrary", "arbitrary")),
        name="fox_attn",
    )(q, ka, va)


def _s5_prep_kernel(lre_ref, lim_ref, ls_ref, bre_ref, bim_ref, are_ref, aim_ref, bbr_ref, bbi_ref):
    lr, li = lre_ref[...], lim_ref[...]
    dt = jnp.exp(ls_ref[...])
    mag = jnp.exp(lr * dt)
    ar = mag * jnp.cos(li * dt)
    ai = mag * jnp.sin(li * dt)
    den = lr * lr + li * li
    cr = ((ar - 1.0) * lr + ai * li) / den
    ci = (ai * lr - (ar - 1.0) * li) / den
    br, bi = bre_ref[...], bim_ref[...]
    bbr_ref[...] = cr * br - ci * bi
    bbi_ref[...] = cr * bi + ci * br
    are_ref[...] = jnp.broadcast_to(ar, are_ref.shape)
    aim_ref[...] = jnp.broadcast_to(ai, aim_ref.shape)


def _s5_prep(lam_re, lam_im, log_step, b_re, b_im):
    row = lambda a: a.reshape(1, S5_NSTATE)
    ls = jnp.repeat(log_step, S5_STATE).reshape(1, S5_NSTATE)
    to_hp = lambda b: b.transpose(2, 0, 1).reshape(S5_GROUP_CH, S5_NSTATE)
    vec = jax.ShapeDtypeStruct((8, S5_NSTATE), F32)
    mat = jax.ShapeDtypeStruct((S5_GROUP_CH, S5_NSTATE), F32)
    return pl.pallas_call(
        _s5_prep_kernel, out_shape=[vec, vec, mat, mat], name="s5_prep",
    )(row(lam_re), row(lam_im), ls, to_hp(b_re), to_hp(b_im))


def _s5_kernel(u_ref, are_ref, aim_ref, wb_ref, wc_ref, d_ref, wglu_ref, o_ref, x_ref, st_ref, y_ref):
    @pl.when(pl.program_id(0) == 0)
    def _():
        st_ref[...] = jnp.zeros_like(st_ref)

    n = S5_NSTATE
    n_tiles = 2 * n // MXU_DIM
    total_rows = u_ref.shape[1]
    steps = total_rows // 8
    u = jnp.concatenate([u_ref[s] for s in range(MIX_W // LANES)], axis=1)
    ub = u.astype(BF16)
    width = 1024

    def project_in(rows):
        for j in range(n_tiles):
            kh = ((j % (n_tiles // 2)) * MXU_DIM // S5_STATE * S5_GROUP_CH) // MXU_DIM
            x_ref[rows, j * MXU_DIM:(j + 1) * MXU_DIM] = _dot(ub[rows, kh * MXU_DIM:(kh + 1) * MXU_DIM],
                                                              wb_ref[j])

    def scan(t0, t1, state):
        for i, c0 in enumerate(range(0, n, width)):
            re_sl = slice(c0, c0 + width)
            im_sl = slice(n + c0, n + c0 + width)
            ar = are_ref[:, re_sl]
            ai = aim_ref[:, re_sl]
            xr, xi = state[i]
            for t in range(t0, t1):
                rows = slice(t * 8, t * 8 + 8)
                xr, xi = (ar * xr - ai * xi + x_ref[rows, re_sl], ar * xi + ai * xr + x_ref[rows, im_sl])
                x_ref[rows, re_sl] = xr
                x_ref[rows, im_sl] = xi
            state[i] = (xr, xi)

    def project_out(rows):
        halves = []
        per_half = n // MXU_DIM // 2
        for nh in range(MIX_W // MXU_DIM):
            acc_re = None
            acc_im = None
            for kk in range(per_half):
                kr = nh * per_half + kk
                ki = n // MXU_DIM + kr
                pr = _dot(x_ref[rows, kr * MXU_DIM:(kr + 1) * MXU_DIM].astype(BF16), wc_ref[kr])
                pi = _dot(x_ref[rows, ki * MXU_DIM:(ki + 1) * MXU_DIM].astype(BF16), wc_ref[ki])
                acc_re = pr if acc_re is None else acc_re + pr
                acc_im = pi if acc_im is None else acc_im + pi
            halves.append(acc_re - acc_im)
        return jnp.concatenate(halves, axis=1)

    phase_rows = total_rows // S5_PHASES
    phases = [slice(p * phase_rows, (p + 1) * phase_rows) for p in range(S5_PHASES)]
    for rows in phases:
        project_in(rows)
    state = [(st_ref[:, c0:c0 + width], st_ref[:, n + c0:n + c0 + width]) for c0 in range(0, n, width)]
    outs = []
    for p, rows in enumerate(phases):
        scan(p * steps // S5_PHASES, (p + 1) * steps // S5_PHASES, state)
        outs.append(project_out(rows))
    for i, c0 in enumerate(range(0, n, width)):
        st_ref[:, c0:c0 + width] = state[i][0]
        st_ref[:, n + c0:n + c0 + width] = state[i][1]
    y = jnp.concatenate(outs, axis=0) + d_ref[...] * u
    y = 0.5 * y * (1.0 + jnp.tanh(math.sqrt(2.0 / math.pi) * (y + 0.044715 * (y * y * y))))
    y = y * _sigmoid(_dot(y.astype(BF16), wglu_ref[...]))
    for s in range(MIX_W // LANES):
        y_ref[s] = y[:, s * LANES:(s + 1) * LANES]
    bsz = o_ref.shape[0]
    for b in range(bsz):
        o_ref[b] = jnp.concatenate([y_ref[s, pl.ds(b, steps, stride=bsz), :]
                                    for s in range(MIX_W // LANES)], axis=1).astype(BF16)


def _s5(u_tm, a_re, a_im, wb, wc, d, wglu, bsz, layer):
    rows = S5_STEPS * bsz
    total = u_tm.shape[1]
    const2 = lambda i: (0, 0)
    const3 = lambda i: (0, 0, 0)
    return pl.pallas_call(
        _s5_kernel,
        grid=(total // rows,),
        in_specs=[
            pl.BlockSpec((MIX_W // LANES, rows, LANES), lambda i: (0, i, 0)),
            pl.BlockSpec(a_re.shape, const2),
            pl.BlockSpec(a_im.shape, const2),
            pl.BlockSpec(wb.shape, const3),
            pl.BlockSpec(wc.shape, const3),
            pl.BlockSpec(d.shape, const2),
            _layer_spec(wglu, layer),
        ],
        out_specs=pl.BlockSpec((bsz, S5_STEPS, MIX_W), lambda i: (0, i, 0)),
        out_shape=jax.ShapeDtypeStruct((bsz, total // bsz, MIX_W), BF16),
        scratch_shapes=[pltpu.VMEM((rows, 2 * S5_NSTATE), F32), pltpu.VMEM((8, 2 * S5_NSTATE), F32),
                        pltpu.VMEM((MIX_W // LANES, rows, LANES), F32)],
        compiler_params=_params(("arbitrary",)),
        name="s5",
    )(u_tm, a_re, a_im, wb, wc, d, wglu)


def _s5_weights(bbar_re, bbar_im, c_re, c_im):
    half = S5_NSTATE // MXU_DIM
    g_per_tile = MXU_DIM // S5_STATE
    g_per_slab = MXU_DIM // S5_GROUP_CH
    t = jnp.arange(half)[:, None, None]
    r = jnp.arange(MXU_DIM)[None, :, None]
    c = jnp.arange(MXU_DIM)[None, None, :]
    slab = (t * g_per_tile) // g_per_slab
    b_keep = slab * g_per_slab + r // S5_GROUP_CH == t * g_per_tile + c // S5_STATE
    c_keep = t * g_per_tile + r // S5_STATE == slab * g_per_slab + c // S5_GROUP_CH

    def b_tiles(bb):
        bb = bb.reshape(S5_GROUP_CH, half, MXU_DIM).transpose(1, 0, 2)
        return jnp.where(b_keep, jnp.tile(bb, (1, g_per_slab, 1)), 0.0)

    def c_tiles(cc):
        cc = cc.transpose(0, 2, 1).reshape(half, MXU_DIM, S5_GROUP_CH)
        return jnp.where(c_keep, jnp.tile(cc, (1, 1, g_per_slab)), 0.0)

    wb = jnp.concatenate([b_tiles(bbar_re), b_tiles(bbar_im)], axis=0).astype(BF16)
    wc = jnp.concatenate([c_tiles(c_re), c_tiles(c_im)], axis=0).astype(BF16)
    return wb, wc


def _merge_kernel(alpha, h_ref, ya_ref, yb_ref, yc_ref, wg_ref, wa_ref, wb_ref, wc_ref, wo_ref,
                  g_ref, b_ref, o_ref):
    for r0 in range(0, h_ref.shape[0], ROW_SUB):
        rows = slice(r0, r0 + ROW_SUB)
        h = h_ref[rows, :]
        hb = h.astype(BF16)
        merged = None
        for i, (y_ref, w_ref) in enumerate(((ya_ref, wa_ref), (yb_ref, wb_ref), (yc_ref, wc_ref))):
            gate = _sigmoid(_dot(hb, wg_ref[:, i * D_MODEL:(i + 1) * D_MODEL]))
            term = gate * _dot(y_ref[rows, :], w_ref[...])
            merged = term if merged is None else merged + term
        mix = _dot(merged.astype(BF16), wo_ref[...])
        o_ref[rows, :] = _layer_norm(alpha * h + mix, g_ref[...], b_ref[...])


def _merge(h, ya, yb, yc, wg, wa, wb, wc, wo, g, b, alpha, layer):
    bsz, seq, _ = h.shape
    tm = MERGE_ROWS
    const = lambda bb, i: (0, 0)
    row_spec = lambda w: pl.BlockSpec((None, tm, w), lambda bb, i: (bb, i, 0))
    return pl.pallas_call(
        functools.partial(_merge_kernel, alpha),
        grid=(bsz, seq // tm),
        in_specs=[
            row_spec(D_MODEL), row_spec(MIX_W), row_spec(MIX_W), row_spec(MIX_W),
            pl.BlockSpec(wg.shape, const, pipeline_mode=pl.Buffered(1)), _layer_spec(wa, layer),
            _layer_spec(wb, layer),
            _layer_spec(wc, layer), _layer_spec(wo, layer),
            pl.BlockSpec(g.shape, const), pl.BlockSpec(b.shape, const),
        ],
        out_specs=row_spec(D_MODEL),
        out_shape=jax.ShapeDtypeStruct(h.shape, F32),
        compiler_params=_params(("arbitrary", "arbitrary")),
        name="merge_ln",
    )(h, ya, yb, yc, wg, wa, wb, wc, wo, g, b)


def _ffn_kernel(alpha, h_ref, wg_ref, wu_ref, wd_ref, g_ref, b_ref, o_ref):
    for r0 in range(0, h_ref.shape[0], ROW_SUB):
        rows = slice(r0, r0 + ROW_SUB)
        h = h_ref[rows, :]
        hb = h.astype(BF16)
        acc = None
        for c0 in range(0, FFN_HIDDEN, FFN_CHUNK):
            sl = slice(c0, min(c0 + FFN_CHUNK, FFN_HIDDEN))
            a = _dot(hb, wg_ref[:, sl])
            hid = (a * _sigmoid(a)) * _dot(hb, wu_ref[:, sl])
            part = _dot(hid.astype(BF16), wd_ref[sl, :])
            acc = part if acc is None else acc + part
        o_ref[rows, :] = _layer_norm(alpha * h + acc, g_ref[...], b_ref[...])


def _ffn(h, wg, wu, wd, g, b, alpha, layer):
    bsz, seq, _ = h.shape
    tm = FFN_ROWS
    const = lambda bb, i: (0, 0)
    row_spec = pl.BlockSpec((None, tm, D_MODEL), lambda bb, i: (bb, i, 0))
    return pl.pallas_call(
        functools.partial(_ffn_kernel, alpha),
        grid=(bsz, seq // tm),
        in_specs=[row_spec, _layer_spec(wg, layer), _layer_spec(wu, layer), _layer_spec(wd, layer),
                  pl.BlockSpec(g.shape, const), pl.BlockSpec(b.shape, const)],
        out_specs=row_spec,
        out_shape=jax.ShapeDtypeStruct(h.shape, F32),
        compiler_params=_params(("arbitrary", "arbitrary")),
        name="ffn_ln",
    )(h, wg, wu, wd, g, b)


def kernel(x, w_in, hg_lower_bounds, hg_norm_w, fox_b_f, s5_lambda_re, s5_lambda_im, s5_log_step,
           s5_b_re, s5_b_im, s5_c_re, s5_c_im, s5_d, s5_w_glu, w_br_a, w_br_b, w_br_c, w_out,
           ln1_g, ln1_b, w_ffn_gate, w_ffn_up, w_ffn_down, ln2_g, ln2_b):
    depth = w_in.shape[0]
    bsz, seq, _ = x.shape
    alpha = (2 * depth) ** 0.25
    o_s5 = FOX_COL0 + 3 * MIX_W + FOX_HEADS
    o_gate = o_s5 + MIX_W
    row = lambda a: a.reshape(1, -1)

    bf = lambda a: a.astype(BF16)
    w_in_b, w_glu_b = bf(w_in), bf(s5_w_glu)
    w_a, w_b, w_c, w_o = bf(w_br_a), bf(w_br_b), bf(w_br_c), bf(w_out)
    w_fg, w_fu, w_fd = bf(w_ffn_gate), bf(w_ffn_up), bf(w_ffn_down)
    fox_bias = jnp.pad(fox_b_f, ((0, 0), (0, LANES - FOX_HEADS))).reshape(depth, 1, LANES)

    h = x
    for l in range(depth):
        ws5 = w_in_b[l, :, o_s5:o_gate]
        wgate = w_in_b[l, :, o_gate:]

        q, ka, va, u_tm = _in_proj(h, w_in_b, fox_bias, ws5, l)

        ya = _hgrn2(h, w_in_b, hg_lower_bounds, row(hg_norm_w[l]), l)
        yb = _fox(q, ka, va)

        a_re, a_im, bbar_re, bbar_im = _s5_prep(s5_lambda_re[l], s5_lambda_im[l], s5_log_step[l],
                                                s5_b_re[l], s5_b_im[l])
        wb, wc = _s5_weights(bbar_re, bbar_im, s5_c_re[l], s5_c_im[l])
        yc = _s5(u_tm, a_re, a_im, wb, wc, row(s5_d[l]), w_glu_b, bsz, l)

        h = _merge(h, ya, yb, yc, wgate, w_a, w_b, w_c, w_o, row(ln1_g[l]), row(ln1_b[l]), alpha, l)
        h = _ffn(h, w_fg, w_fu, w_fd, row(ln2_g[l]), row(ln2_b[l]), alpha, l)
    return h
```

```python
import functools
import math

import jax
import jax.numpy as jnp
from jax import lax
from jax.experimental import pallas as pl
from jax.experimental.pallas import tpu as pltpu

F32 = jnp.float32
BF16 = jnp.bfloat16

D_MODEL = 1024
MIX_W = 512
HG_HEADS = 4
HG_KEY = 128
HG_CHUNK = 64
HG_REF_ROW = HG_CHUNK // 2 - 1
FOX_HEADS = 8
FOX_COL0 = 4 * MIX_W
FOX_HEAD_DIM = 64
S5_GROUPS = 32
S5_STATE = 64
S5_GROUP_CH = 16
S5_NSTATE = S5_GROUPS * S5_STATE
FFN_HIDDEN = 2816
LN_EPS = 1e-5
RMS_EPS = 1e-6
EXP2_CLAMP = 115.0
LOG2E = math.log2(math.e)

LANES = 128
MXU_DIM = 256
VMEM_LIMIT = 56 * 1024 * 1024

PROJ_ROWS = 1024
PROJ_GROUP = 4
HG_ROWS = 256
FOX_Q = 512
FOX_KV = 512
S5_PHASES = 4
S5_STEPS = 128
MERGE_ROWS = 1024
FFN_ROWS = 1024
ROW_SUB = 512
FFN_CHUNK = 6 * MXU_DIM


def _dot(a, b):
    return jnp.dot(a, b, preferred_element_type=F32)


def _dot_nt(a, b):
    return lax.dot_general(a, b, (((1,), (1,)), ((), ())), preferred_element_type=F32)


def _dot_tn(a, b):
    return lax.dot_general(a, b, (((0,), (0,)), ((), ())), preferred_element_type=F32)


def _sigmoid(x):
    return 1.0 / (1.0 + jnp.exp2(x * -LOG2E))


def _layer_norm(z, g, b):
    mu = jnp.mean(z, axis=-1, keepdims=True)
    zc = z - mu
    var = jnp.mean(zc * zc, axis=-1, keepdims=True)
    return zc * lax.rsqrt(var + LN_EPS) * g + b


def _cumsum(x, axis):
    n = x.shape[axis]
    idx = lax.broadcasted_iota(jnp.int32, x.shape, axis)
    d = 1
    while d < n:
        x = x + jnp.where(idx >= d, pltpu.roll(x, d, axis=axis), 0.0)
        d *= 2
    return x


def _layer_spec(w, layer):
    return pl.BlockSpec((None,) + w.shape[1:], lambda *_: (layer, 0, 0), pipeline_mode=pl.Buffered(1))


def _params(sem):
    return pltpu.CompilerParams(dimension_semantics=sem, vmem_limit_bytes=VMEM_LIMIT)


def _fox_lanes(h, lane):
    if h % 2 == 0:
        return lane < FOX_HEAD_DIM, FOX_HEAD_DIM
    return lane >= FOX_HEAD_DIM, 0


def _proj_kernel(x_ref, wq_ref, wk_ref, wv_ref, wff_ref, bf_ref, ws5_ref, q_ref, ka_ref, va_ref, s5_ref,
                 cum_ref):
    bsz, tt, _ = x_ref.shape

    @pl.when(pl.program_id(0) == 0)
    def _():
        cum_ref[...] = jnp.zeros_like(cum_ref)

    lane = lax.broadcasted_iota(jnp.int32, (1, LANES), 1)
    for b0 in range(0, bsz, PROJ_GROUP):
        nb = PROJ_GROUP
        rows = nb * tt
        xb = x_ref[b0:b0 + nb].reshape(rows, D_MODEL).astype(BF16)
        k = _dot(xb, wk_ref[...])
        v = _dot(xb, wv_ref[...])
        z = _dot(xb, wff_ref[...]) + bf_ref[...]
        q_ref[b0:b0 + nb] = _dot(xb, wq_ref[...]).astype(BF16).reshape(nb, tt, MIX_W)
        u = _dot(xb, ws5_ref[...])

        logsig = jnp.minimum(z, 0.0) - jnp.log(1.0 + jnp.exp(-jnp.abs(z)))
        t_idx = lax.broadcasted_iota(jnp.int32, (rows, LANES), 0) % tt
        cum = logsig
        d = 1
        while d < tt:
            cum = cum + jnp.where(t_idx >= d, pltpu.roll(cum, d, axis=0), 0.0)
            d *= 2
        carry = cum_ref[b0:b0 + nb, :]
        cum = jnp.concatenate([cum[b * tt:(b + 1) * tt] + carry[b:b + 1] for b in range(nb)], axis=0)
        cum_ref[b0:b0 + nb, :] = jnp.concatenate([cum[(b + 1) * tt - 1:(b + 1) * tt] for b in range(nb)], axis=0)

        bias = cum * (-LOG2E)
        for h in range(FOX_HEADS):
            sl = slice((h // 2) * LANES, (h // 2 + 1) * LANES)
            own, aux = _fox_lanes(h, lane)
            bh = jnp.broadcast_to(bias[:, h:h + 1], (rows, LANES))
            hi = bh.astype(BF16).astype(F32)
            mid = (bh - hi).astype(BF16).astype(F32)
            lo = bh - hi - mid
            extra = jnp.where(lane == aux, hi, jnp.where(lane == aux + 1, mid,
                                                         jnp.where(lane == aux + 2, lo, 0.0)))
            ka = jnp.where(own, k[:, sl], extra).astype(BF16)
            va = jnp.where(own, v[:, sl], jnp.where(lane == aux, 1.0, 0.0)).astype(BF16)
            for b in range(nb):
                ka_ref[b0 + b, h] = ka[b * tt:(b + 1) * tt]
                va_ref[b0 + b, h] = va[b * tt:(b + 1) * tt]

        for b in range(nb):
            for s in range(MIX_W // LANES):
                s5_ref[s, pl.ds(b0 + b, tt, stride=bsz), :] = u[b * tt:(b + 1) * tt, s * LANES:(s + 1) * LANES]


def _in_proj(h, w_all, bf, ws5, layer):
    bsz, seq, _ = h.shape
    tt = PROJ_ROWS // bsz
    fox_spec = lambda j: pl.BlockSpec((None, D_MODEL, MIX_W), lambda i: (layer, 0, FOX_COL0 // MIX_W + j))
    head_spec = pl.BlockSpec((bsz, FOX_HEADS, tt, LANES), lambda i: (0, 0, i, 0))
    head_shape = jax.ShapeDtypeStruct((bsz, FOX_HEADS, seq, LANES), BF16)
    return pl.pallas_call(
        _proj_kernel,
        grid=(seq // tt,),
        in_specs=[
            pl.BlockSpec((bsz, tt, D_MODEL), lambda i: (0, i, 0)),
            fox_spec(0), fox_spec(1), fox_spec(2),
            pl.BlockSpec((None, D_MODEL, LANES), lambda i: (layer, 0, (FOX_COL0 + 3 * MIX_W) // LANES)),
            pl.BlockSpec((None, 1, LANES), lambda i: (layer, 0, 0)),
            pl.BlockSpec(ws5.shape, lambda i: (0, 0)),
        ],
        out_specs=[
            pl.BlockSpec((bsz, tt, MIX_W), lambda i: (0, i, 0)),
            head_spec, head_spec,
            pl.BlockSpec((MIX_W // LANES, bsz * tt, LANES), lambda i: (0, i, 0)),
        ],
        out_shape=[
            jax.ShapeDtypeStruct((bsz, seq, MIX_W), BF16),
            head_shape, head_shape,
            jax.ShapeDtypeStruct((MIX_W // LANES, seq * bsz, LANES), F32),
        ],
        scratch_shapes=[pltpu.VMEM((bsz, LANES), F32)],
        compiler_params=_params(("arbitrary",)),
        name="in_proj",
    )(h, w_all, w_all, w_all, w_all, bf, ws5)


def _hgrn2_kernel(layer, h_ref, hn_ref, w_ref, lbp_ref, nw_ref, o_ref, st_ref, hg0_ref, hg1_ref):
    i = pl.program_id(1)

    @pl.when(i == 0)
    def _():
        st_ref[...] = jnp.zeros_like(st_ref)
        hg0_ref[...] = _dot(h_ref[...].astype(BF16), w_ref[...])

    lbp = lbp_ref[...]
    e = jnp.exp(lbp - jnp.max(lbp, axis=0, keepdims=True))
    sm = e / jnp.sum(e, axis=0, keepdims=True)
    cum = sm[0:1]
    for j in range(1, layer + 1):
        cum = cum + sm[j:j + 1]
    lb = cum - sm[0:1]

    nw = nw_ref[...]
    c = HG_CHUNK
    n_chunks = h_ref.shape[0] // c
    tri = (lax.broadcasted_iota(jnp.int32, (c, c), 0) >= lax.broadcasted_iota(jnp.int32, (c, c), 1))

    def gates(ci, hg_ref):
        rows = slice(ci * c, (ci + 1) * c)
        q = hg_ref[rows, 0:MIX_W]
        fz = hg_ref[rows, MIX_W:2 * MIX_W]
        f = lb + (1.0 - lb) * _sigmoid(fz)
        k = 1.0 - f
        g = _cumsum(jnp.log2(f), 0)
        g_ref = g[HG_REF_ROW:HG_REF_ROW + 1]
        g_last = g[c - 1:c]
        d_ref = g - g_ref
        q_rel = (q * jnp.exp2(jnp.minimum(d_ref, EXP2_CLAMP))).astype(BF16)
        k_rel = (k * jnp.exp2(jnp.minimum(-d_ref, EXP2_CLAMP))).astype(BF16)
        q_in = (q * jnp.exp2(g)).astype(BF16)
        k_end = (k * jnp.exp2(g_last - g)).astype(BF16)
        s_decay = jnp.exp2(g_last)
        return q_rel, k_rel, q_in, k_end, s_decay

    def mix(ci, hg_ref, operands):
        rows = slice(ci * c, (ci + 1) * c)
        q_rel, k_rel, q_in, k_end, s_decay = operands
        v = hg_ref[rows, 2 * MIX_W:3 * MIX_W]
        gate = hg_ref[rows, 3 * MIX_W:4 * MIX_W]
        for h in range(HG_HEADS):
            sl = slice(h * HG_KEY, (h + 1) * HG_KEY)
            st = st_ref[h]
            v_t = v[:, sl].T.astype(BF16)
            scores = jnp.where(tri, _dot_nt(q_rel[:, sl], k_rel[:, sl]), 0.0)
            lhs = jnp.concatenate([q_in[:, sl], scores.astype(BF16)], axis=1)
            rhs_t = jnp.concatenate([st.astype(BF16), v_t], axis=1)
            o = _dot_nt(lhs, rhs_t)
            st_ref[h] = st * s_decay[:, sl] + _dot(v_t, k_end[:, sl])
            o = o * lax.rsqrt(jnp.mean(o * o, axis=-1, keepdims=True) + RMS_EPS) * nw
            gh = gate[:, sl]
            o_ref[rows, sl] = (o * (gh * _sigmoid(gh))).astype(BF16)

    def tile(cur_ref, nxt_ref):
        hn = hn_ref[...].astype(BF16)
        cols = w_ref.shape[1] // n_chunks
        operands = gates(0, cur_ref)
        for ci in range(n_chunks):
            ahead = gates(ci + 1, cur_ref) if ci + 1 < n_chunks else None
            mix(ci, cur_ref, operands)
            nxt_ref[:, ci * cols:(ci + 1) * cols] = _dot(hn, w_ref[:, ci * cols:(ci + 1) * cols])
            operands = ahead

    @pl.when(i % 2 == 0)
    def _():
        tile(hg0_ref, hg1_ref)

    @pl.when(i % 2 == 1)
    def _():
        tile(hg1_ref, hg0_ref)


def _hgrn2(h, w_all, lbp, nw, layer):
    bsz, seq, _ = h.shape
    tr = HG_ROWS
    n_tiles = seq // tr
    hg = pltpu.VMEM((tr, 4 * MIX_W), F32)
    return pl.pallas_call(
        functools.partial(_hgrn2_kernel, layer),
        grid=(bsz, n_tiles),
        in_specs=[
            pl.BlockSpec((None, tr, D_MODEL), lambda b, i: (b, i, 0)),
            pl.BlockSpec((None, tr, D_MODEL), lambda b, i: (b, jnp.minimum(i + 1, n_tiles - 1), 0)),
            pl.BlockSpec((None, D_MODEL, 4 * MIX_W), lambda b, i: (layer, 0, 0)),
            pl.BlockSpec(lbp.shape, lambda b, i: (0, 0)),
            pl.BlockSpec(nw.shape, lambda b, i: (0, 0)),
        ],
        out_specs=pl.BlockSpec((None, tr, MIX_W), lambda b, i: (b, i, 0)),
        out_shape=jax.ShapeDtypeStruct((bsz, seq, MIX_W), BF16),
        scratch_shapes=[pltpu.VMEM((HG_HEADS, HG_KEY, HG_KEY), F32), hg, hg],
        compiler_params=_params(("arbitrary", "arbitrary")),
        name="hgrn2",
    )(h, h, w_all, lbp, nw)


def _fox_kernel(q_ref, ka_ref, va_ref, o_ref, qa_ref, m_ref, acc_ref, s_ref):
    qi = pl.program_id(1)
    tq, tk = FOX_Q, FOX_KV
    n_q = q_ref.shape[0] // tq
    lane = lax.broadcasted_iota(jnp.int32, (1, LANES), 1)
    causal = (lax.broadcasted_iota(jnp.int32, (tq, tk), 0) >= lax.broadcasted_iota(jnp.int32, (tq, tk), 1))

    def load_queries(t):
        for h in range(FOX_HEADS):
            sl = slice((h // 2) * LANES, (h // 2 + 1) * LANES)
            own, aux = _fox_lanes(h, lane)
            ones = jnp.where((lane >= aux) & (lane < aux + 3), 1.0, 0.0)
            q = q_ref[t * tq:(t + 1) * tq, sl].astype(F32) * (FOX_HEAD_DIM ** -0.5 * LOG2E)
            qa_ref[h] = jnp.where(own, q, ones).astype(BF16)

    def reset_state():
        m_ref[...] = jnp.full(m_ref.shape, -1e30, F32)
        acc_ref[...] = jnp.zeros(acc_ref.shape, F32)

    def logits(h, kb, buf):
        s_ref[buf, h] = _dot_nt(qa_ref[h], ka_ref[h, kb * tk:(kb + 1) * tk, :])

    def update(h, kb, buf, masked):
        s = s_ref[buf, h]
        if masked:
            s = jnp.where(causal, s, -1e30)
        m_prev = m_ref[h]
        m_new = jnp.maximum(m_prev, jnp.max(s, axis=-1, keepdims=True))
        p = jnp.concatenate([jnp.exp2(s[:, j * LANES:(j + 1) * LANES] - m_new)
                             for j in range(tk // LANES)], axis=1)
        pv = _dot(p.astype(BF16), va_ref[h, kb * tk:(kb + 1) * tk, :])
        acc_ref[h] = jnp.exp2(m_prev - m_new) * acc_ref[h] + pv
        m_ref[h] = m_new

    def write_out():
        for pair in range(FOX_HEADS // 2):
            a = acc_ref[2 * pair]
            b = acc_ref[2 * pair + 1]
            o = jnp.where(lane < FOX_HEAD_DIM, a / a[:, FOX_HEAD_DIM:FOX_HEAD_DIM + 1], b / b[:, 0:1])
            o_ref[:, pair * LANES:(pair + 1) * LANES] = o.astype(BF16)

    start = [0]
    for t in range(1, n_q):
        start.append(1 - (start[t - 1] + t - 1) % 2)
    for t in range(n_q):
        @pl.when(qi == t)
        def _(t=t):
            if t == 0:
                load_queries(0)
                reset_state()
                for h in range(FOX_HEADS):
                    logits(h, 0, start[0])
            for j in range(t):
                for h in range(FOX_HEADS):
                    logits(h, j + 1, (start[t] + j + 1) % 2)
                    update(h, j, (start[t] + j) % 2, False)
            if t + 1 < n_q:
                load_queries(t + 1)
            for h in range(FOX_HEADS):
                update(h, t, (start[t] + t) % 2, True)
                if t + 1 < n_q:
                    logits(h, 0, start[t + 1])
            write_out()
            if t + 1 < n_q:
                reset_state()


def _fox(q, ka, va):
    bsz, seq, _ = q.shape
    tq = FOX_Q
    state = pltpu.VMEM((FOX_HEADS, tq, LANES), F32)
    full_spec = pl.BlockSpec((None, FOX_HEADS, seq, LANES), lambda b, i: (b, 0, 0, 0))
    return pl.pallas_call(
        _fox_kernel,
        grid=(bsz, seq // tq),
        in_specs=[pl.BlockSpec((None, seq, MIX_W), lambda b, i: (b, 0, 0)), full_spec, full_spec],
        out_specs=pl.BlockSpec((None, tq, MIX_W), lambda b, i: (b, i, 0)),
        out_shape=jax.ShapeDtypeStruct((bsz, seq, MIX_W), BF16),
        scratch_shapes=[pltpu.VMEM((FOX_HEADS, tq, LANES), BF16), state, state,
                        pltpu.VMEM((2, FOX_HEADS, tq, FOX_KV), F32)],
        compiler_params=_params(("arbitrary", "arbitrary")),
        name="fox_attn",
    )(q, ka, va)


def _s5_prep_kernel(lre_ref, lim_ref, ls_ref, bre_ref, bim_ref, are_ref, aim_ref, bbr_ref, bbi_ref):
    lr, li = lre_ref[...], lim_ref[...]
    dt = jnp.exp(ls_ref[...])
    mag = jnp.exp(lr * dt)
    ar = mag * jnp.cos(li * dt)
    ai = mag * jnp.sin(li * dt)
    den = lr * lr + li * li
    cr = ((ar - 1.0) * lr + ai * li) / den
    ci = (ai * lr - (ar - 1.0) * li) / den
    br, bi = bre_ref[...], bim_ref[...]
    bbr_ref[...] = cr * br - ci * bi
    bbi_ref[...] = cr * bi + ci * br
    are_ref[...] = jnp.broadcast_to(ar, are_ref.shape)
    aim_ref[...] = jnp.broadcast_to(ai, aim_ref.shape)


def _s5_prep(lam_re, lam_im, log_step, b_re, b_im):
    row = lambda a: a.reshape(1, S5_NSTATE)
    ls = jnp.repeat(log_step, S5_STATE).reshape(1, S5_NSTATE)
    to_hp = lambda b: b.transpose(2, 0, 1).reshape(S5_GROUP_CH, S5_NSTATE)
    vec = jax.ShapeDtypeStruct((8, S5_NSTATE), F32)
    mat = jax.ShapeDtypeStruct((S5_GROUP_CH, S5_NSTATE), F32)
    return pl.pallas_call(
        _s5_prep_kernel, out_shape=[vec, vec, mat, mat], name="s5_prep",
    )(row(lam_re), row(lam_im), ls, to_hp(b_re), to_hp(b_im))


def _s5_kernel(u_ref, are_ref, aim_ref, wb_ref, wc_ref, d_ref, wglu_ref, o_ref, x_ref, st_ref, y_ref):
    @pl.when(pl.program_id(0) == 0)
    def _():
        st_ref[...] = jnp.zeros_like(st_ref)

    n = S5_NSTATE
    n_tiles = 2 * n // MXU_DIM
    total_rows = u_ref.shape[1]
    steps = total_rows // 8
    width = 1024

    def load_u(rows):
        return jnp.concatenate([u_ref[s, rows, :] for s in range(MIX_W // LANES)], axis=1)

    def project_in(rows):
        ub = load_u(rows).astype(BF16)
        for j in range(n_tiles):
            kh = ((j % (n_tiles // 2)) * MXU_DIM // S5_STATE * S5_GROUP_CH) // MXU_DIM
            x_ref[rows, j * MXU_DIM:(j + 1) * MXU_DIM] = _dot(ub[:, kh * MXU_DIM:(kh + 1) * MXU_DIM], wb_ref[j])

    def scan(t0, t1, state):
        for i, c0 in enumerate(range(0, n, width)):
            re_sl = slice(c0, c0 + width)
            im_sl = slice(n + c0, n + c0 + width)
            ar = are_ref[:, re_sl]
            ai = aim_ref[:, re_sl]
            xr, xi = state[i]
            for t in range(t0, t1):
                rows = slice(t * 8, t * 8 + 8)
                xr, xi = (ar * xr - ai * xi + x_ref[rows, re_sl], ar * xi + ai * xr + x_ref[rows, im_sl])
                x_ref[rows, re_sl] = xr
                x_ref[rows, im_sl] = xi
            state[i] = (xr, xi)

    def project_out(rows):
        halves = []
        per_half = n // MXU_DIM // 2
        for nh in range(MIX_W // MXU_DIM):
            acc_re = None
            acc_im = None
            for kk in range(per_half):
                kr = nh * per_half + kk
                ki = n // MXU_DIM + kr
                pr = _dot(x_ref[rows, kr * MXU_DIM:(kr + 1) * MXU_DIM].astype(BF16), wc_ref[kr])
                pi = _dot(x_ref[rows, ki * MXU_DIM:(ki + 1) * MXU_DIM].astype(BF16), wc_ref[ki])
                acc_re = pr if acc_re is None else acc_re + pr
                acc_im = pi if acc_im is None else acc_im + pi
            halves.append(acc_re - acc_im)
        y = jnp.concatenate(halves, axis=1) + d_ref[...] * load_u(rows)
        y = 0.5 * y * (1.0 + jnp.tanh(math.sqrt(2.0 / math.pi) * (y + 0.044715 * (y * y * y))))
        y = y * _sigmoid(_dot(y.astype(BF16), wglu_ref[...]))
        for s in range(MIX_W // LANES):
            y_ref[s, rows, :] = y[:, s * LANES:(s + 1) * LANES]

    phase_rows = total_rows // S5_PHASES
    phases = [slice(p * phase_rows, (p + 1) * phase_rows) for p in range(S5_PHASES)]
    for rows in phases:
        project_in(rows)
    state = [(st_ref[:, c0:c0 + width], st_ref[:, n + c0:n + c0 + width]) for c0 in range(0, n, width)]
    for p, rows in enumerate(phases):
        scan(p * steps // S5_PHASES, (p + 1) * steps // S5_PHASES, state)
        project_out(rows)
    for i, c0 in enumerate(range(0, n, width)):
        st_ref[:, c0:c0 + width] = state[i][0]
        st_ref[:, n + c0:n + c0 + width] = state[i][1]
    bsz = o_ref.shape[0]
    for b in range(bsz):
        o_ref[b] = jnp.concatenate([y_ref[s, pl.ds(b, steps, stride=bsz), :]
                                    for s in range(MIX_W // LANES)], axis=1).astype(BF16)


def _s5(u_tm, a_re, a_im, wb, wc, d, wglu, bsz, layer):
    rows = S5_STEPS * bsz
    total = u_tm.shape[1]
    const2 = lambda i: (0, 0)
    const3 = lambda i: (0, 0, 0)
    return pl.pallas_call(
        _s5_kernel,
        grid=(total // rows,),
        in_specs=[
            pl.BlockSpec((MIX_W // LANES, rows, LANES), lambda i: (0, i, 0)),
            pl.BlockSpec(a_re.shape, const2),
            pl.BlockSpec(a_im.shape, const2),
            pl.BlockSpec(wb.shape, const3),
            pl.BlockSpec(wc.shape, const3),
            pl.BlockSpec(d.shape, const2),
            _layer_spec(wglu, layer),
        ],
        out_specs=pl.BlockSpec((bsz, S5_STEPS, MIX_W), lambda i: (0, i, 0)),
        out_shape=jax.ShapeDtypeStruct((bsz, total // bsz, MIX_W), BF16),
        scratch_shapes=[pltpu.VMEM((rows, 2 * S5_NSTATE), F32), pltpu.VMEM((8, 2 * S5_NSTATE), F32),
                        pltpu.VMEM((MIX_W // LANES, rows, LANES), F32)],
        compiler_params=_params(("arbitrary",)),
        name="s5",
    )(u_tm, a_re, a_im, wb, wc, d, wglu)


def _s5_weights(bbar_re, bbar_im, c_re, c_im):
    half = S5_NSTATE // MXU_DIM
    g_per_tile = MXU_DIM // S5_STATE
    g_per_slab = MXU_DIM // S5_GROUP_CH
    t = jnp.arange(half)[:, None, None]
    r = jnp.arange(MXU_DIM)[None, :, None]
    c = jnp.arange(MXU_DIM)[None, None, :]
    slab = (t * g_per_tile) // g_per_slab
    b_keep = slab * g_per_slab + r // S5_GROUP_CH == t * g_per_tile + c // S5_STATE
    c_keep = t * g_per_tile + r // S5_STATE == slab * g_per_slab + c // S5_GROUP_CH

    def b_tiles(bb):
        bb = bb.reshape(S5_GROUP_CH, half, MXU_DIM).transpose(1, 0, 2)
        return jnp.where(b_keep, jnp.tile(bb, (1, g_per_slab, 1)), 0.0)

    def c_tiles(cc):
        cc = cc.transpose(0, 2, 1).reshape(half, MXU_DIM, S5_GROUP_CH)
        return jnp.where(c_keep, jnp.tile(cc, (1, 1, g_per_slab)), 0.0)

    wb = jnp.concatenate([b_tiles(bbar_re), b_tiles(bbar_im)], axis=0).astype(BF16)
    wc = jnp.concatenate([c_tiles(c_re), c_tiles(c_im)], axis=0).astype(BF16)
    return wb, wc


def _merge_kernel(alpha, h_ref, ya_ref, yb_ref, yc_ref, wg_ref, wa_ref, wb_ref, wc_ref, wo_ref,
                  g_ref, b_ref, o_ref):
    for r0 in range(0, h_ref.shape[0], ROW_SUB):
        rows = slice(r0, r0 + ROW_SUB)
        h = h_ref[rows, :]
        hb = h.astype(BF16)
        merged = None
        for i, (y_ref, w_ref) in enumerate(((ya_ref, wa_ref), (yb_ref, wb_ref), (yc_ref, wc_ref))):
            gate = _sigmoid(_dot(hb, wg_ref[:, i * D_MODEL:(i + 1) * D_MODEL]))
            term = gate * _dot(y_ref[rows, :], w_ref[...])
            merged = term if merged is None else merged + term
        mix = _dot(merged.astype(BF16), wo_ref[...])
        o_ref[rows, :] = _layer_norm(alpha * h + mix, g_ref[...], b_ref[...])


def _merge(h, ya, yb, yc, wg, wa, wb, wc, wo, g, b, alpha, layer):
    bsz, seq, _ = h.shape
    tm = MERGE_ROWS
    const = lambda bb, i: (0, 0)
    row_spec = lambda w: pl.BlockSpec((None, tm, w), lambda bb, i: (bb, i, 0))
    return pl.pallas_call(
        functools.partial(_merge_kernel, alpha),
        grid=(bsz, seq // tm),
        in_specs=[
            row_spec(D_MODEL), row_spec(MIX_W), row_spec(MIX_W), row_spec(MIX_W),
            pl.BlockSpec(wg.shape, const, pipeline_mode=pl.Buffered(1)), _layer_spec(wa, layer),
            _layer_spec(wb, layer),
            _layer_spec(wc, layer), _layer_spec(wo, layer),
            pl.BlockSpec(g.shape, const), pl.BlockSpec(b.shape, const),
        ],
        out_specs=row_spec(D_MODEL),
        out_shape=jax.ShapeDtypeStruct(h.shape, F32),
        compiler_params=_params(("arbitrary", "arbitrary")),
        name="merge_ln",
    )(h, ya, yb, yc, wg, wa, wb, wc, wo, g, b)


def _ffn_kernel(alpha, h_ref, wg_ref, wu_ref, wd_ref, g_ref, b_ref, o_ref):
    for r0 in range(0, h_ref.shape[0], ROW_SUB):
        rows = slice(r0, r0 + ROW_SUB)
        h = h_ref[rows, :]
        hb = h.astype(BF16)
        acc = None
        for c0 in range(0, FFN_HIDDEN, FFN_CHUNK):
            sl = slice(c0, min(c0 + FFN_CHUNK, FFN_HIDDEN))
            a = _dot(hb, wg_ref[:, sl])
            hid = (a * _sigmoid(a)) * _dot(hb, wu_ref[:, sl])
            part = _dot(hid.astype(BF16), wd_ref[sl, :])
            acc = part if acc is None else acc + part
        o_ref[rows, :] = _layer_norm(alpha * h + acc, g_ref[...], b_ref[...])


def _ffn(h, wg, wu, wd, g, b, alpha, layer):
    bsz, seq, _ = h.shape
    tm = FFN_ROWS
    const = lambda bb, i: (0, 0)
    row_spec = pl.BlockSpec((None, tm, D_MODEL), lambda bb, i: (bb, i, 0))
    return pl.pallas_call(
        functools.partial(_ffn_kernel, alpha),
        grid=(bsz, seq // tm),
        in_specs=[row_spec, _layer_spec(wg, layer), _layer_spec(wu, layer), _layer_spec(wd, layer),
                  pl.BlockSpec(g.shape, const), pl.BlockSpec(b.shape, const)],
        out_specs=row_spec,
        out_shape=jax.ShapeDtypeStruct(h.shape, F32),
        compiler_params=_params(("arbitrary", "arbitrary")),
        name="ffn_ln",
    )(h, wg, wu, wd, g, b)


def kernel(x, w_in, hg_lower_bounds, hg_norm_w, fox_b_f, s5_lambda_re, s5_lambda_im, s5_log_step,
           s5_b_re, s5_b_im, s5_c_re, s5_c_im, s5_d, s5_w_glu, w_br_a, w_br_b, w_br_c, w_out,
           ln1_g, ln1_b, w_ffn_gate, w_ffn_up, w_ffn_down, ln2_g, ln2_b):
    depth = w_in.shape[0]
    bsz, seq, _ = x.shape
    alpha = (2 * depth) ** 0.25
    o_s5 = FOX_COL0 + 3 * MIX_W + FOX_HEADS
    o_gate = o_s5 + MIX_W
    row = lambda a: a.reshape(1, -1)

    bf = lambda a: a.astype(BF16)
    w_in_b, w_glu_b = bf(w_in), bf(s5_w_glu)
    w_a, w_b, w_c, w_o = bf(w_br_a), bf(w_br_b), bf(w_br_c), bf(w_out)
    w_fg, w_fu, w_fd = bf(w_ffn_gate), bf(w_ffn_up), bf(w_ffn_down)
    fox_bias = jnp.pad(fox_b_f, ((0, 0), (0, LANES - FOX_HEADS))).reshape(depth, 1, LANES)

    h = x
    for l in range(depth):
        ws5 = w_in_b[l, :, o_s5:o_gate]
        wgate = w_in_b[l, :, o_gate:]

        q, ka, va, u_tm = _in_proj(h, w_in_b, fox_bias, ws5, l)

        ya = _hgrn2(h, w_in_b, hg_lower_bounds, row(hg_norm_w[l]), l)
        yb = _fox(q, ka, va)

        a_re, a_im, bbar_re, bbar_im = _s5_prep(s5_lambda_re[l], s5_lambda_im[l], s5_log_step[l],
                                                s5_b_re[l], s5_b_im[l])
        wb, wc = _s5_weights(bbar_re, bbar_im, s5_c_re[l], s5_c_im[l])
        yc = _s5(u_tm, a_re, a_im, wb, wc, row(s5_d[l]), w_glu_b, bsz, l)

        h = _merge(h, ya, yb, yc, wgate, w_a, w_b, w_c, w_o, row(ln1_g[l]), row(ln1_b[l]), alpha, l)
        h = _ffn(h, w_fg, w_fu, w_fd, row(ln2_g[l]), row(ln2_b[l]), alpha, l)
    return h
```

```python
import functools
import math

import jax
import jax.numpy as jnp
from jax import lax
from jax.experimental import pallas as pl
from jax.experimental.pallas import tpu as pltpu

F32 = jnp.float32
BF16 = jnp.bfloat16

D_MODEL = 1024
MIX_W = 512
HG_HEADS = 4
HG_KEY = 128
HG_CHUNK = 64
HG_REF_ROW = HG_CHUNK // 2 - 1
FOX_HEADS = 8
FOX_COL0 = 4 * MIX_W
FOX_HEAD_DIM = 64
S5_GROUPS = 32
S5_STATE = 64
S5_GROUP_CH = 16
S5_NSTATE = S5_GROUPS * S5_STATE
FFN_HIDDEN = 2816
LN_EPS = 1e-5
RMS_EPS = 1e-6
EXP2_CLAMP = 115.0
LOG2E = math.log2(math.e)

LANES = 128
MXU_DIM = 256
VMEM_LIMIT = 56 * 1024 * 1024

PROJ_ROWS = 1024
PROJ_GROUP = 4
HG_ROWS = 256
FOX_Q = 512
FOX_KV = 512
S5_PHASES = 4
S5_STEPS = 128
MERGE_ROWS = 1024
FFN_ROWS = 1024
ROW_SUB = 512
FFN_CHUNK = 6 * MXU_DIM


def _dot(a, b):
    return jnp.dot(a, b, preferred_element_type=F32)


def _dot_nt(a, b):
    return lax.dot_general(a, b, (((1,), (1,)), ((), ())), preferred_element_type=F32)


def _dot_tn(a, b):
    return lax.dot_general(a, b, (((0,), (0,)), ((), ())), preferred_element_type=F32)


def _sigmoid(x):
    return 1.0 / (1.0 + jnp.exp2(x * -LOG2E))


def _layer_norm(z, g, b):
    mu = jnp.mean(z, axis=-1, keepdims=True)
    zc = z - mu
    var = jnp.mean(zc * zc, axis=-1, keepdims=True)
    return zc * lax.rsqrt(var + LN_EPS) * g + b


def _cumsum(x, axis):
    n = x.shape[axis]
    idx = lax.broadcasted_iota(jnp.int32, x.shape, axis)
    d = 1
    while d < n:
        x = x + jnp.where(idx >= d, pltpu.roll(x, d, axis=axis), 0.0)
        d *= 2
    return x


def _layer_spec(w, layer):
    return pl.BlockSpec((None,) + w.shape[1:], lambda *_: (layer, 0, 0), pipeline_mode=pl.Buffered(1))


def _params(sem):
    return pltpu.CompilerParams(dimension_semantics=sem, vmem_limit_bytes=VMEM_LIMIT)


def _fox_lanes(h, lane):
    if h % 2 == 0:
        return lane < FOX_HEAD_DIM, FOX_HEAD_DIM
    return lane >= FOX_HEAD_DIM, 0


def _proj_kernel(x_ref, wq_ref, wk_ref, wv_ref, wff_ref, bf_ref, ws5_ref, q_ref, ka_ref, va_ref, s5_ref,
                 cum_ref):
    bsz, tt, _ = x_ref.shape

    @pl.when(pl.program_id(0) == 0)
    def _():
        cum_ref[...] = jnp.zeros_like(cum_ref)

    lane = lax.broadcasted_iota(jnp.int32, (1, LANES), 1)
    for b0 in range(0, bsz, PROJ_GROUP):
        nb = PROJ_GROUP
        rows = nb * tt
        xb = x_ref[b0:b0 + nb].reshape(rows, D_MODEL).astype(BF16)
        k = _dot(xb, wk_ref[...])
        v = _dot(xb, wv_ref[...])
        z = _dot(xb, wff_ref[...]) + bf_ref[...]
        q_ref[b0:b0 + nb] = _dot(xb, wq_ref[...]).astype(BF16).reshape(nb, tt, MIX_W)
        u = _dot(xb, ws5_ref[...])

        logsig = jnp.minimum(z, 0.0) - jnp.log(1.0 + jnp.exp(-jnp.abs(z)))
        t_idx = lax.broadcasted_iota(jnp.int32, (rows, LANES), 0) % tt
        cum = logsig
        d = 1
        while d < tt:
            cum = cum + jnp.where(t_idx >= d, pltpu.roll(cum, d, axis=0), 0.0)
            d *= 2
        carry = cum_ref[b0:b0 + nb, :]
        cum = jnp.concatenate([cum[b * tt:(b + 1) * tt] + carry[b:b + 1] for b in range(nb)], axis=0)
        cum_ref[b0:b0 + nb, :] = jnp.concatenate([cum[(b + 1) * tt - 1:(b + 1) * tt] for b in range(nb)], axis=0)

        bias = cum * (-LOG2E)
        for h in range(FOX_HEADS):
            sl = slice((h // 2) * LANES, (h // 2 + 1) * LANES)
            own, aux = _fox_lanes(h, lane)
            bh = jnp.broadcast_to(bias[:, h:h + 1], (rows, LANES))
            hi = bh.astype(BF16).astype(F32)
            mid = (bh - hi).astype(BF16).astype(F32)
            lo = bh - hi - mid
            extra = jnp.where(lane == aux, hi, jnp.where(lane == aux + 1, mid,
                                                         jnp.where(lane == aux + 2, lo, 0.0)))
            ka = jnp.where(own, k[:, sl], extra).astype(BF16)
            va = jnp.where(own, v[:, sl], jnp.where(lane == aux, 1.0, 0.0)).astype(BF16)
            for b in range(nb):
                ka_ref[b0 + b, h] = ka[b * tt:(b + 1) * tt]
                va_ref[b0 + b, h] = va[b * tt:(b + 1) * tt]

        for b in range(nb):
            for s in range(MIX_W // LANES):
                s5_ref[s, pl.ds(b0 + b, tt, stride=bsz), :] = u[b * tt:(b + 1) * tt, s * LANES:(s + 1) * LANES]


def _in_proj(h, w_all, bf, ws5, layer):
    bsz, seq, _ = h.shape
    tt = PROJ_ROWS // bsz
    fox_spec = lambda j: pl.BlockSpec((None, D_MODEL, MIX_W), lambda i: (layer, 0, FOX_COL0 // MIX_W + j))
    head_spec = pl.BlockSpec((bsz, FOX_HEADS, tt, LANES), lambda i: (0, 0, i, 0))
    head_shape = jax.ShapeDtypeStruct((bsz, FOX_HEADS, seq, LANES), BF16)
    return pl.pallas_call(
        _proj_kernel,
        grid=(seq // tt,),
        in_specs=[
            pl.BlockSpec((bsz, tt, D_MODEL), lambda i: (0, i, 0)),
            fox_spec(0), fox_spec(1), fox_spec(2),
            pl.BlockSpec((None, D_MODEL, LANES), lambda i: (layer, 0, (FOX_COL0 + 3 * MIX_W) // LANES)),
            pl.BlockSpec((None, 1, LANES), lambda i: (layer, 0, 0)),
            pl.BlockSpec(ws5.shape, lambda i: (0, 0)),
        ],
        out_specs=[
            pl.BlockSpec((bsz, tt, MIX_W), lambda i: (0, i, 0)),
            head_spec, head_spec,
            pl.BlockSpec((MIX_W // LANES, bsz * tt, LANES), lambda i: (0, i, 0)),
        ],
        out_shape=[
            jax.ShapeDtypeStruct((bsz, seq, MIX_W), BF16),
            head_shape, head_shape,
            jax.ShapeDtypeStruct((MIX_W // LANES, seq * bsz, LANES), F32),
        ],
        scratch_shapes=[pltpu.VMEM((bsz, LANES), F32)],
        compiler_params=_params(("arbitrary",)),
        name="in_proj",
    )(h, w_all, w_all, w_all, w_all, bf, ws5)


def _hgrn2_kernel(layer, h_ref, hn_ref, w_ref, lbp_ref, nw_ref, o_ref, st_ref, hg0_ref, hg1_ref):
    i = pl.program_id(1)

    @pl.when(i == 0)
    def _():
        st_ref[...] = jnp.zeros_like(st_ref)
        hg0_ref[...] = _dot(h_ref[...].astype(BF16), w_ref[...])

    lbp = lbp_ref[...]
    e = jnp.exp(lbp - jnp.max(lbp, axis=0, keepdims=True))
    sm = e / jnp.sum(e, axis=0, keepdims=True)
    cum = sm[0:1]
    for j in range(1, layer + 1):
        cum = cum + sm[j:j + 1]
    lb = cum - sm[0:1]

    nw = nw_ref[...]
    c = HG_CHUNK
    n_chunks = h_ref.shape[0] // c
    tri = (lax.broadcasted_iota(jnp.int32, (c, c), 0) >= lax.broadcasted_iota(jnp.int32, (c, c), 1))

    def gates(ci, hg_ref):
        rows = slice(ci * c, (ci + 1) * c)
        q = hg_ref[rows, 0:MIX_W]
        fz = hg_ref[rows, MIX_W:2 * MIX_W]
        f = lb + (1.0 - lb) * _sigmoid(fz)
        k = 1.0 - f
        g = _cumsum(jnp.log2(f), 0)
        g_ref = g[HG_REF_ROW:HG_REF_ROW + 1]
        g_last = g[c - 1:c]
        d_ref = g - g_ref
        q_rel = (q * jnp.exp2(jnp.minimum(d_ref, EXP2_CLAMP))).astype(BF16)
        k_rel = (k * jnp.exp2(jnp.minimum(-d_ref, EXP2_CLAMP))).astype(BF16)
        q_in = (q * jnp.exp2(g)).astype(BF16)
        k_end = (k * jnp.exp2(g_last - g)).astype(BF16)
        s_decay = jnp.exp2(g_last)
        return q_rel, k_rel, q_in, k_end, s_decay

    def mix(ci, hg_ref, operands):
        rows = slice(ci * c, (ci + 1) * c)
        q_rel, k_rel, q_in, k_end, s_decay = operands
        v = hg_ref[rows, 2 * MIX_W:3 * MIX_W]
        gate = hg_ref[rows, 3 * MIX_W:4 * MIX_W]
        for h in range(HG_HEADS):
            sl = slice(h * HG_KEY, (h + 1) * HG_KEY)
            st = st_ref[h]
            v_t = v[:, sl].T.astype(BF16)
            scores = jnp.where(tri, _dot_nt(q_rel[:, sl], k_rel[:, sl]), 0.0)
            lhs = jnp.concatenate([q_in[:, sl], scores.astype(BF16)], axis=1)
            rhs_t = jnp.concatenate([st.astype(BF16), v_t], axis=1)
            o = _dot_nt(lhs, rhs_t)
            st_ref[h] = st * s_decay[:, sl] + _dot(v_t, k_end[:, sl])
            o = o * lax.rsqrt(jnp.mean(o * o, axis=-1, keepdims=True) + RMS_EPS) * nw
            gh = gate[:, sl]
            o_ref[rows, sl] = (o * (gh * _sigmoid(gh))).astype(BF16)

    def tile(cur_ref, nxt_ref):
        hn = hn_ref[...].astype(BF16)
        cols = w_ref.shape[1] // n_chunks
        operands = gates(0, cur_ref)
        for ci in range(n_chunks):
            ahead = gates(ci + 1, cur_ref) if ci + 1 < n_chunks else None
            mix(ci, cur_ref, operands)
            nxt_ref[:, ci * cols:(ci + 1) * cols] = _dot(hn, w_ref[:, ci * cols:(ci + 1) * cols])
            operands = ahead

    @pl.when(i % 2 == 0)
    def _():
        tile(hg0_ref, hg1_ref)

    @pl.when(i % 2 == 1)
    def _():
        tile(hg1_ref, hg0_ref)


def _hgrn2(h, w_all, lbp, nw, layer):
    bsz, seq, _ = h.shape
    tr = HG_ROWS
    n_tiles = seq // tr
    hg = pltpu.VMEM((tr, 4 * MIX_W), F32)
    return pl.pallas_call(
        functools.partial(_hgrn2_kernel, layer),
        grid=(bsz, n_tiles),
        in_specs=[
            pl.BlockSpec((None, tr, D_MODEL), lambda b, i: (b, i, 0)),
            pl.BlockSpec((None, tr, D_MODEL), lambda b, i: (b, jnp.minimum(i + 1, n_tiles - 1), 0)),
            pl.BlockSpec((None, D_MODEL, 4 * MIX_W), lambda b, i: (layer, 0, 0)),
            pl.BlockSpec(lbp.shape, lambda b, i: (0, 0)),
            pl.BlockSpec(nw.shape, lambda b, i: (0, 0)),
        ],
        out_specs=pl.BlockSpec((None, tr, MIX_W), lambda b, i: (b, i, 0)),
        out_shape=jax.ShapeDtypeStruct((bsz, seq, MIX_W), BF16),
        scratch_shapes=[pltpu.VMEM((HG_HEADS, HG_KEY, HG_KEY), F32), hg, hg],
        compiler_params=_params(("arbitrary", "arbitrary")),
        name="hgrn2",
    )(h, h, w_all, lbp, nw)


def _fox_kernel(q_ref, ka_ref, va_ref, o_ref, qa_ref, m_ref, acc_ref, s_ref):
    qi = pl.program_id(1)
    tq, tk = FOX_Q, FOX_KV
    n_q = q_ref.shape[0] // tq
    lane = lax.broadcasted_iota(jnp.int32, (1, LANES), 1)
    causal = (lax.broadcasted_iota(jnp.int32, (tq, tk), 0) >= lax.broadcasted_iota(jnp.int32, (tq, tk), 1))

    def load_queries(t):
        for h in range(FOX_HEADS):
            sl = slice((h // 2) * LANES, (h // 2 + 1) * LANES)
            own, aux = _fox_lanes(h, lane)
            ones = jnp.where((lane >= aux) & (lane < aux + 3), 1.0, 0.0)
            q = q_ref[t * tq:(t + 1) * tq, sl].astype(F32) * (FOX_HEAD_DIM ** -0.5 * LOG2E)
            qa_ref[h] = jnp.where(own, q, ones).astype(BF16)

    def reset_state():
        m_ref[...] = jnp.full(m_ref.shape, -1e30, F32)
        acc_ref[...] = jnp.zeros(acc_ref.shape, F32)

    hq = tq // 2
    spans = ((slice(0, hq), tk // 2), (slice(hq, tq), tk))

    def logits(h, kb, buf, diag=False):
        if not diag:
            s_ref[buf, h] = _dot_nt(qa_ref[h], ka_ref[h, kb * tk:(kb + 1) * tk, :])
            return
        for rows, nk in spans:
            s_ref[buf, h, rows, 0:nk] = _dot_nt(qa_ref[h, rows, :], ka_ref[h, kb * tk:kb * tk + nk, :])

    def softmax_pv(h, kb, s, rows):
        nk = s.shape[1]
        m_prev = m_ref[h, rows, :]
        m_new = jnp.maximum(m_prev, jnp.max(s, axis=-1, keepdims=True))
        p = jnp.concatenate([jnp.exp2(s[:, j * LANES:(j + 1) * LANES] - m_new)
                             for j in range(nk // LANES)], axis=1)
        pv = _dot(p.astype(BF16), va_ref[h, kb * tk:kb * tk + nk, :])
        acc_ref[h, rows, :] = jnp.exp2(m_prev - m_new) * acc_ref[h, rows, :] + pv
        m_ref[h, rows, :] = m_new

    def update(h, kb, buf, masked):
        if not masked:
            softmax_pv(h, kb, s_ref[buf, h], slice(0, tq))
            return
        for rows, nk in spans:
            s = s_ref[buf, h, rows, 0:nk]
            softmax_pv(h, kb, jnp.where(causal[rows, 0:nk], s, -1e30), rows)

    def write_out():
        for pair in range(FOX_HEADS // 2):
            a = acc_ref[2 * pair]
            b = acc_ref[2 * pair + 1]
            o = jnp.where(lane < FOX_HEAD_DIM, a / a[:, FOX_HEAD_DIM:FOX_HEAD_DIM + 1], b / b[:, 0:1])
            o_ref[:, pair * LANES:(pair + 1) * LANES] = o.astype(BF16)

    start = [0]
    for t in range(1, n_q):
        start.append(1 - (start[t - 1] + t - 1) % 2)
    for t in range(n_q):
        @pl.when(qi == t)
        def _(t=t):
            if t == 0:
                load_queries(0)
                reset_state()
                for h in range(FOX_HEADS):
                    logits(h, 0, start[0], diag=True)
            for j in range(t):
                for h in range(FOX_HEADS):
                    logits(h, j + 1, (start[t] + j + 1) % 2, diag=(j + 1 == t))
                    update(h, j, (start[t] + j) % 2, False)
            if t + 1 < n_q:
                load_queries(t + 1)
            for h in range(FOX_HEADS):
                update(h, t, (start[t] + t) % 2, True)
                if t + 1 < n_q:
                    logits(h, 0, start[t + 1])
            write_out()
            if t + 1 < n_q:
                reset_state()


def _fox(q, ka, va):
    bsz, seq, _ = q.shape
    tq = FOX_Q
    state = pltpu.VMEM((FOX_HEADS, tq, LANES), F32)
    full_spec = pl.BlockSpec((None, FOX_HEADS, seq, LANES), lambda b, i: (b, 0, 0, 0))
    return pl.pallas_call(
        _fox_kernel,
        grid=(bsz, seq // tq),
        in_specs=[pl.BlockSpec((None, seq, MIX_W), lambda b, i: (b, 0, 0)), full_spec, full_spec],
        out_specs=pl.BlockSpec((None, tq, MIX_W), lambda b, i: (b, i, 0)),
        out_shape=jax.ShapeDtypeStruct((bsz, seq, MIX_W), BF16),
        scratch_shapes=[pltpu.VMEM((FOX_HEADS, tq, LANES), BF16), state, state,
                        pltpu.VMEM((2, FOX_HEADS, tq, FOX_KV), F32)],
        compiler_params=_params(("arbitrary", "arbitrary")),
        name="fox_attn",
    )(q, ka, va)


def _s5_prep_kernel(lre_ref, lim_ref, ls_ref, bre_ref, bim_ref, are_ref, aim_ref, bbr_ref, bbi_ref):
    lr, li = lre_ref[...], lim_ref[...]
    dt = jnp.exp(ls_ref[...])
    mag = jnp.exp(lr * dt)
    ar = mag * jnp.cos(li * dt)
    ai = mag * jnp.sin(li * dt)
    den = lr * lr + li * li
    cr = ((ar - 1.0) * lr + ai * li) / den
    ci = (ai * lr - (ar - 1.0) * li) / den
    br, bi = bre_ref[...], bim_ref[...]
    bbr_ref[...] = cr * br - ci * bi
    bbi_ref[...] = cr * bi + ci * br
    are_ref[...] = jnp.broadcast_to(ar, are_ref.shape)
    aim_ref[...] = jnp.broadcast_to(ai, aim_ref.shape)


def _s5_prep(lam_re, lam_im, log_step, b_re, b_im):
    row = lambda a: a.reshape(1, S5_NSTATE)
    ls = jnp.repeat(log_step, S5_STATE).reshape(1, S5_NSTATE)
    to_hp = lambda b: b.transpose(2, 0, 1).reshape(S5_GROUP_CH, S5_NSTATE)
    vec = jax.ShapeDtypeStruct((8, S5_NSTATE), F32)
    mat = jax.ShapeDtypeStruct((S5_GROUP_CH, S5_NSTATE), F32)
    return pl.pallas_call(
        _s5_prep_kernel, out_shape=[vec, vec, mat, mat], name="s5_prep",
    )(row(lam_re), row(lam_im), ls, to_hp(b_re), to_hp(b_im))


def _s5_kernel(u_ref, are_ref, aim_ref, wb_ref, wc_ref, d_ref, wglu_ref, o_ref, x_ref, st_ref, y_ref):
    @pl.when(pl.program_id(0) == 0)
    def _():
        st_ref[...] = jnp.zeros_like(st_ref)

    n = S5_NSTATE
    n_tiles = 2 * n // MXU_DIM
    total_rows = u_ref.shape[1]
    steps = total_rows // 8
    width = 1024

    def load_u(rows):
        return jnp.concatenate([u_ref[s, rows, :] for s in range(MIX_W // LANES)], axis=1)

    def project_in(rows):
        ub = load_u(rows).astype(BF16)
        for j in range(n_tiles):
            kh = ((j % (n_tiles // 2)) * MXU_DIM // S5_STATE * S5_GROUP_CH) // MXU_DIM
            x_ref[rows, j * MXU_DIM:(j + 1) * MXU_DIM] = _dot(ub[:, kh * MXU_DIM:(kh + 1) * MXU_DIM], wb_ref[j])

    def scan(t0, t1, state):
        for i, c0 in enumerate(range(0, n, width)):
            re_sl = slice(c0, c0 + width)
            im_sl = slice(n + c0, n + c0 + width)
            ar = are_ref[:, re_sl]
            ai = aim_ref[:, re_sl]
            xr, xi = state[i]
            for t in range(t0, t1):
                rows = slice(t * 8, t * 8 + 8)
                xr, xi = (ar * xr - ai * xi + x_ref[rows, re_sl], ar * xi + ai * xr + x_ref[rows, im_sl])
                x_ref[rows, re_sl] = xr
                x_ref[rows, im_sl] = xi
            state[i] = (xr, xi)

    def project_out(rows):
        halves = []
        per_half = n // MXU_DIM // 2
        for nh in range(MIX_W // MXU_DIM):
            acc_re = None
            acc_im = None
            for kk in range(per_half):
                kr = nh * per_half + kk
                ki = n // MXU_DIM + kr
                pr = _dot(x_ref[rows, kr * MXU_DIM:(kr + 1) * MXU_DIM].astype(BF16), wc_ref[kr])
                pi = _dot(x_ref[rows, ki * MXU_DIM:(ki + 1) * MXU_DIM].astype(BF16), wc_ref[ki])
                acc_re = pr if acc_re is None else acc_re + pr
                acc_im = pi if acc_im is None else acc_im + pi
            halves.append(acc_re - acc_im)
        y = jnp.concatenate(halves, axis=1) + d_ref[...] * load_u(rows)
        y = 0.5 * y * (1.0 + jnp.tanh(math.sqrt(2.0 / math.pi) * (y + 0.044715 * (y * y * y))))
        y = y * _sigmoid(_dot(y.astype(BF16), wglu_ref[...]))
        for s in range(MIX_W // LANES):
            y_ref[s, rows, :] = y[:, s * LANES:(s + 1) * LANES]

    phase_rows = total_rows // S5_PHASES
    phases = [slice(p * phase_rows, (p + 1) * phase_rows) for p in range(S5_PHASES)]
    for rows in phases:
        project_in(rows)
    state = [(st_ref[:, c0:c0 + width], st_ref[:, n + c0:n + c0 + width]) for c0 in range(0, n, width)]
    for p, rows in enumerate(phases):
        scan(p * steps // S5_PHASES, (p + 1) * steps // S5_PHASES, state)
        project_out(rows)
    for i, c0 in enumerate(range(0, n, width)):
        st_ref[:, c0:c0 + width] = state[i][0]
        st_ref[:, n + c0:n + c0 + width] = state[i][1]
    bsz = o_ref.shape[0]
    for b in range(bsz):
        o_ref[b] = jnp.concatenate([y_ref[s, pl.ds(b, steps, stride=bsz), :]
                                    for s in range(MIX_W // LANES)], axis=1).astype(BF16)


def _s5(u_tm, a_re, a_im, wb, wc, d, wglu, bsz, layer):
    rows = S5_STEPS * bsz
    total = u_tm.shape[1]
    const2 = lambda i: (0, 0)
    const3 = lambda i: (0, 0, 0)
    return pl.pallas_call(
        _s5_kernel,
        grid=(total // rows,),
        in_specs=[
            pl.BlockSpec((MIX_W // LANES, rows, LANES), lambda i: (0, i, 0)),
            pl.BlockSpec(a_re.shape, const2),
            pl.BlockSpec(a_im.shape, const2),
            pl.BlockSpec(wb.shape, const3),
            pl.BlockSpec(wc.shape, const3),
            pl.BlockSpec(d.shape, const2),
            _layer_spec(wglu, layer),
        ],
        out_specs=pl.BlockSpec((bsz, S5_STEPS, MIX_W), lambda i: (0, i, 0)),
        out_shape=jax.ShapeDtypeStruct((bsz, total // bsz, MIX_W), BF16),
        scratch_shapes=[pltpu.VMEM((rows, 2 * S5_NSTATE), F32), pltpu.VMEM((8, 2 * S5_NSTATE), F32),
                        pltpu.VMEM((MIX_W // LANES, rows, LANES), F32)],
        compiler_params=_params(("arbitrary",)),
        name="s5",
    )(u_tm, a_re, a_im, wb, wc, d, wglu)


def _s5_weights(bbar_re, bbar_im, c_re, c_im):
    half = S5_NSTATE // MXU_DIM
    g_per_tile = MXU_DIM // S5_STATE
    g_per_slab = MXU_DIM // S5_GROUP_CH
    t = jnp.arange(half)[:, None, None]
    r = jnp.arange(MXU_DIM)[None, :, None]
    c = jnp.arange(MXU_DIM)[None, None, :]
    slab = (t * g_per_tile) // g_per_slab
    b_keep = slab * g_per_slab + r // S5_GROUP_CH == t * g_per_tile + c // S5_STATE
    c_keep = t * g_per_tile + r // S5_STATE == slab * g_per_slab + c // S5_GROUP_CH

    def b_tiles(bb):
        bb = bb.reshape(S5_GROUP_CH, half, MXU_DIM).transpose(1, 0, 2)
        return jnp.where(b_keep, jnp.tile(bb, (1, g_per_slab, 1)), 0.0)

    def c_tiles(cc):
        cc = cc.transpose(0, 2, 1).reshape(half, MXU_DIM, S5_GROUP_CH)
        return jnp.where(c_keep, jnp.tile(cc, (1, 1, g_per_slab)), 0.0)

    wb = jnp.concatenate([b_tiles(bbar_re), b_tiles(bbar_im)], axis=0).astype(BF16)
    wc = jnp.concatenate([c_tiles(c_re), c_tiles(c_im)], axis=0).astype(BF16)
    return wb, wc


def _merge_kernel(alpha, h_ref, ya_ref, yb_ref, yc_ref, wg_ref, wa_ref, wb_ref, wc_ref, wo_ref,
                  g_ref, b_ref, o_ref):
    for r0 in range(0, h_ref.shape[0], ROW_SUB):
        rows = slice(r0, r0 + ROW_SUB)
        h = h_ref[rows, :]
        hb = h.astype(BF16)
        merged = None
        for i, (y_ref, w_ref) in enumerate(((ya_ref, wa_ref), (yb_ref, wb_ref), (yc_ref, wc_ref))):
            gate = _sigmoid(_dot(hb, wg_ref[:, i * D_MODEL:(i + 1) * D_MODEL]))
            term = gate * _dot(y_ref[rows, :], w_ref[...])
            merged = term if merged is None else merged + term
        mix = _dot(merged.astype(BF16), wo_ref[...])
        o_ref[rows, :] = _layer_norm(alpha * h + mix, g_ref[...], b_ref[...])


def _merge(h, ya, yb, yc, wg, wa, wb, wc, wo, g, b, alpha, layer):
    bsz, seq, _ = h.shape
    tm = MERGE_ROWS
    const = lambda bb, i: (0, 0)
    row_spec = lambda w: pl.BlockSpec((None, tm, w), lambda bb, i: (bb, i, 0))
    return pl.pallas_call(
        functools.partial(_merge_kernel, alpha),
        grid=(bsz, seq // tm),
        in_specs=[
            row_spec(D_MODEL), row_spec(MIX_W), row_spec(MIX_W), row_spec(MIX_W),
            pl.BlockSpec(wg.shape, const, pipeline_mode=pl.Buffered(1)), _layer_spec(wa, layer),
            _layer_spec(wb, layer),
            _layer_spec(wc, layer), _layer_spec(wo, layer),
            pl.BlockSpec(g.shape, const), pl.BlockSpec(b.shape, const),
        ],
        out_specs=row_spec(D_MODEL),
        out_shape=jax.ShapeDtypeStruct(h.shape, F32),
        compiler_params=_params(("arbitrary", "arbitrary")),
        name="merge_ln",
    )(h, ya, yb, yc, wg, wa, wb, wc, wo, g, b)


def _ffn_kernel(alpha, h_ref, wg_ref, wu_ref, wd_ref, g_ref, b_ref, o_ref):
    for r0 in range(0, h_ref.shape[0], ROW_SUB):
        rows = slice(r0, r0 + ROW_SUB)
        h = h_ref[rows, :]
        hb = h.astype(BF16)
        acc = None
        for c0 in range(0, FFN_HIDDEN, FFN_CHUNK):
            sl = slice(c0, min(c0 + FFN_CHUNK, FFN_HIDDEN))
            a = _dot(hb, wg_ref[:, sl])
            hid = (a * _sigmoid(a)) * _dot(hb, wu_ref[:, sl])
            part = _dot(hid.astype(BF16), wd_ref[sl, :])
            acc = part if acc is None else acc + part
        o_ref[rows, :] = _layer_norm(alpha * h + acc, g_ref[...], b_ref[...])


def _ffn(h, wg, wu, wd, g, b, alpha, layer):
    bsz, seq, _ = h.shape
    tm = FFN_ROWS
    const = lambda bb, i: (0, 0)
    row_spec = pl.BlockSpec((None, tm, D_MODEL), lambda bb, i: (bb, i, 0))
    return pl.pallas_call(
        functools.partial(_ffn_kernel, alpha),
        grid=(bsz, seq // tm),
        in_specs=[row_spec, _layer_spec(wg, layer), _layer_spec(wu, layer), _layer_spec(wd, layer),
                  pl.BlockSpec(g.shape, const), pl.BlockSpec(b.shape, const)],
        out_specs=row_spec,
        out_shape=jax.ShapeDtypeStruct(h.shape, F32),
        compiler_params=_params(("arbitrary", "arbitrary")),
        name="ffn_ln",
    )(h, wg, wu, wd, g, b)


def kernel(x, w_in, hg_lower_bounds, hg_norm_w, fox_b_f, s5_lambda_re, s5_lambda_im, s5_log_step,
           s5_b_re, s5_b_im, s5_c_re, s5_c_im, s5_d, s5_w_glu, w_br_a, w_br_b, w_br_c, w_out,
           ln1_g, ln1_b, w_ffn_gate, w_ffn_up, w_ffn_down, ln2_g, ln2_b):
    depth = w_in.shape[0]
    bsz, seq, _ = x.shape
    alpha = (2 * depth) ** 0.25
    o_s5 = FOX_COL0 + 3 * MIX_W + FOX_HEADS
    o_gate = o_s5 + MIX_W
    row = lambda a: a.reshape(1, -1)

    bf = lambda a: a.astype(BF16)
    w_in_b, w_glu_b = bf(w_in), bf(s5_w_glu)
    w_a, w_b, w_c, w_o = bf(w_br_a), bf(w_br_b), bf(w_br_c), bf(w_out)
    w_fg, w_fu, w_fd = bf(w_ffn_gate), bf(w_ffn_up), bf(w_ffn_down)
    fox_bias = jnp.pad(fox_b_f, ((0, 0), (0, LANES - FOX_HEADS))).reshape(depth, 1, LANES)

    h = x
    for l in range(depth):
        ws5 = w_in_b[l, :, o_s5:o_gate]
        wgate = w_in_b[l, :, o_gate:]

        q, ka, va, u_tm = _in_proj(h, w_in_b, fox_bias, ws5, l)

        ya = _hgrn2(h, w_in_b, hg_lower_bounds, row(hg_norm_w[l]), l)
        yb = _fox(q, ka, va)

        a_re, a_im, bbar_re, bbar_im = _s5_prep(s5_lambda_re[l], s5_lambda_im[l], s5_log_step[l],
                                                s5_b_re[l], s5_b_im[l])
        wb, wc = _s5_weights(bbar_re, bbar_im, s5_c_re[l], s5_c_im[l])
        yc = _s5(u_tm, a_re, a_im, wb, wc, row(s5_d[l]), w_glu_b, bsz, l)

        h = _merge(h, ya, yb, yc, wgate, w_a, w_b, w_c, w_o, row(ln1_g[l]), row(ln1_b[l]), alpha, l)
        h = _ffn(h, w_fg, w_fu, w_fd, row(ln2_g[l]), row(ln2_b[l]), alpha, l)
    return h
```

```python
import functools
import math

import jax
import jax.numpy as jnp
from jax import lax
from jax.experimental import pallas as pl
from jax.experimental.pallas import tpu as pltpu

F32 = jnp.float32
BF16 = jnp.bfloat16

D_MODEL = 1024
MIX_W = 512
HG_HEADS = 4
HG_KEY = 128
HG_CHUNK = 64
HG_REF_ROW = HG_CHUNK // 2 - 1
FOX_HEADS = 8
FOX_COL0 = 4 * MIX_W
FOX_HEAD_DIM = 64
S5_GROUPS = 32
S5_STATE = 64
S5_GROUP_CH = 16
S5_NSTATE = S5_GROUPS * S5_STATE
FFN_HIDDEN = 2816
LN_EPS = 1e-5
RMS_EPS = 1e-6
EXP2_CLAMP = 115.0
LOG2E = math.log2(math.e)

LANES = 128
MXU_DIM = 256
VMEM_LIMIT = 56 * 1024 * 1024

PROJ_ROWS = 1024
PROJ_GROUP = 4
HG_ROWS = 256
FOX_Q = 512
FOX_KV = 512
S5_PHASES = 4
S5_STEPS = 128
MERGE_ROWS = 1024
FFN_ROWS = 1024
ROW_SUB = 512
FFN_CHUNK = 6 * MXU_DIM


def _dot(a, b):
    return jnp.dot(a, b, preferred_element_type=F32)


def _dot_nt(a, b):
    return lax.dot_general(a, b, (((1,), (1,)), ((), ())), preferred_element_type=F32)


def _dot_tn(a, b):
    return lax.dot_general(a, b, (((0,), (0,)), ((), ())), preferred_element_type=F32)


def _sigmoid(x):
    return 1.0 / (1.0 + jnp.exp2(x * -LOG2E))


def _layer_norm(z, g, b):
    mu = jnp.mean(z, axis=-1, keepdims=True)
    zc = z - mu
    var = jnp.mean(zc * zc, axis=-1, keepdims=True)
    return zc * lax.rsqrt(var + LN_EPS) * g + b


def _cumsum(x, axis):
    n = x.shape[axis]
    idx = lax.broadcasted_iota(jnp.int32, x.shape, axis)
    d = 1
    while d < n:
        x = x + jnp.where(idx >= d, pltpu.roll(x, d, axis=axis), 0.0)
        d *= 2
    return x


def _layer_spec(w, layer):
    zeros = (0,) * (w.ndim - 1)
    return pl.BlockSpec((None,) + w.shape[1:], lambda *_: (layer,) + zeros, pipeline_mode=pl.Buffered(1))


def _params(sem):
    return pltpu.CompilerParams(dimension_semantics=sem, vmem_limit_bytes=VMEM_LIMIT)


def _fox_lanes(h, lane):
    if h % 2 == 0:
        return lane < FOX_HEAD_DIM, FOX_HEAD_DIM
    return lane >= FOX_HEAD_DIM, 0


def _proj_kernel(x_ref, wq_ref, wk_ref, wv_ref, wff_ref, bf_ref, ws5_ref, q_ref, ka_ref, va_ref, s5_ref,
                 cum_ref):
    bsz, tt, _ = x_ref.shape

    @pl.when(pl.program_id(0) == 0)
    def _():
        cum_ref[...] = jnp.zeros_like(cum_ref)

    lane = lax.broadcasted_iota(jnp.int32, (1, LANES), 1)
    for b0 in range(0, bsz, PROJ_GROUP):
        nb = PROJ_GROUP
        rows = nb * tt
        xb = x_ref[b0:b0 + nb].reshape(rows, D_MODEL).astype(BF16)
        k = _dot(xb, wk_ref[...])
        v = _dot(xb, wv_ref[...])
        z = _dot(xb, wff_ref[...]) + bf_ref[...]
        q_ref[b0:b0 + nb] = _dot(xb, wq_ref[...]).astype(BF16).reshape(nb, tt, MIX_W)
        u = _dot(xb, ws5_ref[...])

        logsig = jnp.minimum(z, 0.0) - jnp.log(1.0 + jnp.exp(-jnp.abs(z)))
        t_idx = lax.broadcasted_iota(jnp.int32, (rows, LANES), 0) % tt
        cum = logsig
        d = 1
        while d < tt:
            cum = cum + jnp.where(t_idx >= d, pltpu.roll(cum, d, axis=0), 0.0)
            d *= 2
        carry = cum_ref[b0:b0 + nb, :]
        cum = jnp.concatenate([cum[b * tt:(b + 1) * tt] + carry[b:b + 1] for b in range(nb)], axis=0)
        cum_ref[b0:b0 + nb, :] = jnp.concatenate([cum[(b + 1) * tt - 1:(b + 1) * tt] for b in range(nb)], axis=0)

        bias = cum * (-LOG2E)
        for h in range(FOX_HEADS):
            sl = slice((h // 2) * LANES, (h // 2 + 1) * LANES)
            own, aux = _fox_lanes(h, lane)
            bh = jnp.broadcast_to(bias[:, h:h + 1], (rows, LANES))
            hi = bh.astype(BF16).astype(F32)
            mid = (bh - hi).astype(BF16).astype(F32)
            lo = bh - hi - mid
            extra = jnp.where(lane == aux, hi, jnp.where(lane == aux + 1, mid,
                                                         jnp.where(lane == aux + 2, lo, 0.0)))
            ka = jnp.where(own, k[:, sl], extra).astype(BF16)
            va = jnp.where(own, v[:, sl], jnp.where(lane == aux, 1.0, 0.0)).astype(BF16)
            for b in range(nb):
                ka_ref[b0 + b, h] = ka[b * tt:(b + 1) * tt]
                va_ref[b0 + b, h] = va[b * tt:(b + 1) * tt]

        for b in range(nb):
            for s in range(MIX_W // LANES):
                s5_ref[s, pl.ds(b0 + b, tt, stride=bsz), :] = u[b * tt:(b + 1) * tt, s * LANES:(s + 1) * LANES]


def _in_proj(h, w_all, bf, ws5, layer):
    bsz, seq, _ = h.shape
    tt = PROJ_ROWS // bsz
    fox_spec = lambda j: pl.BlockSpec((None, D_MODEL, MIX_W), lambda i: (layer, 0, FOX_COL0 // MIX_W + j))
    head_spec = pl.BlockSpec((bsz, FOX_HEADS, tt, LANES), lambda i: (0, 0, i, 0))
    head_shape = jax.ShapeDtypeStruct((bsz, FOX_HEADS, seq, LANES), BF16)
    return pl.pallas_call(
        _proj_kernel,
        grid=(seq // tt,),
        in_specs=[
            pl.BlockSpec((bsz, tt, D_MODEL), lambda i: (0, i, 0)),
            fox_spec(0), fox_spec(1), fox_spec(2),
            pl.BlockSpec((None, D_MODEL, LANES), lambda i: (layer, 0, (FOX_COL0 + 3 * MIX_W) // LANES)),
            pl.BlockSpec((None, 1, LANES), lambda i: (layer, 0, 0)),
            _layer_spec(ws5, layer),
        ],
        out_specs=[
            pl.BlockSpec((bsz, tt, MIX_W), lambda i: (0, i, 0)),
            head_spec, head_spec,
            pl.BlockSpec((MIX_W // LANES, bsz * tt, LANES), lambda i: (0, i, 0)),
        ],
        out_shape=[
            jax.ShapeDtypeStruct((bsz, seq, MIX_W), BF16),
            head_shape, head_shape,
            jax.ShapeDtypeStruct((MIX_W // LANES, seq * bsz, LANES), F32),
        ],
        scratch_shapes=[pltpu.VMEM((bsz, LANES), F32)],
        compiler_params=_params(("arbitrary",)),
        name="in_proj",
    )(h, w_all, w_all, w_all, w_all, bf, ws5)


def _hgrn2_kernel(layer, h_ref, hn_ref, w_ref, lbp_ref, nw_ref, o_ref, st_ref, hg0_ref, hg1_ref):
    i = pl.program_id(1)

    @pl.when(i == 0)
    def _():
        st_ref[...] = jnp.zeros_like(st_ref)
        hg0_ref[...] = _dot(h_ref[...].astype(BF16), w_ref[...])

    lbp = lbp_ref[...]
    e = jnp.exp(lbp - jnp.max(lbp, axis=0, keepdims=True))
    sm = e / jnp.sum(e, axis=0, keepdims=True)
    cum = sm[0:1]
    for j in range(1, layer + 1):
        cum = cum + sm[j:j + 1]
    lb = cum - sm[0:1]

    nw = nw_ref[...]
    c = HG_CHUNK
    n_chunks = h_ref.shape[0] // c
    tri = (lax.broadcasted_iota(jnp.int32, (c, c), 0) >= lax.broadcasted_iota(jnp.int32, (c, c), 1))

    def gates(ci, hg_ref):
        rows = slice(ci * c, (ci + 1) * c)
        q = hg_ref[rows, 0:MIX_W]
        fz = hg_ref[rows, MIX_W:2 * MIX_W]
        f = lb + (1.0 - lb) * _sigmoid(fz)
        k = 1.0 - f
        g = _cumsum(jnp.log2(f), 0)
        g_ref = g[HG_REF_ROW:HG_REF_ROW + 1]
        g_last = g[c - 1:c]
        d_ref = g - g_ref
        q_rel = (q * jnp.exp2(jnp.minimum(d_ref, EXP2_CLAMP))).astype(BF16)
        k_rel = (k * jnp.exp2(jnp.minimum(-d_ref, EXP2_CLAMP))).astype(BF16)
        q_in = (q * jnp.exp2(g)).astype(BF16)
        k_end = (k * jnp.exp2(g_last - g)).astype(BF16)
        s_decay = jnp.exp2(g_last)
        return q_rel, k_rel, q_in, k_end, s_decay

    def mix(ci, hg_ref, operands):
        rows = slice(ci * c, (ci + 1) * c)
        q_rel, k_rel, q_in, k_end, s_decay = operands
        v = hg_ref[rows, 2 * MIX_W:3 * MIX_W]
        gate = hg_ref[rows, 3 * MIX_W:4 * MIX_W]
        for h in range(HG_HEADS):
            sl = slice(h * HG_KEY, (h + 1) * HG_KEY)
            st = st_ref[h]
            v_t = v[:, sl].T.astype(BF16)
            scores = jnp.where(tri, _dot_nt(q_rel[:, sl], k_rel[:, sl]), 0.0)
            lhs = jnp.concatenate([q_in[:, sl], scores.astype(BF16)], axis=1)
            rhs_t = jnp.concatenate([st.astype(BF16), v_t], axis=1)
            o = _dot_nt(lhs, rhs_t)
            st_ref[h] = st * s_decay[:, sl] + _dot(v_t, k_end[:, sl])
            o = o * lax.rsqrt(jnp.mean(o * o, axis=-1, keepdims=True) + RMS_EPS) * nw
            gh = gate[:, sl]
            o_ref[rows, sl] = (o * (gh * _sigmoid(gh))).astype(BF16)

    def tile(cur_ref, nxt_ref):
        hn = hn_ref[...].astype(BF16)
        cols = w_ref.shape[1] // n_chunks
        operands = gates(0, cur_ref)
        for ci in range(n_chunks):
            ahead = gates(ci + 1, cur_ref) if ci + 1 < n_chunks else None
            mix(ci, cur_ref, operands)
            nxt_ref[:, ci * cols:(ci + 1) * cols] = _dot(hn, w_ref[:, ci * cols:(ci + 1) * cols])
            operands = ahead

    @pl.when(i % 2 == 0)
    def _():
        tile(hg0_ref, hg1_ref)

    @pl.when(i % 2 == 1)
    def _():
        tile(hg1_ref, hg0_ref)


def _hgrn2(h, w_all, lbp, nw, layer):
    bsz, seq, _ = h.shape
    tr = HG_ROWS
    n_tiles = seq // tr
    hg = pltpu.VMEM((tr, 4 * MIX_W), F32)
    return pl.pallas_call(
        functools.partial(_hgrn2_kernel, layer),
        grid=(bsz, n_tiles),
        in_specs=[
            pl.BlockSpec((None, tr, D_MODEL), lambda b, i: (b, i, 0)),
            pl.BlockSpec((None, tr, D_MODEL), lambda b, i: (b, jnp.minimum(i + 1, n_tiles - 1), 0)),
            pl.BlockSpec((None, D_MODEL, 4 * MIX_W), lambda b, i: (layer, 0, 0)),
            pl.BlockSpec(lbp.shape, lambda b, i: (0, 0)),
            _layer_spec(nw, layer),
        ],
        out_specs=pl.BlockSpec((None, tr, MIX_W), lambda b, i: (b, i, 0)),
        out_shape=jax.ShapeDtypeStruct((bsz, seq, MIX_W), BF16),
        scratch_shapes=[pltpu.VMEM((HG_HEADS, HG_KEY, HG_KEY), F32), hg, hg],
        compiler_params=_params(("arbitrary", "arbitrary")),
        name="hgrn2",
    )(h, h, w_all, lbp, nw)


def _fox_kernel(q_ref, ka_ref, va_ref, o_ref, qa_ref, m_ref, acc_ref, s_ref):
    qi = pl.program_id(1)
    tq, tk = FOX_Q, FOX_KV
    n_q = q_ref.shape[0] // tq
    lane = lax.broadcasted_iota(jnp.int32, (1, LANES), 1)
    causal = (lax.broadcasted_iota(jnp.int32, (tq, tk), 0) >= lax.broadcasted_iota(jnp.int32, (tq, tk), 1))

    def load_queries(t):
        for h in range(FOX_HEADS):
            sl = slice((h // 2) * LANES, (h // 2 + 1) * LANES)
            own, aux = _fox_lanes(h, lane)
            ones = jnp.where((lane >= aux) & (lane < aux + 3), 1.0, 0.0)
            q = q_ref[t * tq:(t + 1) * tq, sl].astype(F32) * (FOX_HEAD_DIM ** -0.5 * LOG2E)
            qa_ref[h] = jnp.where(own, q, ones).astype(BF16)

    def reset_state():
        m_ref[...] = jnp.full(m_ref.shape, -1e30, F32)
        acc_ref[...] = jnp.zeros(acc_ref.shape, F32)

    hq = tq // 2
    spans = ((slice(0, hq), tk // 2), (slice(hq, tq), tk))

    def logits(h, kb, buf, diag=False):
        if not diag:
            s_ref[buf, h] = _dot_nt(qa_ref[h], ka_ref[h, kb * tk:(kb + 1) * tk, :])
            return
        for rows, nk in spans:
            s_ref[buf, h, rows, 0:nk] = _dot_nt(qa_ref[h, rows, :], ka_ref[h, kb * tk:kb * tk + nk, :])

    def softmax_pv(h, kb, s, rows):
        nk = s.shape[1]
        m_prev = m_ref[h, rows, :]
        m_new = jnp.maximum(m_prev, jnp.max(s, axis=-1, keepdims=True))
        p = jnp.concatenate([jnp.exp2(s[:, j * LANES:(j + 1) * LANES] - m_new)
                             for j in range(nk // LANES)], axis=1)
        pv = _dot(p.astype(BF16), va_ref[h, kb * tk:kb * tk + nk, :])
        acc_ref[h, rows, :] = jnp.exp2(m_prev - m_new) * acc_ref[h, rows, :] + pv
        m_ref[h, rows, :] = m_new

    def update(h, kb, buf, masked):
        if not masked:
            softmax_pv(h, kb, s_ref[buf, h], slice(0, tq))
            return
        for rows, nk in spans:
            s = s_ref[buf, h, rows, 0:nk]
            softmax_pv(h, kb, jnp.where(causal[rows, 0:nk], s, -1e30), rows)

    def write_out():
        for pair in range(FOX_HEADS // 2):
            a = acc_ref[2 * pair]
            b = acc_ref[2 * pair + 1]
            o = jnp.where(lane < FOX_HEAD_DIM, a / a[:, FOX_HEAD_DIM:FOX_HEAD_DIM + 1], b / b[:, 0:1])
            o_ref[:, pair * LANES:(pair + 1) * LANES] = o.astype(BF16)

    start = [0]
    for t in range(1, n_q):
        start.append(1 - (start[t - 1] + t - 1) % 2)
    for t in range(n_q):
        @pl.when(qi == t)
        def _(t=t):
            if t == 0:
                load_queries(0)
                reset_state()
                for h in range(FOX_HEADS):
                    logits(h, 0, start[0], diag=True)
            for j in range(t):
                for h in range(FOX_HEADS):
                    logits(h, j + 1, (start[t] + j + 1) % 2, diag=(j + 1 == t))
                    update(h, j, (start[t] + j) % 2, False)
            if t + 1 < n_q:
                load_queries(t + 1)
            for h in range(FOX_HEADS):
                update(h, t, (start[t] + t) % 2, True)
                if t + 1 < n_q:
                    logits(h, 0, start[t + 1])
            write_out()
            if t + 1 < n_q:
                reset_state()


def _fox(q, ka, va):
    bsz, seq, _ = q.shape
    tq = FOX_Q
    state = pltpu.VMEM((FOX_HEADS, tq, LANES), F32)
    full_spec = pl.BlockSpec((None, FOX_HEADS, seq, LANES), lambda b, i: (b, 0, 0, 0))
    return pl.pallas_call(
        _fox_kernel,
        grid=(bsz, seq // tq),
        in_specs=[pl.BlockSpec((None, seq, MIX_W), lambda b, i: (b, 0, 0)), full_spec, full_spec],
        out_specs=pl.BlockSpec((None, tq, MIX_W), lambda b, i: (b, i, 0)),
        out_shape=jax.ShapeDtypeStruct((bsz, seq, MIX_W), BF16),
        scratch_shapes=[pltpu.VMEM((FOX_HEADS, tq, LANES), BF16), state, state,
                        pltpu.VMEM((2, FOX_HEADS, tq, FOX_KV), F32)],
        compiler_params=_params(("arbitrary", "arbitrary")),
        name="fox_attn",
    )(q, ka, va)


def _s5_prep_kernel(lre_ref, lim_ref, ls_ref, bre_ref, bim_ref, are_ref, aim_ref, bbr_ref, bbi_ref):
    lr, li = lre_ref[...], lim_ref[...]
    dt = jnp.exp(ls_ref[...])
    mag = jnp.exp(lr * dt)
    ar = mag * jnp.cos(li * dt)
    ai = mag * jnp.sin(li * dt)
    den = lr * lr + li * li
    cr = ((ar - 1.0) * lr + ai * li) / den
    ci = (ai * lr - (ar - 1.0) * li) / den
    br, bi = bre_ref[...], bim_ref[...]
    bbr_ref[...] = cr * br - ci * bi
    bbi_ref[...] = cr * bi + ci * br
    are_ref[...] = jnp.broadcast_to(ar, are_ref.shape)
    aim_ref[...] = jnp.broadcast_to(ai, aim_ref.shape)


def _s5_prep(lam_re, lam_im, log_step, b_re, b_im):
    depth = lam_re.shape[0]
    row = lambda a: a.reshape(depth, 1, S5_NSTATE)
    ls = jnp.repeat(log_step, S5_STATE, axis=1).reshape(depth, 1, S5_NSTATE)
    to_hp = lambda b: b.transpose(0, 3, 1, 2).reshape(depth, S5_GROUP_CH, S5_NSTATE)
    spec = lambda r: pl.BlockSpec((None, r, S5_NSTATE), lambda l: (l, 0, 0))
    vec = jax.ShapeDtypeStruct((depth, 8, S5_NSTATE), F32)
    mat = jax.ShapeDtypeStruct((depth, S5_GROUP_CH, S5_NSTATE), F32)
    return pl.pallas_call(
        _s5_prep_kernel,
        grid=(depth,),
        in_specs=[spec(1), spec(1), spec(1), spec(S5_GROUP_CH), spec(S5_GROUP_CH)],
        out_specs=[spec(8), spec(8), spec(S5_GROUP_CH), spec(S5_GROUP_CH)],
        out_shape=[vec, vec, mat, mat],
        name="s5_prep",
    )(row(lam_re), row(lam_im), ls, to_hp(b_re), to_hp(b_im))


def _s5_kernel(u_ref, are_ref, aim_ref, wb_ref, wc_ref, d_ref, wglu_ref, o_ref, x_ref, st_ref, y_ref):
    @pl.when(pl.program_id(0) == 0)
    def _():
        st_ref[...] = jnp.zeros_like(st_ref)

    n = S5_NSTATE
    n_tiles = 2 * n // MXU_DIM
    total_rows = u_ref.shape[1]
    steps = total_rows // 8
    width = 1024

    def load_u(rows):
        return jnp.concatenate([u_ref[s, rows, :] for s in range(MIX_W // LANES)], axis=1)

    def project_in(rows):
        ub = load_u(rows).astype(BF16)
        for j in range(n_tiles):
            kh = ((j % (n_tiles // 2)) * MXU_DIM // S5_STATE * S5_GROUP_CH) // MXU_DIM
            x_ref[rows, j * MXU_DIM:(j + 1) * MXU_DIM] = _dot(ub[:, kh * MXU_DIM:(kh + 1) * MXU_DIM], wb_ref[j])

    def scan(t0, t1, state):
        for i, c0 in enumerate(range(0, n, width)):
            re_sl = slice(c0, c0 + width)
            im_sl = slice(n + c0, n + c0 + width)
            ar = are_ref[:, re_sl]
            ai = aim_ref[:, re_sl]
            xr, xi = state[i]
            for t in range(t0, t1):
                rows = slice(t * 8, t * 8 + 8)
                xr, xi = (ar * xr - ai * xi + x_ref[rows, re_sl], ar * xi + ai * xr + x_ref[rows, im_sl])
                x_ref[rows, re_sl] = xr
                x_ref[rows, im_sl] = xi
            state[i] = (xr, xi)

    def project_out(rows):
        halves = []
        per_half = n // MXU_DIM // 2
        for nh in range(MIX_W // MXU_DIM):
            acc_re = None
            acc_im = None
            for kk in range(per_half):
                kr = nh * per_half + kk
                ki = n // MXU_DIM + kr
                pr = _dot(x_ref[rows, kr * MXU_DIM:(kr + 1) * MXU_DIM].astype(BF16), wc_ref[kr])
                pi = _dot(x_ref[rows, ki * MXU_DIM:(ki + 1) * MXU_DIM].astype(BF16), wc_ref[ki])
                acc_re = pr if acc_re is None else acc_re + pr
                acc_im = pi if acc_im is None else acc_im + pi
            halves.append(acc_re - acc_im)
        y = jnp.concatenate(halves, axis=1) + d_ref[...] * load_u(rows)
        y = 0.5 * y * (1.0 + jnp.tanh(math.sqrt(2.0 / math.pi) * (y + 0.044715 * (y * y * y))))
        y = y * _sigmoid(_dot(y.astype(BF16), wglu_ref[...]))
        for s in range(MIX_W // LANES):
            y_ref[s, rows, :] = y[:, s * LANES:(s + 1) * LANES]

    phase_rows = total_rows // S5_PHASES
    phases = [slice(p * phase_rows, (p + 1) * phase_rows) for p in range(S5_PHASES)]
    for rows in phases:
        project_in(rows)
    state = [(st_ref[:, c0:c0 + width], st_ref[:, n + c0:n + c0 + width]) for c0 in range(0, n, width)]
    for p, rows in enumerate(phases):
        scan(p * steps // S5_PHASES, (p + 1) * steps // S5_PHASES, state)
        project_out(rows)
    for i, c0 in enumerate(range(0, n, width)):
        st_ref[:, c0:c0 + width] = state[i][0]
        st_ref[:, n + c0:n + c0 + width] = state[i][1]
    bsz = o_ref.shape[0]
    for b in range(bsz):
        o_ref[b] = jnp.concatenate([y_ref[s, pl.ds(b, steps, stride=bsz), :]
                                    for s in range(MIX_W // LANES)], axis=1).astype(BF16)


def _s5(u_tm, a_re, a_im, wb, wc, d, wglu, bsz, layer):
    rows = S5_STEPS * bsz
    total = u_tm.shape[1]
    return pl.pallas_call(
        _s5_kernel,
        grid=(total // rows,),
        in_specs=[
            pl.BlockSpec((MIX_W // LANES, rows, LANES), lambda i: (0, i, 0)),
            _layer_spec(a_re, layer), _layer_spec(a_im, layer),
            _layer_spec(wb, layer), _layer_spec(wc, layer),
            _layer_spec(d, layer), _layer_spec(wglu, layer),
        ],
        out_specs=pl.BlockSpec((bsz, S5_STEPS, MIX_W), lambda i: (0, i, 0)),
        out_shape=jax.ShapeDtypeStruct((bsz, total // bsz, MIX_W), BF16),
        scratch_shapes=[pltpu.VMEM((rows, 2 * S5_NSTATE), F32), pltpu.VMEM((8, 2 * S5_NSTATE), F32),
                        pltpu.VMEM((MIX_W // LANES, rows, LANES), F32)],
        compiler_params=_params(("arbitrary",)),
        name="s5",
    )(u_tm, a_re, a_im, wb, wc, d, wglu)


def _s5_weights(bbar_re, bbar_im, c_re, c_im):
    half = S5_NSTATE // MXU_DIM
    g_per_tile = MXU_DIM // S5_STATE
    g_per_slab = MXU_DIM // S5_GROUP_CH
    depth = bbar_re.shape[0]
    t = jnp.arange(half)[:, None, None]
    r = jnp.arange(MXU_DIM)[None, :, None]
    c = jnp.arange(MXU_DIM)[None, None, :]
    slab = (t * g_per_tile) // g_per_slab
    b_keep = slab * g_per_slab + r // S5_GROUP_CH == t * g_per_tile + c // S5_STATE
    c_keep = t * g_per_tile + r // S5_STATE == slab * g_per_slab + c // S5_GROUP_CH

    def b_tiles(bb):
        bb = bb.reshape(depth, S5_GROUP_CH, half, MXU_DIM).transpose(0, 2, 1, 3)
        return jnp.where(b_keep, jnp.tile(bb, (1, 1, g_per_slab, 1)), 0.0)

    def c_tiles(cc):
        cc = cc.transpose(0, 1, 3, 2).reshape(depth, half, MXU_DIM, S5_GROUP_CH)
        return jnp.where(c_keep, jnp.tile(cc, (1, 1, 1, g_per_slab)), 0.0)

    wb = jnp.concatenate([b_tiles(bbar_re), b_tiles(bbar_im)], axis=1).astype(BF16)
    wc = jnp.concatenate([c_tiles(c_re), c_tiles(c_im)], axis=1).astype(BF16)
    return wb, wc


def _merge_kernel(alpha, h_ref, ya_ref, yb_ref, yc_ref, wg_ref, wa_ref, wb_ref, wc_ref, wo_ref,
                  g_ref, b_ref, o_ref):
    for r0 in range(0, h_ref.shape[0], ROW_SUB):
        rows = slice(r0, r0 + ROW_SUB)
        h = h_ref[rows, :]
        hb = h.astype(BF16)
        merged = None
        for i, (y_ref, w_ref) in enumerate(((ya_ref, wa_ref), (yb_ref, wb_ref), (yc_ref, wc_ref))):
            gate = _sigmoid(_dot(hb, wg_ref[:, i * D_MODEL:(i + 1) * D_MODEL]))
            term = gate * _dot(y_ref[rows, :], w_ref[...])
            merged = term if merged is None else merged + term
        mix = _dot(merged.astype(BF16), wo_ref[...])
        o_ref[rows, :] = _layer_norm(alpha * h + mix, g_ref[...], b_ref[...])


def _merge(h, ya, yb, yc, wg, wa, wb, wc, wo, g, b, alpha, layer):
    bsz, seq, _ = h.shape
    tm = MERGE_ROWS
    row_spec = lambda w: pl.BlockSpec((None, tm, w), lambda bb, i: (bb, i, 0))
    return pl.pallas_call(
        functools.partial(_merge_kernel, alpha),
        grid=(bsz, seq // tm),
        in_specs=[
            row_spec(D_MODEL), row_spec(MIX_W), row_spec(MIX_W), row_spec(MIX_W),
            _layer_spec(wg, layer), _layer_spec(wa, layer), _layer_spec(wb, layer),
            _layer_spec(wc, layer), _layer_spec(wo, layer), _layer_spec(g, layer), _layer_spec(b, layer),
        ],
        out_specs=row_spec(D_MODEL),
        out_shape=jax.ShapeDtypeStruct(h.shape, F32),
        compiler_params=_params(("arbitrary", "arbitrary")),
        name="merge_ln",
    )(h, ya, yb, yc, wg, wa, wb, wc, wo, g, b)


def _ffn_kernel(alpha, h_ref, wg_ref, wu_ref, wd_ref, g_ref, b_ref, o_ref):
    for r0 in range(0, h_ref.shape[0], ROW_SUB):
        rows = slice(r0, r0 + ROW_SUB)
        h = h_ref[rows, :]
        hb = h.astype(BF16)
        acc = None
        for c0 in range(0, FFN_HIDDEN, FFN_CHUNK):
            sl = slice(c0, min(c0 + FFN_CHUNK, FFN_HIDDEN))
            a = _dot(hb, wg_ref[:, sl])
            hid = (a * _sigmoid(a)) * _dot(hb, wu_ref[:, sl])
            part = _dot(hid.astype(BF16), wd_ref[sl, :])
            acc = part if acc is None else acc + part
        o_ref[rows, :] = _layer_norm(alpha * h + acc, g_ref[...], b_ref[...])


def _ffn(h, wg, wu, wd, g, b, alpha, layer):
    bsz, seq, _ = h.shape
    tm = FFN_ROWS
    row_spec = pl.BlockSpec((None, tm, D_MODEL), lambda bb, i: (bb, i, 0))
    return pl.pallas_call(
        functools.partial(_ffn_kernel, alpha),
        grid=(bsz, seq // tm),
        in_specs=[row_spec, _layer_spec(wg, layer), _layer_spec(wu, layer), _layer_spec(wd, layer),
                  _layer_spec(g, layer), _layer_spec(b, layer)],
        out_specs=row_spec,
        out_shape=jax.ShapeDtypeStruct(h.shape, F32),
        compiler_params=_params(("arbitrary", "arbitrary")),
        name="ffn_ln",
    )(h, wg, wu, wd, g, b)


def kernel(x, w_in, hg_lower_bounds, hg_norm_w, fox_b_f, s5_lambda_re, s5_lambda_im, s5_log_step,
           s5_b_re, s5_b_im, s5_c_re, s5_c_im, s5_d, s5_w_glu, w_br_a, w_br_b, w_br_c, w_out,
           ln1_g, ln1_b, w_ffn_gate, w_ffn_up, w_ffn_down, ln2_g, ln2_b):
    depth = w_in.shape[0]
    bsz, seq, _ = x.shape
    alpha = (2 * depth) ** 0.25
    o_s5 = FOX_COL0 + 3 * MIX_W + FOX_HEADS
    o_gate = o_s5 + MIX_W

    bf = lambda a: a.astype(BF16)
    vec = lambda a: a.reshape(depth, 1, -1)
    w_main = bf(w_in[:, :, :FOX_COL0 + 3 * MIX_W + LANES])
    w_s5, w_gate = bf(w_in[:, :, o_s5:o_gate]), bf(w_in[:, :, o_gate:])
    w_glu_b = bf(s5_w_glu)
    w_a, w_b, w_c, w_o = bf(w_br_a), bf(w_br_b), bf(w_br_c), bf(w_out)
    w_fg, w_fu, w_fd = bf(w_ffn_gate), bf(w_ffn_up), bf(w_ffn_down)
    fox_bias = jnp.pad(fox_b_f, ((0, 0), (0, LANES - FOX_HEADS))).reshape(depth, 1, LANES)
    a_re, a_im, bbar_re, bbar_im = _s5_prep(s5_lambda_re, s5_lambda_im, s5_log_step, s5_b_re, s5_b_im)
    wb, wc = _s5_weights(bbar_re, bbar_im, s5_c_re, s5_c_im)
    norm_w, s5_skip = vec(hg_norm_w), vec(s5_d)
    g1, b1, g2, b2 = vec(ln1_g), vec(ln1_b), vec(ln2_g), vec(ln2_b)

    h = x
    for l in range(depth):
        q, ka, va, u_tm = _in_proj(h, w_main, fox_bias, w_s5, l)
        ya = _hgrn2(h, w_main, hg_lower_bounds, norm_w, l)
        yb = _fox(q, ka, va)
        yc = _s5(u_tm, a_re, a_im, wb, wc, s5_skip, w_glu_b, bsz, l)
        h = _merge(h, ya, yb, yc, w_gate, w_a, w_b, w_c, w_o, g1, b1, alpha, l)
        h = _ffn(h, w_fg, w_fu, w_fd, g2, b2, alpha, l)
    return h
```

```python
import functools
import math

import jax
import jax.numpy as jnp
from jax import lax
from jax.experimental import pallas as pl
from jax.experimental.pallas import tpu as pltpu

F32 = jnp.float32
BF16 = jnp.bfloat16

D_MODEL = 1024
MIX_W = 512
HG_HEADS = 4
HG_KEY = 128
HG_CHUNK = 64
HG_REF_ROW = HG_CHUNK // 2 - 1
FOX_HEADS = 8
FOX_COL0 = 4 * MIX_W
FOX_HEAD_DIM = 64
S5_GROUPS = 32
S5_STATE = 64
S5_GROUP_CH = 16
S5_NSTATE = S5_GROUPS * S5_STATE
FFN_HIDDEN = 2816
LN_EPS = 1e-5
RMS_EPS = 1e-6
EXP2_CLAMP = 115.0
LOG2E = math.log2(math.e)

LANES = 128
MXU_DIM = 256
VMEM_LIMIT = 56 * 1024 * 1024

PROJ_ROWS = 1024
PROJ_GROUP = 4
HG_ROWS = 256
FOX_Q = 512
FOX_KV = 512
S5_PHASES = 4
S5_STEPS = 128
MERGE_ROWS = 1024
FFN_ROWS = 1024
ROW_SUB = 512
FFN_CHUNK = 6 * MXU_DIM


def _dot(a, b):
    return jnp.dot(a, b, preferred_element_type=F32)


def _dot_nt(a, b):
    return lax.dot_general(a, b, (((1,), (1,)), ((), ())), preferred_element_type=F32)


def _dot_tn(a, b):
    return lax.dot_general(a, b, (((0,), (0,)), ((), ())), preferred_element_type=F32)


def _sigmoid(x):
    return 1.0 / (1.0 + jnp.exp2(x * -LOG2E))


def _layer_norm(z, g, b):
    mu = jnp.mean(z, axis=-1, keepdims=True)
    zc = z - mu
    var = jnp.mean(zc * zc, axis=-1, keepdims=True)
    return zc * lax.rsqrt(var + LN_EPS) * g + b


def _cumsum(x, axis):
    n = x.shape[axis]
    idx = lax.broadcasted_iota(jnp.int32, x.shape, axis)
    d = 1
    while d < n:
        x = x + jnp.where(idx >= d, pltpu.roll(x, d, axis=axis), 0.0)
        d *= 2
    return x


def _layer_spec(w, layer):
    zeros = (0,) * (w.ndim - 1)
    return pl.BlockSpec((None,) + w.shape[1:], lambda *_: (layer,) + zeros, pipeline_mode=pl.Buffered(1))


def _params(sem):
    return pltpu.CompilerParams(dimension_semantics=sem, vmem_limit_bytes=VMEM_LIMIT)


def _fox_lanes(h, lane):
    if h % 2 == 0:
        return lane < FOX_HEAD_DIM, FOX_HEAD_DIM
    return lane >= FOX_HEAD_DIM, 0


def _proj_kernel(x_ref, wq_ref, wk_ref, wv_ref, wff_ref, bf_ref, ws5_ref, q_ref, ka_ref, va_ref, s5_ref,
                 cum_ref):
    bsz, tt, _ = x_ref.shape

    @pl.when(pl.program_id(0) == 0)
    def _():
        cum_ref[...] = jnp.zeros_like(cum_ref)

    lane = lax.broadcasted_iota(jnp.int32, (1, LANES), 1)
    for b0 in range(0, bsz, PROJ_GROUP):
        nb = PROJ_GROUP
        rows = nb * tt
        xb = x_ref[b0:b0 + nb].reshape(rows, D_MODEL).astype(BF16)
        k = _dot(xb, wk_ref[...])
        v = _dot(xb, wv_ref[...])
        z = _dot(xb, wff_ref[...]) + bf_ref[...]
        q_ref[b0:b0 + nb] = _dot(xb, wq_ref[...]).astype(BF16).reshape(nb, tt, MIX_W)
        u = _dot(xb, ws5_ref[...])

        logsig = jnp.minimum(z, 0.0) - jnp.log(1.0 + jnp.exp(-jnp.abs(z)))
        t_idx = lax.broadcasted_iota(jnp.int32, (rows, LANES), 0) % tt
        cum = logsig
        d = 1
        while d < tt:
            cum = cum + jnp.where(t_idx >= d, pltpu.roll(cum, d, axis=0), 0.0)
            d *= 2
        carry = cum_ref[b0:b0 + nb, :]
        cum = jnp.concatenate([cum[b * tt:(b + 1) * tt] + carry[b:b + 1] for b in range(nb)], axis=0)
        cum_ref[b0:b0 + nb, :] = jnp.concatenate([cum[(b + 1) * tt - 1:(b + 1) * tt] for b in range(nb)], axis=0)

        bias = cum * (-LOG2E)
        for h in range(FOX_HEADS):
            sl = slice((h // 2) * LANES, (h // 2 + 1) * LANES)
            own, aux = _fox_lanes(h, lane)
            bh = jnp.broadcast_to(bias[:, h:h + 1], (rows, LANES))
            hi = bh.astype(BF16).astype(F32)
            mid = (bh - hi).astype(BF16).astype(F32)
            lo = bh - hi - mid
            extra = jnp.where(lane == aux, hi, jnp.where(lane == aux + 1, mid,
                                                         jnp.where(lane == aux + 2, lo, 0.0)))
            ka = jnp.where(own, k[:, sl], extra).astype(BF16)
            va = jnp.where(own, v[:, sl], jnp.where(lane == aux, 1.0, 0.0)).astype(BF16)
            for b in range(nb):
                ka_ref[b0 + b, h] = ka[b * tt:(b + 1) * tt]
                va_ref[b0 + b, h] = va[b * tt:(b + 1) * tt]

        for b in range(nb):
            for s in range(MIX_W // LANES):
                s5_ref[s, pl.ds(b0 + b, tt, stride=bsz), :] = u[b * tt:(b + 1) * tt, s * LANES:(s + 1) * LANES]


def _in_proj(h, w_all, bf, ws5, layer):
    bsz, seq, _ = h.shape
    tt = PROJ_ROWS // bsz
    fox_spec = lambda j: pl.BlockSpec((None, D_MODEL, MIX_W), lambda i: (layer, 0, FOX_COL0 // MIX_W + j))
    head_spec = pl.BlockSpec((bsz, FOX_HEADS, tt, LANES), lambda i: (0, 0, i, 0))
    head_shape = jax.ShapeDtypeStruct((bsz, FOX_HEADS, seq, LANES), BF16)
    return pl.pallas_call(
        _proj_kernel,
        grid=(seq // tt,),
        in_specs=[
            pl.BlockSpec((bsz, tt, D_MODEL), lambda i: (0, i, 0)),
            fox_spec(0), fox_spec(1), fox_spec(2),
            pl.BlockSpec((None, D_MODEL, LANES), lambda i: (layer, 0, (FOX_COL0 + 3 * MIX_W) // LANES)),
            pl.BlockSpec((None, 1, LANES), lambda i: (layer, 0, 0)),
            _layer_spec(ws5, layer),
        ],
        out_specs=[
            pl.BlockSpec((bsz, tt, MIX_W), lambda i: (0, i, 0)),
            head_spec, head_spec,
            pl.BlockSpec((MIX_W // LANES, bsz * tt, LANES), lambda i: (0, i, 0)),
        ],
        out_shape=[
            jax.ShapeDtypeStruct((bsz, seq, MIX_W), BF16),
            head_shape, head_shape,
            jax.ShapeDtypeStruct((MIX_W // LANES, seq * bsz, LANES), F32),
        ],
        scratch_shapes=[pltpu.VMEM((bsz, LANES), F32)],
        compiler_params=_params(("arbitrary",)),
        name="in_proj",
    )(h, w_all, w_all, w_all, w_all, bf, ws5)


def _hgrn2_kernel(layer, h_ref, hn_ref, w_ref, lbp_ref, nw_ref, o_ref, st_ref, hg0_ref, hg1_ref):
    i = pl.program_id(1)

    @pl.when(i == 0)
    def _():
        st_ref[...] = jnp.zeros_like(st_ref)
        hg0_ref[...] = _dot(h_ref[...].astype(BF16), w_ref[...])

    lbp = lbp_ref[...]
    e = jnp.exp(lbp - jnp.max(lbp, axis=0, keepdims=True))
    sm = e / jnp.sum(e, axis=0, keepdims=True)
    cum = sm[0:1]
    for j in range(1, layer + 1):
        cum = cum + sm[j:j + 1]
    lb = cum - sm[0:1]

    nw = nw_ref[...]
    c = HG_CHUNK
    n_chunks = h_ref.shape[0] // c
    tri = (lax.broadcasted_iota(jnp.int32, (c, c), 0) >= lax.broadcasted_iota(jnp.int32, (c, c), 1))

    def gates(ci, hg_ref):
        rows = slice(ci * c, (ci + 1) * c)
        q = hg_ref[rows, 0:MIX_W]
        fz = hg_ref[rows, MIX_W:2 * MIX_W]
        f = lb + (1.0 - lb) * _sigmoid(fz)
        k = 1.0 - f
        g = _cumsum(jnp.log2(f), 0)
        g_ref = g[HG_REF_ROW:HG_REF_ROW + 1]
        g_last = g[c - 1:c]
        d_ref = g - g_ref
        q_rel = (q * jnp.exp2(jnp.minimum(d_ref, EXP2_CLAMP))).astype(BF16)
        k_rel = (k * jnp.exp2(jnp.minimum(-d_ref, EXP2_CLAMP))).astype(BF16)
        q_in = (q * jnp.exp2(g)).astype(BF16)
        k_end = (k * jnp.exp2(g_last - g)).astype(BF16)
        s_decay = jnp.exp2(g_last)
        return q_rel, k_rel, q_in, k_end, s_decay

    def mix(ci, hg_ref, operands):
        rows = slice(ci * c, (ci + 1) * c)
        q_rel, k_rel, q_in, k_end, s_decay = operands
        v = hg_ref[rows, 2 * MIX_W:3 * MIX_W]
        gate = hg_ref[rows, 3 * MIX_W:4 * MIX_W]
        for h in range(HG_HEADS):
            sl = slice(h * HG_KEY, (h + 1) * HG_KEY)
            st = st_ref[h]
            v_t = v[:, sl].T.astype(BF16)
            scores = jnp.where(tri, _dot_nt(q_rel[:, sl], k_rel[:, sl]), 0.0)
            lhs = jnp.concatenate([q_in[:, sl], scores.astype(BF16)], axis=1)
            rhs_t = jnp.concatenate([st.astype(BF16), v_t], axis=1)
            o = _dot_nt(lhs, rhs_t)
            st_ref[h] = st * s_decay[:, sl] + _dot(v_t, k_end[:, sl])
            o = o * lax.rsqrt(jnp.mean(o * o, axis=-1, keepdims=True) + RMS_EPS) * nw
            gh = gate[:, sl]
            o_ref[rows, sl] = (o * (gh * _sigmoid(gh))).astype(BF16)

    def tile(cur_ref, nxt_ref):
        hn = hn_ref[...].astype(BF16)
        cols = w_ref.shape[1] // n_chunks
        operands = gates(0, cur_ref)
        for ci in range(n_chunks):
            ahead = gates(ci + 1, cur_ref) if ci + 1 < n_chunks else None
            mix(ci, cur_ref, operands)
            nxt_ref[:, ci * cols:(ci + 1) * cols] = _dot(hn, w_ref[:, ci * cols:(ci + 1) * cols])
            operands = ahead

    @pl.when(i % 2 == 0)
    def _():
        tile(hg0_ref, hg1_ref)

    @pl.when(i % 2 == 1)
    def _():
        tile(hg1_ref, hg0_ref)


def _hgrn2(h, w_all, lbp, nw, layer):
    bsz, seq, _ = h.shape
    tr = HG_ROWS
    n_tiles = seq // tr
    hg = pltpu.VMEM((tr, 4 * MIX_W), F32)
    return pl.pallas_call(
        functools.partial(_hgrn2_kernel, layer),
        grid=(bsz, n_tiles),
        in_specs=[
            pl.BlockSpec((None, tr, D_MODEL), lambda b, i: (b, i, 0)),
            pl.BlockSpec((None, tr, D_MODEL), lambda b, i: (b, jnp.minimum(i + 1, n_tiles - 1), 0)),
            pl.BlockSpec((None, D_MODEL, 4 * MIX_W), lambda b, i: (layer, 0, 0)),
            pl.BlockSpec(lbp.shape, lambda b, i: (0, 0)),
            _layer_spec(nw, layer),
        ],
        out_specs=pl.BlockSpec((None, tr, MIX_W), lambda b, i: (b, i, 0)),
        out_shape=jax.ShapeDtypeStruct((bsz, seq, MIX_W), BF16),
        scratch_shapes=[pltpu.VMEM((HG_HEADS, HG_KEY, HG_KEY), F32), hg, hg],
        compiler_params=_params(("arbitrary", "arbitrary")),
        name="hgrn2",
    )(h, h, w_all, lbp, nw)


def _fox_kernel(q_ref, ka_ref, va_ref, o_ref, qa_ref, m_ref, acc_ref, s_ref):
    qi = pl.program_id(1)
    tq, tk = FOX_Q, FOX_KV
    n_q = q_ref.shape[0] // tq
    lane = lax.broadcasted_iota(jnp.int32, (1, LANES), 1)
    causal = (lax.broadcasted_iota(jnp.int32, (tq, tk), 0) >= lax.broadcasted_iota(jnp.int32, (tq, tk), 1))

    def load_queries(t):
        for h in range(FOX_HEADS):
            sl = slice((h // 2) * LANES, (h // 2 + 1) * LANES)
            own, aux = _fox_lanes(h, lane)
            ones = jnp.where((lane >= aux) & (lane < aux + 3), 1.0, 0.0)
            q = q_ref[t * tq:(t + 1) * tq, sl].astype(F32) * (FOX_HEAD_DIM ** -0.5 * LOG2E)
            qa_ref[h] = jnp.where(own, q, ones).astype(BF16)

    def reset_state():
        m_ref[...] = jnp.full(m_ref.shape, -1e30, F32)
        acc_ref[...] = jnp.zeros(acc_ref.shape, F32)

    hq = tq // 2
    spans = ((slice(0, hq), tk // 2), (slice(hq, tq), tk))

    def logits(h, kb, buf, diag=False):
        if not diag:
            s_ref[buf, h] = _dot_nt(qa_ref[h], ka_ref[h, kb * tk:(kb + 1) * tk, :])
            return
        for rows, nk in spans:
            s_ref[buf, h, rows, 0:nk] = _dot_nt(qa_ref[h, rows, :], ka_ref[h, kb * tk:kb * tk + nk, :])

    def softmax_pv(h, kb, s, rows):
        nk = s.shape[1]
        m_prev = m_ref[h, rows, :]
        m_new = jnp.maximum(m_prev, jnp.max(s, axis=-1, keepdims=True))
        p = jnp.concatenate([jnp.exp2(s[:, j * LANES:(j + 1) * LANES] - m_new)
                             for j in range(nk // LANES)], axis=1)
        pv = _dot(p.astype(BF16), va_ref[h, kb * tk:kb * tk + nk, :])
        acc_ref[h, rows, :] = jnp.exp2(m_prev - m_new) * acc_ref[h, rows, :] + pv
        m_ref[h, rows, :] = m_new

    def update(h, kb, buf, masked):
        if not masked:
            softmax_pv(h, kb, s_ref[buf, h], slice(0, tq))
            return
        for rows, nk in spans:
            s = s_ref[buf, h, rows, 0:nk]
            softmax_pv(h, kb, jnp.where(causal[rows, 0:nk], s, -1e30), rows)

    def write_out():
        for pair in range(FOX_HEADS // 2):
            a = acc_ref[2 * pair]
            b = acc_ref[2 * pair + 1]
            o = jnp.where(lane < FOX_HEAD_DIM, a / a[:, FOX_HEAD_DIM:FOX_HEAD_DIM + 1], b / b[:, 0:1])
            o_ref[:, pair * LANES:(pair + 1) * LANES] = o.astype(BF16)

    start = [0]
    for t in range(1, n_q):
        start.append(1 - (start[t - 1] + t - 1) % 2)
    for t in range(n_q):
        @pl.when(qi == t)
        def _(t=t):
            if t == 0:
                load_queries(0)
                reset_state()
                for h in range(FOX_HEADS):
                    logits(h, 0, start[0], diag=True)
            for j in range(t):
                for h in range(FOX_HEADS):
                    logits(h, j + 1, (start[t] + j + 1) % 2, diag=(j + 1 == t))
                    update(h, j, (start[t] + j) % 2, False)
            if t + 1 < n_q:
                load_queries(t + 1)
            for h in range(FOX_HEADS):
                update(h, t, (start[t] + t) % 2, True)
                if t + 1 < n_q:
                    logits(h, 0, start[t + 1])
            write_out()
            if t + 1 < n_q:
                reset_state()


def _fox(q, ka, va):
    bsz, seq, _ = q.shape
    tq = FOX_Q
    state = pltpu.VMEM((FOX_HEADS, tq, LANES), F32)
    full_spec = pl.BlockSpec((None, FOX_HEADS, seq, LANES), lambda b, i: (b, 0, 0, 0))
    return pl.pallas_call(
        _fox_kernel,
        grid=(bsz, seq // tq),
        in_specs=[pl.BlockSpec((None, seq, MIX_W), lambda b, i: (b, 0, 0)), full_spec, full_spec],
        out_specs=pl.BlockSpec((None, tq, MIX_W), lambda b, i: (b, i, 0)),
        out_shape=jax.ShapeDtypeStruct((bsz, seq, MIX_W), BF16),
        scratch_shapes=[pltpu.VMEM((FOX_HEADS, tq, LANES), BF16), state, state,
                        pltpu.VMEM((2, FOX_HEADS, tq, FOX_KV), F32)],
        compiler_params=_params(("arbitrary", "arbitrary")),
        name="fox_attn",
    )(q, ka, va)


def _s5_prep_kernel(lre_ref, lim_ref, ls_ref, bre_ref, bim_ref, are_ref, aim_ref, bbr_ref, bbi_ref):
    lr, li = lre_ref[...], lim_ref[...]
    dt = jnp.exp(ls_ref[...])
    mag = jnp.exp(lr * dt)
    ar = mag * jnp.cos(li * dt)
    ai = mag * jnp.sin(li * dt)
    den = lr * lr + li * li
    cr = ((ar - 1.0) * lr + ai * li) / den
    ci = (ai * lr - (ar - 1.0) * li) / den
    br, bi = bre_ref[...], bim_ref[...]
    bbr_ref[...] = cr * br - ci * bi
    bbi_ref[...] = cr * bi + ci * br
    are_ref[...] = jnp.broadcast_to(ar, are_ref.shape)
    aim_ref[...] = jnp.broadcast_to(ai, aim_ref.shape)


def _s5_prep(lam_re, lam_im, log_step, b_re, b_im):
    depth = lam_re.shape[0]
    row = lambda a: a.reshape(depth, 1, S5_NSTATE)
    ls = jnp.repeat(log_step, S5_STATE, axis=1).reshape(depth, 1, S5_NSTATE)
    to_hp = lambda b: b.transpose(0, 3, 1, 2).reshape(depth, S5_GROUP_CH, S5_NSTATE)
    spec = lambda r: pl.BlockSpec((None, r, S5_NSTATE), lambda l: (l, 0, 0))
    vec = jax.ShapeDtypeStruct((depth, 8, S5_NSTATE), F32)
    mat = jax.ShapeDtypeStruct((depth, S5_GROUP_CH, S5_NSTATE), F32)
    return pl.pallas_call(
        _s5_prep_kernel,
        grid=(depth,),
        in_specs=[spec(1), spec(1), spec(1), spec(S5_GROUP_CH), spec(S5_GROUP_CH)],
        out_specs=[spec(8), spec(8), spec(S5_GROUP_CH), spec(S5_GROUP_CH)],
        out_shape=[vec, vec, mat, mat],
        name="s5_prep",
    )(row(lam_re), row(lam_im), ls, to_hp(b_re), to_hp(b_im))


def _s5_kernel(u_ref, are_ref, aim_ref, wb_ref, wc_ref, d_ref, wglu_ref, o_ref, x_ref, st_ref, y_ref):
    @pl.when(pl.program_id(0) == 0)
    def _():
        st_ref[...] = jnp.zeros_like(st_ref)

    n = S5_NSTATE
    n_tiles = 2 * n // MXU_DIM
    total_rows = u_ref.shape[1]
    steps = total_rows // 8
    width = 1024

    def load_u(rows):
        return jnp.concatenate([u_ref[s, rows, :] for s in range(MIX_W // LANES)], axis=1)

    def project_in(rows):
        ub = load_u(rows).astype(BF16)
        for j in range(n_tiles):
            kh = ((j % (n_tiles // 2)) * MXU_DIM // S5_STATE * S5_GROUP_CH) // MXU_DIM
            x_ref[rows, j * MXU_DIM:(j + 1) * MXU_DIM] = _dot(ub[:, kh * MXU_DIM:(kh + 1) * MXU_DIM], wb_ref[j])

    def scan(t0, t1, state):
        for i, c0 in enumerate(range(0, n, width)):
            re_sl = slice(c0, c0 + width)
            im_sl = slice(n + c0, n + c0 + width)
            ar = are_ref[:, re_sl]
            ai = aim_ref[:, re_sl]
            xr, xi = state[i]
            for t in range(t0, t1):
                rows = slice(t * 8, t * 8 + 8)
                xr, xi = (ar * xr - ai * xi + x_ref[rows, re_sl], ar * xi + ai * xr + x_ref[rows, im_sl])
                x_ref[rows, re_sl] = xr
                x_ref[rows, im_sl] = xi
            state[i] = (xr, xi)

    def project_out(rows):
        halves = []
        per_half = n // MXU_DIM // 2
        for nh in range(MIX_W // MXU_DIM):
            acc_re = None
            acc_im = None
            for kk in range(per_half):
                kr = nh * per_half + kk
                ki = n // MXU_DIM + kr
                pr = _dot(x_ref[rows, kr * MXU_DIM:(kr + 1) * MXU_DIM].astype(BF16), wc_ref[kr])
                pi = _dot(x_ref[rows, ki * MXU_DIM:(ki + 1) * MXU_DIM].astype(BF16), wc_ref[ki])
                acc_re = pr if acc_re is None else acc_re + pr
                acc_im = pi if acc_im is None else acc_im + pi
            halves.append(acc_re - acc_im)
        y = jnp.concatenate(halves, axis=1) + d_ref[...] * load_u(rows)
        y = 0.5 * y * (1.0 + jnp.tanh(math.sqrt(2.0 / math.pi) * (y + 0.044715 * (y * y * y))))
        y = y * _sigmoid(_dot(y.astype(BF16), wglu_ref[...]))
        for s in range(MIX_W // LANES):
            y_ref[s, rows, :] = y[:, s * LANES:(s + 1) * LANES]

    phase_rows = total_rows // S5_PHASES
    phases = [slice(p * phase_rows, (p + 1) * phase_rows) for p in range(S5_PHASES)]
    for rows in phases:
        project_in(rows)
    state = [(st_ref[:, c0:c0 + width], st_ref[:, n + c0:n + c0 + width]) for c0 in range(0, n, width)]
    for p, rows in enumerate(phases):
        scan(p * steps // S5_PHASES, (p + 1) * steps // S5_PHASES, state)
        project_out(rows)
    for i, c0 in enumerate(range(0, n, width)):
        st_ref[:, c0:c0 + width] = state[i][0]
        st_ref[:, n + c0:n + c0 + width] = state[i][1]
    bsz = o_ref.shape[0]
    for b in range(bsz):
        o_ref[b] = jnp.concatenate([y_ref[s, pl.ds(b, steps, stride=bsz), :]
                                    for s in range(MIX_W // LANES)], axis=1).astype(BF16)


def _s5(u_tm, a_re, a_im, wb, wc, d, wglu, bsz, layer):
    rows = S5_STEPS * bsz
    total = u_tm.shape[1]
    return pl.pallas_call(
        _s5_kernel,
        grid=(total // rows,),
        in_specs=[
            pl.BlockSpec((MIX_W // LANES, rows, LANES), lambda i: (0, i, 0)),
            _layer_spec(a_re, layer), _layer_spec(a_im, layer),
            _layer_spec(wb, layer), _layer_spec(wc, layer),
            _layer_spec(d, layer), _layer_spec(wglu, layer),
        ],
        out_specs=pl.BlockSpec((bsz, S5_STEPS, MIX_W), lambda i: (0, i, 0)),
        out_shape=jax.ShapeDtypeStruct((bsz, total // bsz, MIX_W), BF16),
        scratch_shapes=[pltpu.VMEM((rows, 2 * S5_NSTATE), F32), pltpu.VMEM((8, 2 * S5_NSTATE), F32),
                        pltpu.VMEM((MIX_W // LANES, rows, LANES), F32)],
        compiler_params=_params(("arbitrary",)),
        name="s5",
    )(u_tm, a_re, a_im, wb, wc, d, wglu)


def _s5_weights(bbar_re, bbar_im, c_re, c_im):
    half = S5_NSTATE // MXU_DIM
    g_per_tile = MXU_DIM // S5_STATE
    g_per_slab = MXU_DIM // S5_GROUP_CH
    depth = bbar_re.shape[0]
    t = jnp.arange(half)[:, None, None]
    r = jnp.arange(MXU_DIM)[None, :, None]
    c = jnp.arange(MXU_DIM)[None, None, :]
    slab = (t * g_per_tile) // g_per_slab
    b_keep = slab * g_per_slab + r // S5_GROUP_CH == t * g_per_tile + c // S5_STATE
    c_keep = t * g_per_tile + r // S5_STATE == slab * g_per_slab + c // S5_GROUP_CH

    def b_tiles(bb):
        bb = bb.reshape(depth, S5_GROUP_CH, half, MXU_DIM).transpose(0, 2, 1, 3)
        return jnp.where(b_keep, jnp.tile(bb, (1, 1, g_per_slab, 1)), 0.0)

    def c_tiles(cc):
        cc = cc.transpose(0, 1, 3, 2).reshape(depth, half, MXU_DIM, S5_GROUP_CH)
        return jnp.where(c_keep, jnp.tile(cc, (1, 1, 1, g_per_slab)), 0.0)

    wb = jnp.concatenate([b_tiles(bbar_re), b_tiles(bbar_im)], axis=1).astype(BF16)
    wc = jnp.concatenate([c_tiles(c_re), c_tiles(c_im)], axis=1).astype(BF16)
    return wb, wc


def _merge_kernel(alpha, h_ref, ya_ref, yb_ref, yc_ref, wg_ref, wa_ref, wb_ref, wc_ref, wo_ref,
                  g_ref, b_ref, o_ref):
    for r0 in range(0, h_ref.shape[0], ROW_SUB):
        rows = slice(r0, r0 + ROW_SUB)
        h = h_ref[rows, :]
        hb = h.astype(BF16)
        merged = None
        for i, (y_ref, w_ref) in enumerate(((ya_ref, wa_ref), (yb_ref, wb_ref), (yc_ref, wc_ref))):
            gate = _sigmoid(_dot(hb, wg_ref[:, i * D_MODEL:(i + 1) * D_MODEL]))
            term = gate * _dot(y_ref[rows, :], w_ref[...])
            merged = term if merged is None else merged + term
        mix = _dot(merged.astype(BF16), wo_ref[...])
        o_ref[rows, :] = _layer_norm(alpha * h + mix, g_ref[...], b_ref[...])


def _merge(h, ya, yb, yc, wg, wa, wb, wc, wo, g, b, alpha, layer):
    bsz, seq, _ = h.shape
    tm = MERGE_ROWS
    row_spec = lambda w: pl.BlockSpec((None, tm, w), lambda bb, i: (bb, i, 0))
    return pl.pallas_call(
        functools.partial(_merge_kernel, alpha),
        grid=(bsz, seq // tm),
        in_specs=[
            row_spec(D_MODEL), row_spec(MIX_W), row_spec(MIX_W), row_spec(MIX_W),
            _layer_spec(wg, layer), _layer_spec(wa, layer), _layer_spec(wb, layer),
            _layer_spec(wc, layer), _layer_spec(wo, layer), _layer_spec(g, layer), _layer_spec(b, layer),
        ],
        out_specs=row_spec(D_MODEL),
        out_shape=jax.ShapeDtypeStruct(h.shape, F32),
        compiler_params=_params(("arbitrary", "arbitrary")),
        name="merge_ln",
    )(h, ya, yb, yc, wg, wa, wb, wc, wo, g, b)


def _ffn_kernel(alpha, h_ref, wg_ref, wu_ref, wd_ref, g_ref, b_ref, o_ref):
    for r0 in range(0, h_ref.shape[0], ROW_SUB):
        rows = slice(r0, r0 + ROW_SUB)
        h = h_ref[rows, :]
        hb = h.astype(BF16)
        acc = None
        for c0 in range(0, FFN_HIDDEN, FFN_CHUNK):
            sl = slice(c0, min(c0 + FFN_CHUNK, FFN_HIDDEN))
            a = _dot(hb, wg_ref[:, sl])
            hid = (a * _sigmoid(a)) * _dot(hb, wu_ref[:, sl])
            part = _dot(hid.astype(BF16), wd_ref[sl, :])
            acc = part if acc is None else acc + part
        o_ref[rows, :] = _layer_norm(alpha * h + acc, g_ref[...], b_ref[...])


def _ffn(h, wg, wu, wd, g, b, alpha, layer):
    bsz, seq, _ = h.shape
    tm = FFN_ROWS
    row_spec = pl.BlockSpec((None, tm, D_MODEL), lambda bb, i: (bb, i, 0))
    return pl.pallas_call(
        functools.partial(_ffn_kernel, alpha),
        grid=(bsz, seq // tm),
        in_specs=[row_spec, _layer_spec(wg, layer), _layer_spec(wu, layer), _layer_spec(wd, layer),
                  _layer_spec(g, layer), _layer_spec(b, layer)],
        out_specs=row_spec,
        out_shape=jax.ShapeDtypeStruct(h.shape, F32),
        compiler_params=_params(("arbitrary", "arbitrary")),
        name="ffn_ln",
    )(h, wg, wu, wd, g, b)


def kernel(x, w_in, hg_lower_bounds, hg_norm_w, fox_b_f, s5_lambda_re, s5_lambda_im, s5_log_step,
           s5_b_re, s5_b_im, s5_c_re, s5_c_im, s5_d, s5_w_glu, w_br_a, w_br_b, w_br_c, w_out,
           ln1_g, ln1_b, w_ffn_gate, w_ffn_up, w_ffn_down, ln2_g, ln2_b):
    depth = w_in.shape[0]
    bsz, seq, _ = x.shape
    alpha = (2 * depth) ** 0.25
    o_s5 = FOX_COL0 + 3 * MIX_W + FOX_HEADS
    o_gate = o_s5 + MIX_W

    bf = lambda a: a.astype(BF16)
    vec = lambda a: a.reshape(depth, 1, -1)
    w_main = bf(w_in)
    w_s5, w_gate = w_main[:, :, o_s5:o_gate], w_main[:, :, o_gate:]
    w_glu_b = bf(s5_w_glu)
    w_a, w_b, w_c, w_o = bf(w_br_a), bf(w_br_b), bf(w_br_c), bf(w_out)
    w_fg, w_fu, w_fd = bf(w_ffn_gate), bf(w_ffn_up), bf(w_ffn_down)
    fox_bias = jnp.pad(fox_b_f, ((0, 0), (0, LANES - FOX_HEADS))).reshape(depth, 1, LANES)
    a_re, a_im, bbar_re, bbar_im = _s5_prep(s5_lambda_re, s5_lambda_im, s5_log_step, s5_b_re, s5_b_im)
    wb, wc = _s5_weights(bbar_re, bbar_im, s5_c_re, s5_c_im)
    norm_w, s5_skip = vec(hg_norm_w), vec(s5_d)
    g1, b1, g2, b2 = vec(ln1_g), vec(ln1_b), vec(ln2_g), vec(ln2_b)

    h = x
    for l in range(depth):
        q, ka, va, u_tm = _in_proj(h, w_main, fox_bias, w_s5, l)
        ya = _hgrn2(h, w_main, hg_lower_bounds, norm_w, l)
        yb = _fox(q, ka, va)
        yc = _s5(u_tm, a_re, a_im, wb, wc, s5_skip, w_glu_b, bsz, l)
        h = _merge(h, ya, yb, yc, w_gate, w_a, w_b, w_c, w_o, g1, b1, alpha, l)
        h = _ffn(h, w_fg, w_fu, w_fd, g2, b2, alpha, l)
    return h
```

```python
import functools
import math

import jax
import jax.numpy as jnp
from jax import lax
from jax.experimental import pallas as pl
from jax.experimental.pallas import tpu as pltpu

F32 = jnp.float32
BF16 = jnp.bfloat16

D_MODEL = 1024
MIX_W = 512
HG_HEADS = 4
HG_KEY = 128
HG_CHUNK = 64
HG_REF_ROW = HG_CHUNK // 2 - 1
FOX_HEADS = 8
FOX_COL0 = 4 * MIX_W
FOX_HEAD_DIM = 64
S5_GROUPS = 32
S5_STATE = 64
S5_GROUP_CH = 16
S5_NSTATE = S5_GROUPS * S5_STATE
FFN_HIDDEN = 2816
LN_EPS = 1e-5
RMS_EPS = 1e-6
EXP2_CLAMP = 115.0
LOG2E = math.log2(math.e)

LANES = 128
MXU_DIM = 256
VMEM_LIMIT = 56 * 1024 * 1024

PROJ_ROWS = 1024
PROJ_GROUP = 2
HG_ROWS = 256
FOX_Q = 512
FOX_KV = 512
S5_PHASES = 4
S5_STEPS = 128
MERGE_ROWS = 1024
FFN_ROWS = 1024
ROW_SUB = 256
FFN_CHUNK = 6 * MXU_DIM


def _dot(a, b):
    return jnp.dot(a, b, preferred_element_type=F32)


def _dot_nt(a, b):
    return lax.dot_general(a, b, (((1,), (1,)), ((), ())), preferred_element_type=F32)


def _dot_tn(a, b):
    return lax.dot_general(a, b, (((0,), (0,)), ((), ())), preferred_element_type=F32)


def _sigmoid(x):
    return 1.0 / (1.0 + jnp.exp2(x * -LOG2E))


def _layer_norm(z, g, b):
    mu = jnp.mean(z, axis=-1, keepdims=True)
    zc = z - mu
    var = jnp.mean(zc * zc, axis=-1, keepdims=True)
    return zc * lax.rsqrt(var + LN_EPS) * g + b


def _cumsum(x, axis):
    n = x.shape[axis]
    idx = lax.broadcasted_iota(jnp.int32, x.shape, axis)
    d = 1
    while d < n:
        x = x + jnp.where(idx >= d, pltpu.roll(x, d, axis=axis), 0.0)
        d *= 2
    return x


def _layer_spec(w, layer):
    zeros = (0,) * (w.ndim - 1)
    return pl.BlockSpec((None,) + w.shape[1:], lambda *_: (layer,) + zeros, pipeline_mode=pl.Buffered(1))


def _params(sem):
    return pltpu.CompilerParams(dimension_semantics=sem, vmem_limit_bytes=VMEM_LIMIT)


def _fox_lanes(h, lane):
    if h % 2 == 0:
        return lane < FOX_HEAD_DIM, FOX_HEAD_DIM
    return lane >= FOX_HEAD_DIM, 0


def _proj_kernel(x_ref, wq_ref, wk_ref, wv_ref, wff_ref, bf_ref, ws5_ref, q_ref, ka_ref, va_ref, s5_ref,
                 cum_ref):
    bsz, tt, _ = x_ref.shape

    @pl.when(pl.program_id(0) == 0)
    def _():
        cum_ref[...] = jnp.zeros_like(cum_ref)

    lane = lax.broadcasted_iota(jnp.int32, (1, LANES), 1)
    for b0 in range(0, bsz, PROJ_GROUP):
        nb = PROJ_GROUP
        rows = nb * tt
        xb = x_ref[b0:b0 + nb].reshape(rows, D_MODEL).astype(BF16)
        k = _dot(xb, wk_ref[...])
        v = _dot(xb, wv_ref[...])
        z = _dot(xb, wff_ref[...]) + bf_ref[...]
        q_ref[b0:b0 + nb] = _dot(xb, wq_ref[...]).astype(BF16).reshape(nb, tt, MIX_W)
        u = _dot(xb, ws5_ref[...])

        logsig = jnp.minimum(z, 0.0) - jnp.log(1.0 + jnp.exp(-jnp.abs(z)))
        t_idx = lax.broadcasted_iota(jnp.int32, (rows, LANES), 0) % tt
        cum = logsig
        d = 1
        while d < tt:
            cum = cum + jnp.where(t_idx >= d, pltpu.roll(cum, d, axis=0), 0.0)
            d *= 2
        carry = cum_ref[b0:b0 + nb, :]
        cum = jnp.concatenate([cum[b * tt:(b + 1) * tt] + carry[b:b + 1] for b in range(nb)], axis=0)
        cum_ref[b0:b0 + nb, :] = jnp.concatenate([cum[(b + 1) * tt - 1:(b + 1) * tt] for b in range(nb)], axis=0)

        bias = cum * (-LOG2E)
        for h in range(FOX_HEADS):
            sl = slice((h // 2) * LANES, (h // 2 + 1) * LANES)
            own, aux = _fox_lanes(h, lane)
            bh = jnp.broadcast_to(bias[:, h:h + 1], (rows, LANES))
            hi = bh.astype(BF16).astype(F32)
            mid = (bh - hi).astype(BF16).astype(F32)
            lo = bh - hi - mid
            extra = jnp.where(lane == aux, hi, jnp.where(lane == aux + 1, mid,
                                                         jnp.where(lane == aux + 2, lo, 0.0)))
            ka = jnp.where(own, k[:, sl], extra).astype(BF16)
            va = jnp.where(own, v[:, sl], jnp.where(lane == aux, 1.0, 0.0)).astype(BF16)
            for b in range(nb):
                ka_ref[b0 + b, h] = ka[b * tt:(b + 1) * tt]
                va_ref[b0 + b, h] = va[b * tt:(b + 1) * tt]

        for b in range(nb):
            for s in range(MIX_W // LANES):
                s5_ref[s, pl.ds(b0 + b, tt, stride=bsz), :] = u[b * tt:(b + 1) * tt, s * LANES:(s + 1) * LANES]


def _in_proj(h, w_all, bf, ws5, layer):
    bsz, seq, _ = h.shape
    tt = PROJ_ROWS // bsz
    fox_spec = lambda j: pl.BlockSpec((None, D_MODEL, MIX_W), lambda i: (layer, 0, FOX_COL0 // MIX_W + j))
    head_spec = pl.BlockSpec((bsz, FOX_HEADS, tt, LANES), lambda i: (0, 0, i, 0))
    head_shape = jax.ShapeDtypeStruct((bsz, FOX_HEADS, seq, LANES), BF16)
    return pl.pallas_call(
        _proj_kernel,
        grid=(seq // tt,),
        in_specs=[
            pl.BlockSpec((bsz, tt, D_MODEL), lambda i: (0, i, 0)),
            fox_spec(0), fox_spec(1), fox_spec(2),
            pl.BlockSpec((None, D_MODEL, LANES), lambda i: (layer, 0, (FOX_COL0 + 3 * MIX_W) // LANES)),
            pl.BlockSpec((None, 1, LANES), lambda i: (layer, 0, 0)),
            _layer_spec(ws5, layer),
        ],
        out_specs=[
            pl.BlockSpec((bsz, tt, MIX_W), lambda i: (0, i, 0)),
            head_spec, head_spec,
            pl.BlockSpec((MIX_W // LANES, bsz * tt, LANES), lambda i: (0, i, 0)),
        ],
        out_shape=[
            jax.ShapeDtypeStruct((bsz, seq, MIX_W), BF16),
            head_shape, head_shape,
            jax.ShapeDtypeStruct((MIX_W // LANES, seq * bsz, LANES), F32),
        ],
        scratch_shapes=[pltpu.VMEM((bsz, LANES), F32)],
        compiler_params=_params(("arbitrary",)),
        name="in_proj",
    )(h, w_all, w_all, w_all, w_all, bf, ws5)


def _hgrn2_kernel(layer, h_ref, hn_ref, w_ref, lbp_ref, nw_ref, o_ref, st_ref, hg0_ref, hg1_ref):
    i = pl.program_id(1)

    @pl.when(i == 0)
    def _():
        st_ref[...] = jnp.zeros_like(st_ref)
        hg0_ref[...] = _dot(h_ref[...].astype(BF16), w_ref[...])

    lbp = lbp_ref[...]
    e = jnp.exp(lbp - jnp.max(lbp, axis=0, keepdims=True))
    sm = e / jnp.sum(e, axis=0, keepdims=True)
    cum = sm[0:1]
    for j in range(1, layer + 1):
        cum = cum + sm[j:j + 1]
    lb = cum - sm[0:1]

    nw = nw_ref[...]
    c = HG_CHUNK
    n_chunks = h_ref.shape[0] // c
    tri = (lax.broadcasted_iota(jnp.int32, (c, c), 0) >= lax.broadcasted_iota(jnp.int32, (c, c), 1))

    def gates(ci, hg_ref):
        rows = slice(ci * c, (ci + 1) * c)
        q = hg_ref[rows, 0:MIX_W]
        fz = hg_ref[rows, MIX_W:2 * MIX_W]
        f = lb + (1.0 - lb) * _sigmoid(fz)
        k = 1.0 - f
        g = _cumsum(jnp.log2(f), 0)
        g_ref = g[HG_REF_ROW:HG_REF_ROW + 1]
        g_last = g[c - 1:c]
        d_ref = g - g_ref
        q_rel = (q * jnp.exp2(jnp.minimum(d_ref, EXP2_CLAMP))).astype(BF16)
        k_rel = (k * jnp.exp2(jnp.minimum(-d_ref, EXP2_CLAMP))).astype(BF16)
        q_in = (q * jnp.exp2(g)).astype(BF16)
        k_end = (k * jnp.exp2(g_last - g)).astype(BF16)
        s_decay = jnp.exp2(g_last)
        return q_rel, k_rel, q_in, k_end, s_decay

    def mix(ci, hg_ref, operands):
        rows = slice(ci * c, (ci + 1) * c)
        q_rel, k_rel, q_in, k_end, s_decay = operands
        v = hg_ref[rows, 2 * MIX_W:3 * MIX_W]
        gate = hg_ref[rows, 3 * MIX_W:4 * MIX_W]
        for h in range(HG_HEADS):
            sl = slice(h * HG_KEY, (h + 1) * HG_KEY)
            st = st_ref[h]
            v_t = v[:, sl].T.astype(BF16)
            scores = jnp.where(tri, _dot_nt(q_rel[:, sl], k_rel[:, sl]), 0.0)
            lhs = jnp.concatenate([q_in[:, sl], scores.astype(BF16)], axis=1)
            rhs_t = jnp.concatenate([st.astype(BF16), v_t], axis=1)
            o = _dot_nt(lhs, rhs_t)
            st_ref[h] = st * s_decay[:, sl] + _dot(v_t, k_end[:, sl])
            o = o * lax.rsqrt(jnp.mean(o * o, axis=-1, keepdims=True) + RMS_EPS) * nw
            gh = gate[:, sl]
            o_ref[rows, sl] = (o * (gh * _sigmoid(gh))).astype(BF16)

    def tile(cur_ref, nxt_ref):
        hn = hn_ref[...].astype(BF16)
        cols = w_ref.shape[1] // n_chunks
        operands = gates(0, cur_ref)
        for ci in range(n_chunks):
            ahead = gates(ci + 1, cur_ref) if ci + 1 < n_chunks else None
            mix(ci, cur_ref, operands)
            nxt_ref[:, ci * cols:(ci + 1) * cols] = _dot(hn, w_ref[:, ci * cols:(ci + 1) * cols])
            operands = ahead

    @pl.when(i % 2 == 0)
    def _():
        tile(hg0_ref, hg1_ref)

    @pl.when(i % 2 == 1)
    def _():
        tile(hg1_ref, hg0_ref)


def _hgrn2(h, w_all, lbp, nw, layer):
    bsz, seq, _ = h.shape
    tr = HG_ROWS
    n_tiles = seq // tr
    hg = pltpu.VMEM((tr, 4 * MIX_W), F32)
    return pl.pallas_call(
        functools.partial(_hgrn2_kernel, layer),
        grid=(bsz, n_tiles),
        in_specs=[
            pl.BlockSpec((None, tr, D_MODEL), lambda b, i: (b, i, 0)),
            pl.BlockSpec((None, tr, D_MODEL), lambda b, i: (b, jnp.minimum(i + 1, n_tiles - 1), 0)),
            pl.BlockSpec((None, D_MODEL, 4 * MIX_W), lambda b, i: (layer, 0, 0)),
            pl.BlockSpec(lbp.shape, lambda b, i: (0, 0)),
            _layer_spec(nw, layer),
        ],
        out_specs=pl.BlockSpec((None, tr, MIX_W), lambda b, i: (b, i, 0)),
        out_shape=jax.ShapeDtypeStruct((bsz, seq, MIX_W), BF16),
        scratch_shapes=[pltpu.VMEM((HG_HEADS, HG_KEY, HG_KEY), F32), hg, hg],
        compiler_params=_params(("arbitrary", "arbitrary")),
        name="hgrn2",
    )(h, h, w_all, lbp, nw)


def _fox_kernel(q_ref, ka_ref, va_ref, o_ref, qa_ref, m_ref, acc_ref, s_ref):
    qi = pl.program_id(1)
    tq, tk = FOX_Q, FOX_KV
    n_q = q_ref.shape[0] // tq
    lane = lax.broadcasted_iota(jnp.int32, (1, LANES), 1)
    causal = (lax.broadcasted_iota(jnp.int32, (tq, tk), 0) >= lax.broadcasted_iota(jnp.int32, (tq, tk), 1))

    def load_queries(t):
        for h in range(FOX_HEADS):
            sl = slice((h // 2) * LANES, (h // 2 + 1) * LANES)
            own, aux = _fox_lanes(h, lane)
            ones = jnp.where((lane >= aux) & (lane < aux + 3), 1.0, 0.0)
            q = q_ref[t * tq:(t + 1) * tq, sl].astype(F32) * (FOX_HEAD_DIM ** -0.5 * LOG2E)
            qa_ref[h] = jnp.where(own, q, ones).astype(BF16)

    def reset_state():
        m_ref[...] = jnp.full(m_ref.shape, -1e30, F32)
        acc_ref[...] = jnp.zeros(acc_ref.shape, F32)

    hq = tq // 2
    spans = ((slice(0, hq), tk // 2), (slice(hq, tq), tk))

    def logits(h, kb, buf, diag=False):
        if not diag:
            s_ref[buf, h] = _dot_nt(qa_ref[h], ka_ref[h, kb * tk:(kb + 1) * tk, :])
            return
        for rows, nk in spans:
            s_ref[buf, h, rows, 0:nk] = _dot_nt(qa_ref[h, rows, :], ka_ref[h, kb * tk:kb * tk + nk, :])

    def softmax_pv(h, kb, s, rows):
        nk = s.shape[1]
        m_prev = m_ref[h, rows, :]
        m_new = jnp.maximum(m_prev, jnp.max(s, axis=-1, keepdims=True))
        p = jnp.concatenate([jnp.exp2(s[:, j * LANES:(j + 1) * LANES] - m_new)
                             for j in range(nk // LANES)], axis=1)
        pv = _dot(p.astype(BF16), va_ref[h, kb * tk:kb * tk + nk, :])
        acc_ref[h, rows, :] = jnp.exp2(m_prev - m_new) * acc_ref[h, rows, :] + pv
        m_ref[h, rows, :] = m_new

    def update(h, kb, buf, masked):
        if not masked:
            softmax_pv(h, kb, s_ref[buf, h], slice(0, tq))
            return
        for rows, nk in spans:
            s = s_ref[buf, h, rows, 0:nk]
            softmax_pv(h, kb, jnp.where(causal[rows, 0:nk], s, -1e30), rows)

    def write_out():
        for pair in range(FOX_HEADS // 2):
            a = acc_ref[2 * pair]
            b = acc_ref[2 * pair + 1]
            o = jnp.where(lane < FOX_HEAD_DIM, a / a[:, FOX_HEAD_DIM:FOX_HEAD_DIM + 1], b / b[:, 0:1])
            o_ref[:, pair * LANES:(pair + 1) * LANES] = o.astype(BF16)

    start = [0]
    for t in range(1, n_q):
        start.append(1 - (start[t - 1] + t - 1) % 2)
    for t in range(n_q):
        @pl.when(qi == t)
        def _(t=t):
            if t == 0:
                load_queries(0)
                reset_state()
                for h in range(FOX_HEADS):
                    logits(h, 0, start[0], diag=True)
            for j in range(t):
                for h in range(FOX_HEADS):
                    logits(h, j + 1, (start[t] + j + 1) % 2, diag=(j + 1 == t))
                    update(h, j, (start[t] + j) % 2, False)
            if t + 1 < n_q:
                load_queries(t + 1)
            for h in range(FOX_HEADS):
                update(h, t, (start[t] + t) % 2, True)
                if t + 1 < n_q:
                    logits(h, 0, start[t + 1])
            write_out()
            if t + 1 < n_q:
                reset_state()


def _fox(q, ka, va):
    bsz, seq, _ = q.shape
    tq = FOX_Q
    state = pltpu.VMEM((FOX_HEADS, tq, LANES), F32)
    full_spec = pl.BlockSpec((None, FOX_HEADS, seq, LANES), lambda b, i: (b, 0, 0, 0))
    return pl.pallas_call(
        _fox_kernel,
        grid=(bsz, seq // tq),
        in_specs=[pl.BlockSpec((None, seq, MIX_W), lambda b, i: (b, 0, 0)), full_spec, full_spec],
        out_specs=pl.BlockSpec((None, tq, MIX_W), lambda b, i: (b, i, 0)),
        out_shape=jax.ShapeDtypeStruct((bsz, seq, MIX_W), BF16),
        scratch_shapes=[pltpu.VMEM((FOX_HEADS, tq, LANES), BF16), state, state,
                        pltpu.VMEM((2, FOX_HEADS, tq, FOX_KV), F32)],
        compiler_params=_params(("arbitrary", "arbitrary")),
        name="fox_attn",
    )(q, ka, va)


def _s5_prep_kernel(lre_ref, lim_ref, ls_ref, bre_ref, bim_ref, are_ref, aim_ref, bbr_ref, bbi_ref):
    lr, li = lre_ref[...], lim_ref[...]
    dt = jnp.exp(ls_ref[...])
    mag = jnp.exp(lr * dt)
    ar = mag * jnp.cos(li * dt)
    ai = mag * jnp.sin(li * dt)
    den = lr * lr + li * li
    cr = ((ar - 1.0) * lr + ai * li) / den
    ci = (ai * lr - (ar - 1.0) * li) / den
    br, bi = bre_ref[...], bim_ref[...]
    bbr_ref[...] = cr * br - ci * bi
    bbi_ref[...] = cr * bi + ci * br
    are_ref[...] = jnp.broadcast_to(ar, are_ref.shape)
    aim_ref[...] = jnp.broadcast_to(ai, aim_ref.shape)


def _s5_prep(lam_re, lam_im, log_step, b_re, b_im):
    depth = lam_re.shape[0]
    row = lambda a: a.reshape(depth, 1, S5_NSTATE)
    ls = jnp.repeat(log_step, S5_STATE, axis=1).reshape(depth, 1, S5_NSTATE)
    to_hp = lambda b: b.transpose(0, 3, 1, 2).reshape(depth, S5_GROUP_CH, S5_NSTATE)
    spec = lambda r: pl.BlockSpec((None, r, S5_NSTATE), lambda l: (l, 0, 0))
    vec = jax.ShapeDtypeStruct((depth, 8, S5_NSTATE), F32)
    mat = jax.ShapeDtypeStruct((depth, S5_GROUP_CH, S5_NSTATE), F32)
    return pl.pallas_call(
        _s5_prep_kernel,
        grid=(depth,),
        in_specs=[spec(1), spec(1), spec(1), spec(S5_GROUP_CH), spec(S5_GROUP_CH)],
        out_specs=[spec(8), spec(8), spec(S5_GROUP_CH), spec(S5_GROUP_CH)],
        out_shape=[vec, vec, mat, mat],
        name="s5_prep",
    )(row(lam_re), row(lam_im), ls, to_hp(b_re), to_hp(b_im))


def _s5_kernel(u_ref, are_ref, aim_ref, wb_ref, wc_ref, d_ref, wglu_ref, o_ref, x_ref, st_ref, y_ref):
    @pl.when(pl.program_id(0) == 0)
    def _():
        st_ref[...] = jnp.zeros_like(st_ref)

    n = S5_NSTATE
    n_tiles = 2 * n // MXU_DIM
    total_rows = u_ref.shape[1]
    steps = total_rows // 8
    width = 1024

    def load_u(rows):
        return jnp.concatenate([u_ref[s, rows, :] for s in range(MIX_W // LANES)], axis=1)

    def project_in(rows):
        ub = load_u(rows).astype(BF16)
        for j in range(n_tiles):
            kh = ((j % (n_tiles // 2)) * MXU_DIM // S5_STATE * S5_GROUP_CH) // MXU_DIM
            x_ref[rows, j * MXU_DIM:(j + 1) * MXU_DIM] = _dot(ub[:, kh * MXU_DIM:(kh + 1) * MXU_DIM], wb_ref[j])

    def scan(t0, t1, state):
        for i, c0 in enumerate(range(0, n, width)):
            re_sl = slice(c0, c0 + width)
            im_sl = slice(n + c0, n + c0 + width)
            ar = are_ref[:, re_sl]
            ai = aim_ref[:, re_sl]
            xr, xi = state[i]
            for t in range(t0, t1):
                rows = slice(t * 8, t * 8 + 8)
                xr, xi = (ar * xr - ai * xi + x_ref[rows, re_sl], ar * xi + ai * xr + x_ref[rows, im_sl])
                x_ref[rows, re_sl] = xr
                x_ref[rows, im_sl] = xi
            state[i] = (xr, xi)

    def project_out(rows):
        halves = []
        per_half = n // MXU_DIM // 2
        for nh in range(MIX_W // MXU_DIM):
            acc_re = None
            acc_im = None
            for kk in range(per_half):
                kr = nh * per_half + kk
                ki = n // MXU_DIM + kr
                pr = _dot(x_ref[rows, kr * MXU_DIM:(kr + 1) * MXU_DIM].astype(BF16), wc_ref[kr])
                pi = _dot(x_ref[rows, ki * MXU_DIM:(ki + 1) * MXU_DIM].astype(BF16), wc_ref[ki])
                acc_re = pr if acc_re is None else acc_re + pr
                acc_im = pi if acc_im is None else acc_im + pi
            halves.append(acc_re - acc_im)
        y = jnp.concatenate(halves, axis=1) + d_ref[...] * load_u(rows)
        y = 0.5 * y * (1.0 + jnp.tanh(math.sqrt(2.0 / math.pi) * (y + 0.044715 * (y * y * y))))
        y = y * _sigmoid(_dot(y.astype(BF16), wglu_ref[...]))
        for s in range(MIX_W // LANES):
            y_ref[s, rows, :] = y[:, s * LANES:(s + 1) * LANES]

    phase_rows = total_rows // S5_PHASES
    phases = [slice(p * phase_rows, (p + 1) * phase_rows) for p in range(S5_PHASES)]
    for rows in phases:
        project_in(rows)
    state = [(st_ref[:, c0:c0 + width], st_ref[:, n + c0:n + c0 + width]) for c0 in range(0, n, width)]
    for p, rows in enumerate(phases):
        scan(p * steps // S5_PHASES, (p + 1) * steps // S5_PHASES, state)
        project_out(rows)
    for i, c0 in enumerate(range(0, n, width)):
        st_ref[:, c0:c0 + width] = state[i][0]
        st_ref[:, n + c0:n + c0 + width] = state[i][1]
    bsz = o_ref.shape[0]
    for b in range(bsz):
        o_ref[b] = jnp.concatenate([y_ref[s, pl.ds(b, steps, stride=bsz), :]
                                    for s in range(MIX_W // LANES)], axis=1).astype(BF16)


def _s5(u_tm, a_re, a_im, wb, wc, d, wglu, bsz, layer):
    rows = S5_STEPS * bsz
    total = u_tm.shape[1]
    return pl.pallas_call(
        _s5_kernel,
        grid=(total // rows,),
        in_specs=[
            pl.BlockSpec((MIX_W // LANES, rows, LANES), lambda i: (0, i, 0)),
            _layer_spec(a_re, layer), _layer_spec(a_im, layer),
            _layer_spec(wb, layer), _layer_spec(wc, layer),
            _layer_spec(d, layer), _layer_spec(wglu, layer),
        ],
        out_specs=pl.BlockSpec((bsz, S5_STEPS, MIX_W), lambda i: (0, i, 0)),
        out_shape=jax.ShapeDtypeStruct((bsz, total // bsz, MIX_W), BF16),
        scratch_shapes=[pltpu.VMEM((rows, 2 * S5_NSTATE), F32), pltpu.VMEM((8, 2 * S5_NSTATE), F32),
                        pltpu.VMEM((MIX_W // LANES, rows, LANES), F32)],
        compiler_params=_params(("arbitrary",)),
        name="s5",
    )(u_tm, a_re, a_im, wb, wc, d, wglu)


def _s5_weights(bbar_re, bbar_im, c_re, c_im):
    half = S5_NSTATE // MXU_DIM
    g_per_tile = MXU_DIM // S5_STATE
    g_per_slab = MXU_DIM // S5_GROUP_CH
    depth = bbar_re.shape[0]
    t = jnp.arange(half)[:, None, None]
    r = jnp.arange(MXU_DIM)[None, :, None]
    c = jnp.arange(MXU_DIM)[None, None, :]
    slab = (t * g_per_tile) // g_per_slab
    b_keep = slab * g_per_slab + r // S5_GROUP_CH == t * g_per_tile + c // S5_STATE
    c_keep = t * g_per_tile + r // S5_STATE == slab * g_per_slab + c // S5_GROUP_CH

    def b_tiles(bb):
        bb = bb.reshape(depth, S5_GROUP_CH, half, MXU_DIM).transpose(0, 2, 1, 3)
        return jnp.where(b_keep, jnp.tile(bb, (1, 1, g_per_slab, 1)), 0.0)

    def c_tiles(cc):
        cc = cc.transpose(0, 1, 3, 2).reshape(depth, half, MXU_DIM, S5_GROUP_CH)
        return jnp.where(c_keep, jnp.tile(cc, (1, 1, 1, g_per_slab)), 0.0)

    wb = jnp.concatenate([b_tiles(bbar_re), b_tiles(bbar_im)], axis=1).astype(BF16)
    wc = jnp.concatenate([c_tiles(c_re), c_tiles(c_im)], axis=1).astype(BF16)
    return wb, wc


def _merge_kernel(alpha, h_ref, ya_ref, yb_ref, yc_ref, wg_ref, wa_ref, wb_ref, wc_ref, wo_ref,
                  g_ref, b_ref, o_ref):
    for r0 in range(0, h_ref.shape[0], ROW_SUB):
        rows = slice(r0, r0 + ROW_SUB)
        h = h_ref[rows, :]
        hb = h.astype(BF16)
        merged = None
        for i, (y_ref, w_ref) in enumerate(((ya_ref, wa_ref), (yb_ref, wb_ref), (yc_ref, wc_ref))):
            gate = _sigmoid(_dot(hb, wg_ref[:, i * D_MODEL:(i + 1) * D_MODEL]))
            term = gate * _dot(y_ref[rows, :], w_ref[...])
            merged = term if merged is None else merged + term
        mix = _dot(merged.astype(BF16), wo_ref[...])
        o_ref[rows, :] = _layer_norm(alpha * h + mix, g_ref[...], b_ref[...])


def _merge(h, ya, yb, yc, wg, wa, wb, wc, wo, g, b, alpha, layer):
    bsz, seq, _ = h.shape
    tm = MERGE_ROWS
    row_spec = lambda w: pl.BlockSpec((None, tm, w), lambda bb, i: (bb, i, 0))
    return pl.pallas_call(
        functools.partial(_merge_kernel, alpha),
        grid=(bsz, seq // tm),
        in_specs=[
            row_spec(D_MODEL), row_spec(MIX_W), row_spec(MIX_W), row_spec(MIX_W),
            _layer_spec(wg, layer), _layer_spec(wa, layer), _layer_spec(wb, layer),
            _layer_spec(wc, layer), _layer_spec(wo, layer), _layer_spec(g, layer), _layer_spec(b, layer),
        ],
        out_specs=row_spec(D_MODEL),
        out_shape=jax.ShapeDtypeStruct(h.shape, F32),
        compiler_params=_params(("arbitrary", "arbitrary")),
        name="merge_ln",
    )(h, ya, yb, yc, wg, wa, wb, wc, wo, g, b)


def _ffn_kernel(alpha, h_ref, wg_ref, wu_ref, wd_ref, g_ref, b_ref, o_ref):
    for r0 in range(0, h_ref.shape[0], ROW_SUB):
        rows = slice(r0, r0 + ROW_SUB)
        h = h_ref[rows, :]
        hb = h.astype(BF16)
        acc = None
        for c0 in range(0, FFN_HIDDEN, FFN_CHUNK):
            sl = slice(c0, min(c0 + FFN_CHUNK, FFN_HIDDEN))
            a = _dot(hb, wg_ref[:, sl])
            hid = (a * _sigmoid(a)) * _dot(hb, wu_ref[:, sl])
            part = _dot(hid.astype(BF16), wd_ref[sl, :])
            acc = part if acc is None else acc + part
        o_ref[rows, :] = _layer_norm(alpha * h + acc, g_ref[...], b_ref[...])


def _ffn(h, wg, wu, wd, g, b, alpha, layer):
    bsz, seq, _ = h.shape
    tm = FFN_ROWS
    row_spec = pl.BlockSpec((None, tm, D_MODEL), lambda bb, i: (bb, i, 0))
    return pl.pallas_call(
        functools.partial(_ffn_kernel, alpha),
        grid=(bsz, seq // tm),
        in_specs=[row_spec, _layer_spec(wg, layer), _layer_spec(wu, layer), _layer_spec(wd, layer),
                  _layer_spec(g, layer), _layer_spec(b, layer)],
        out_specs=row_spec,
        out_shape=jax.ShapeDtypeStruct(h.shape, F32),
        compiler_params=_params(("arbitrary", "arbitrary")),
        name="ffn_ln",
    )(h, wg, wu, wd, g, b)


def kernel(x, w_in, hg_lower_bounds, hg_norm_w, fox_b_f, s5_lambda_re, s5_lambda_im, s5_log_step,
           s5_b_re, s5_b_im, s5_c_re, s5_c_im, s5_d, s5_w_glu, w_br_a, w_br_b, w_br_c, w_out,
           ln1_g, ln1_b, w_ffn_gate, w_ffn_up, w_ffn_down, ln2_g, ln2_b):
    depth = w_in.shape[0]
    bsz, seq, _ = x.shape
    alpha = (2 * depth) ** 0.25
    o_s5 = FOX_COL0 + 3 * MIX_W + FOX_HEADS
    o_gate = o_s5 + MIX_W

    bf = lambda a: a.astype(BF16)
    vec = lambda a: a.reshape(depth, 1, -1)
    w_main = bf(w_in)
    w_s5, w_gate = w_main[:, :, o_s5:o_gate], w_main[:, :, o_gate:]
    w_glu_b = bf(s5_w_glu)
    w_a, w_b, w_c, w_o = bf(w_br_a), bf(w_br_b), bf(w_br_c), bf(w_out)
    w_fg, w_fu, w_fd = bf(w_ffn_gate), bf(w_ffn_up), bf(w_ffn_down)
    fox_bias = jnp.pad(fox_b_f, ((0, 0), (0, LANES - FOX_HEADS))).reshape(depth, 1, LANES)
    a_re, a_im, bbar_re, bbar_im = _s5_prep(s5_lambda_re, s5_lambda_im, s5_log_step, s5_b_re, s5_b_im)
    wb, wc = _s5_weights(bbar_re, bbar_im, s5_c_re, s5_c_im)
    norm_w, s5_skip = vec(hg_norm_w), vec(s5_d)
    g1, b1, g2, b2 = vec(ln1_g), vec(ln1_b), vec(ln2_g), vec(ln2_b)

    h = x
    for l in range(depth):
        q, ka, va, u_tm = _in_proj(h, w_main, fox_bias, w_s5, l)
        ya = _hgrn2(h, w_main, hg_lower_bounds, norm_w, l)
        yb = _fox(q, ka, va)
        yc = _s5(u_tm, a_re, a_im, wb, wc, s5_skip, w_glu_b, bsz, l)
        h = _merge(h, ya, yb, yc, w_gate, w_a, w_b, w_c, w_o, g1, b1, alpha, l)
        h = _ffn(h, w_fg, w_fu, w_fd, g2, b2, alpha, l)
    return h
```

```python
import functools
import math

import jax
import jax.numpy as jnp
from jax import lax
from jax.experimental import pallas as pl
from jax.experimental.pallas import tpu as pltpu

F32 = jnp.float32
BF16 = jnp.bfloat16

D_MODEL = 1024
MIX_W = 512
HG_HEADS = 4
HG_KEY = 128
HG_CHUNK = 64
HG_REF_ROW = HG_CHUNK // 2 - 1
FOX_HEADS = 8
FOX_COL0 = 4 * MIX_W
FOX_HEAD_DIM = 64
S5_GROUPS = 32
S5_STATE = 64
S5_GROUP_CH = 16
S5_NSTATE = S5_GROUPS * S5_STATE
FFN_HIDDEN = 2816
LN_EPS = 1e-5
RMS_EPS = 1e-6
EXP2_CLAMP = 115.0
LOG2E = math.log2(math.e)

LANES = 128
MXU_DIM = 256
VMEM_LIMIT = 56 * 1024 * 1024

PROJ_ROWS = 1024
PROJ_GROUP = 2
HG_ROWS = 256
FOX_Q = 512
FOX_KV = 512
S5_PHASES = 4
S5_STEPS = 128
MERGE_ROWS = 1024
FFN_ROWS = 1024
ROW_SUB = 256
FFN_CHUNK = 6 * MXU_DIM


def _dot(a, b):
    return jnp.dot(a, b, preferred_element_type=F32)


def _dot_nt(a, b):
    return lax.dot_general(a, b, (((1,), (1,)), ((), ())), preferred_element_type=F32)


def _dot_tn(a, b):
    return lax.dot_general(a, b, (((0,), (0,)), ((), ())), preferred_element_type=F32)


def _sigmoid(x):
    return 1.0 / (1.0 + jnp.exp2(x * -LOG2E))


def _layer_norm(z, g, b):
    mu = jnp.mean(z, axis=-1, keepdims=True)
    zc = z - mu
    var = jnp.mean(zc * zc, axis=-1, keepdims=True)
    return zc * lax.rsqrt(var + LN_EPS) * g + b


def _cumsum(x, axis):
    n = x.shape[axis]
    idx = lax.broadcasted_iota(jnp.int32, x.shape, axis)
    d = 1
    while d < n:
        x = x + jnp.where(idx >= d, pltpu.roll(x, d, axis=axis), 0.0)
        d *= 2
    return x


def _layer_spec(w, layer):
    zeros = (0,) * (w.ndim - 1)
    return pl.BlockSpec((None,) + w.shape[1:], lambda *_: (layer,) + zeros, pipeline_mode=pl.Buffered(1))


def _params(sem):
    return pltpu.CompilerParams(dimension_semantics=sem, vmem_limit_bytes=VMEM_LIMIT)


def _fox_lanes(h, lane):
    if h % 2 == 0:
        return lane < FOX_HEAD_DIM, FOX_HEAD_DIM
    return lane >= FOX_HEAD_DIM, 0


def _proj_kernel(n_cast, x_ref, wq_ref, wk_ref, wv_ref, wff_ref, bf_ref, ws5_ref, *refs):
    cast_in = refs[:n_cast]
    q_ref, ka_ref, va_ref, s5_ref = refs[n_cast:n_cast + 4]
    cast_out = refs[n_cast + 4:2 * n_cast + 4]
    cum_ref = refs[2 * n_cast + 4]
    for src, dst in zip(cast_in, cast_out):
        dst[...] = src[...].astype(BF16)
    bsz, tt, _ = x_ref.shape

    @pl.when(pl.program_id(0) == 0)
    def _():
        cum_ref[...] = jnp.zeros_like(cum_ref)

    lane = lax.broadcasted_iota(jnp.int32, (1, LANES), 1)
    for b0 in range(0, bsz, PROJ_GROUP):
        nb = PROJ_GROUP
        rows = nb * tt
        xb = x_ref[b0:b0 + nb].reshape(rows, D_MODEL).astype(BF16)
        k = _dot(xb, wk_ref[...])
        v = _dot(xb, wv_ref[...])
        z = _dot(xb, wff_ref[...]) + bf_ref[...]
        q_ref[b0:b0 + nb] = _dot(xb, wq_ref[...]).astype(BF16).reshape(nb, tt, MIX_W)
        u = _dot(xb, ws5_ref[...])

        logsig = jnp.minimum(z, 0.0) - jnp.log(1.0 + jnp.exp(-jnp.abs(z)))
        t_idx = lax.broadcasted_iota(jnp.int32, (rows, LANES), 0) % tt
        cum = logsig
        d = 1
        while d < tt:
            cum = cum + jnp.where(t_idx >= d, pltpu.roll(cum, d, axis=0), 0.0)
            d *= 2
        carry = cum_ref[b0:b0 + nb, :]
        cum = jnp.concatenate([cum[b * tt:(b + 1) * tt] + carry[b:b + 1] for b in range(nb)], axis=0)
        cum_ref[b0:b0 + nb, :] = jnp.concatenate([cum[(b + 1) * tt - 1:(b + 1) * tt] for b in range(nb)], axis=0)

        bias = cum * (-LOG2E)
        for h in range(FOX_HEADS):
            sl = slice((h // 2) * LANES, (h // 2 + 1) * LANES)
            own, aux = _fox_lanes(h, lane)
            bh = jnp.broadcast_to(bias[:, h:h + 1], (rows, LANES))
            hi = bh.astype(BF16).astype(F32)
            mid = (bh - hi).astype(BF16).astype(F32)
            lo = bh - hi - mid
            extra = jnp.where(lane == aux, hi, jnp.where(lane == aux + 1, mid,
                                                         jnp.where(lane == aux + 2, lo, 0.0)))
            ka = jnp.where(own, k[:, sl], extra).astype(BF16)
            va = jnp.where(own, v[:, sl], jnp.where(lane == aux, 1.0, 0.0)).astype(BF16)
            for b in range(nb):
                ka_ref[b0 + b, h] = ka[b * tt:(b + 1) * tt]
                va_ref[b0 + b, h] = va[b * tt:(b + 1) * tt]

        for b in range(nb):
            for s in range(MIX_W // LANES):
                s5_ref[s, pl.ds(b0 + b, tt, stride=bsz), :] = u[b * tt:(b + 1) * tt, s * LANES:(s + 1) * LANES]


def _in_proj(h, w_all, bf, ws5, layer, cast=()):
    bsz, seq, _ = h.shape
    tt = PROJ_ROWS // bsz
    n_steps = seq // tt
    cast_specs = [pl.BlockSpec((w.shape[0] // n_steps, w.shape[1]), lambda i: (i, 0)) for w in cast]
    fox_spec = lambda j: pl.BlockSpec((None, D_MODEL, MIX_W), lambda i: (layer, 0, FOX_COL0 // MIX_W + j))
    head_spec = pl.BlockSpec((bsz, FOX_HEADS, tt, LANES), lambda i: (0, 0, i, 0))
    head_shape = jax.ShapeDtypeStruct((bsz, FOX_HEADS, seq, LANES), BF16)
    return pl.pallas_call(
        functools.partial(_proj_kernel, len(cast)),
        grid=(n_steps,),
        in_specs=[
            pl.BlockSpec((bsz, tt, D_MODEL), lambda i: (0, i, 0)),
            fox_spec(0), fox_spec(1), fox_spec(2),
            pl.BlockSpec((None, D_MODEL, LANES), lambda i: (layer, 0, (FOX_COL0 + 3 * MIX_W) // LANES)),
            pl.BlockSpec((None, 1, LANES), lambda i: (layer, 0, 0)),
            _layer_spec(ws5, layer),
        ] + cast_specs,
        out_specs=[
            pl.BlockSpec((bsz, tt, MIX_W), lambda i: (0, i, 0)),
            head_spec, head_spec,
            pl.BlockSpec((MIX_W // LANES, bsz * tt, LANES), lambda i: (0, i, 0)),
        ] + cast_specs,
        out_shape=[
            jax.ShapeDtypeStruct((bsz, seq, MIX_W), BF16),
            head_shape, head_shape,
            jax.ShapeDtypeStruct((MIX_W // LANES, seq * bsz, LANES), F32),
        ] + [jax.ShapeDtypeStruct(w.shape, BF16) for w in cast],
        scratch_shapes=[pltpu.VMEM((bsz, LANES), F32)],
        compiler_params=_params(("arbitrary",)),
        name="in_proj",
    )(h, w_all, w_all, w_all, w_all, bf, ws5, *cast)


def _hgrn2_kernel(layer, h_ref, hn_ref, w_ref, lbp_ref, nw_ref, o_ref, st_ref, hg0_ref, hg1_ref):
    i = pl.program_id(1)

    @pl.when(i == 0)
    def _():
        st_ref[...] = jnp.zeros_like(st_ref)
        hg0_ref[...] = _dot(h_ref[...].astype(BF16), w_ref[...])

    lbp = lbp_ref[...]
    e = jnp.exp(lbp - jnp.max(lbp, axis=0, keepdims=True))
    sm = e / jnp.sum(e, axis=0, keepdims=True)
    cum = sm[0:1]
    for j in range(1, layer + 1):
        cum = cum + sm[j:j + 1]
    lb = cum - sm[0:1]

    nw = nw_ref[...]
    c = HG_CHUNK
    n_chunks = h_ref.shape[0] // c
    tri = (lax.broadcasted_iota(jnp.int32, (c, c), 0) >= lax.broadcasted_iota(jnp.int32, (c, c), 1))

    def gates(ci, hg_ref):
        rows = slice(ci * c, (ci + 1) * c)
        q = hg_ref[rows, 0:MIX_W]
        fz = hg_ref[rows, MIX_W:2 * MIX_W]
        f = lb + (1.0 - lb) * _sigmoid(fz)
        k = 1.0 - f
        g = _cumsum(jnp.log2(f), 0)
        g_ref = g[HG_REF_ROW:HG_REF_ROW + 1]
        g_last = g[c - 1:c]
        d_ref = g - g_ref
        q_rel = (q * jnp.exp2(jnp.minimum(d_ref, EXP2_CLAMP))).astype(BF16)
        k_rel = (k * jnp.exp2(jnp.minimum(-d_ref, EXP2_CLAMP))).astype(BF16)
        q_in = (q * jnp.exp2(g)).astype(BF16)
        k_end = (k * jnp.exp2(g_last - g)).astype(BF16)
        s_decay = jnp.exp2(g_last)
        return q_rel, k_rel, q_in, k_end, s_decay

    def mix(ci, hg_ref, operands):
        rows = slice(ci * c, (ci + 1) * c)
        q_rel, k_rel, q_in, k_end, s_decay = operands
        v = hg_ref[rows, 2 * MIX_W:3 * MIX_W]
        gate = hg_ref[rows, 3 * MIX_W:4 * MIX_W]
        for h in range(HG_HEADS):
            sl = slice(h * HG_KEY, (h + 1) * HG_KEY)
            st = st_ref[h]
            v_t = v[:, sl].T.astype(BF16)
            scores = jnp.where(tri, _dot_nt(q_rel[:, sl], k_rel[:, sl]), 0.0)
            lhs = jnp.concatenate([q_in[:, sl], scores.astype(BF16)], axis=1)
            rhs_t = jnp.concatenate([st.astype(BF16), v_t], axis=1)
            o = _dot_nt(lhs, rhs_t)
            st_ref[h] = st * s_decay[:, sl] + _dot(v_t, k_end[:, sl])
            o = o * lax.rsqrt(jnp.mean(o * o, axis=-1, keepdims=True) + RMS_EPS) * nw
            gh = gate[:, sl]
            o_ref[rows, sl] = (o * (gh * _sigmoid(gh))).astype(BF16)

    def tile(cur_ref, nxt_ref):
        hn = hn_ref[...].astype(BF16)
        cols = w_ref.shape[1] // n_chunks
        operands = gates(0, cur_ref)
        for ci in range(n_chunks):
            ahead = gates(ci + 1, cur_ref) if ci + 1 < n_chunks else None
            mix(ci, cur_ref, operands)
            nxt_ref[:, ci * cols:(ci + 1) * cols] = _dot(hn, w_ref[:, ci * cols:(ci + 1) * cols])
            operands = ahead

    @pl.when(i % 2 == 0)
    def _():
        tile(hg0_ref, hg1_ref)

    @pl.when(i % 2 == 1)
    def _():
        tile(hg1_ref, hg0_ref)


def _hgrn2(h, w_all, lbp, nw, layer):
    bsz, seq, _ = h.shape
    tr = HG_ROWS
    n_tiles = seq // tr
    hg = pltpu.VMEM((tr, 4 * MIX_W), F32)
    return pl.pallas_call(
        functools.partial(_hgrn2_kernel, layer),
        grid=(bsz, n_tiles),
        in_specs=[
            pl.BlockSpec((None, tr, D_MODEL), lambda b, i: (b, i, 0)),
            pl.BlockSpec((None, tr, D_MODEL), lambda b, i: (b, jnp.minimum(i + 1, n_tiles - 1), 0)),
            pl.BlockSpec((None, D_MODEL, 4 * MIX_W), lambda b, i: (layer, 0, 0)),
            pl.BlockSpec(lbp.shape, lambda b, i: (0, 0)),
            _layer_spec(nw, layer),
        ],
        out_specs=pl.BlockSpec((None, tr, MIX_W), lambda b, i: (b, i, 0)),
        out_shape=jax.ShapeDtypeStruct((bsz, seq, MIX_W), BF16),
        scratch_shapes=[pltpu.VMEM((HG_HEADS, HG_KEY, HG_KEY), F32), hg, hg],
        compiler_params=_params(("arbitrary", "arbitrary")),
        name="hgrn2",
    )(h, h, w_all, lbp, nw)


def _fox_kernel(q_ref, ka_ref, va_ref, o_ref, qa_ref, m_ref, acc_ref, s_ref):
    qi = pl.program_id(1)
    tq, tk = FOX_Q, FOX_KV
    n_q = q_ref.shape[0] // tq
    lane = lax.broadcasted_iota(jnp.int32, (1, LANES), 1)
    causal = (lax.broadcasted_iota(jnp.int32, (tq, tk), 0) >= lax.broadcasted_iota(jnp.int32, (tq, tk), 1))

    def load_queries(t):
        for h in range(FOX_HEADS):
            sl = slice((h // 2) * LANES, (h // 2 + 1) * LANES)
            own, aux = _fox_lanes(h, lane)
            ones = jnp.where((lane >= aux) & (lane < aux + 3), 1.0, 0.0)
            q = q_ref[t * tq:(t + 1) * tq, sl].astype(F32) * (FOX_HEAD_DIM ** -0.5 * LOG2E)
            qa_ref[h] = jnp.where(own, q, ones).astype(BF16)

    def reset_state():
        m_ref[...] = jnp.full(m_ref.shape, -1e30, F32)
        acc_ref[...] = jnp.zeros(acc_ref.shape, F32)

    hq = tq // 2
    spans = ((slice(0, hq), tk // 2), (slice(hq, tq), tk))

    def logits(h, kb, buf, diag=False):
        if not diag:
            s_ref[buf, h] = _dot_nt(qa_ref[h], ka_ref[h, kb * tk:(kb + 1) * tk, :])
            return
        for rows, nk in spans:
            s_ref[buf, h, rows, 0:nk] = _dot_nt(qa_ref[h, rows, :], ka_ref[h, kb * tk:kb * tk + nk, :])

    def softmax_pv(h, kb, s, rows):
        nk = s.shape[1]
        m_prev = m_ref[h, rows, :]
        m_new = jnp.maximum(m_prev, jnp.max(s, axis=-1, keepdims=True))
        p = jnp.concatenate([jnp.exp2(s[:, j * LANES:(j + 1) * LANES] - m_new)
                             for j in range(nk // LANES)], axis=1)
        pv = _dot(p.astype(BF16), va_ref[h, kb * tk:kb * tk + nk, :])
        acc_ref[h, rows, :] = jnp.exp2(m_prev - m_new) * acc_ref[h, rows, :] + pv
        m_ref[h, rows, :] = m_new

    def update(h, kb, buf, masked):
        if not masked:
            softmax_pv(h, kb, s_ref[buf, h], slice(0, tq))
            return
        for rows, nk in spans:
            s = s_ref[buf, h, rows, 0:nk]
            softmax_pv(h, kb, jnp.where(causal[rows, 0:nk], s, -1e30), rows)

    def write_out():
        for pair in range(FOX_HEADS // 2):
            a = acc_ref[2 * pair]
            b = acc_ref[2 * pair + 1]
            o = jnp.where(lane < FOX_HEAD_DIM, a / a[:, FOX_HEAD_DIM:FOX_HEAD_DIM + 1], b / b[:, 0:1])
            o_ref[:, pair * LANES:(pair + 1) * LANES] = o.astype(BF16)

    start = [0]
    for t in range(1, n_q):
        start.append(1 - (start[t - 1] + t - 1) % 2)
    for t in range(n_q):
        @pl.when(qi == t)
        def _(t=t):
            if t == 0:
                load_queries(0)
                reset_state()
                for h in range(FOX_HEADS):
                    logits(h, 0, start[0], diag=True)
            for j in range(t):
                for h in range(FOX_HEADS):
                    logits(h, j + 1, (start[t] + j + 1) % 2, diag=(j + 1 == t))
                    update(h, j, (start[t] + j) % 2, False)
            if t + 1 < n_q:
                load_queries(t + 1)
            for h in range(FOX_HEADS):
                update(h, t, (start[t] + t) % 2, True)
                if t + 1 < n_q:
                    logits(h, 0, start[t + 1])
            write_out()
            if t + 1 < n_q:
                reset_state()


def _fox(q, ka, va):
    bsz, seq, _ = q.shape
    tq = FOX_Q
    state = pltpu.VMEM((FOX_HEADS, tq, LANES), F32)
    full_spec = pl.BlockSpec((None, FOX_HEADS, seq, LANES), lambda b, i: (b, 0, 0, 0))
    return pl.pallas_call(
        _fox_kernel,
        grid=(bsz, seq // tq),
        in_specs=[pl.BlockSpec((None, seq, MIX_W), lambda b, i: (b, 0, 0)), full_spec, full_spec],
        out_specs=pl.BlockSpec((None, tq, MIX_W), lambda b, i: (b, i, 0)),
        out_shape=jax.ShapeDtypeStruct((bsz, seq, MIX_W), BF16),
        scratch_shapes=[pltpu.VMEM((FOX_HEADS, tq, LANES), BF16), state, state,
                        pltpu.VMEM((2, FOX_HEADS, tq, FOX_KV), F32)],
        compiler_params=_params(("arbitrary", "arbitrary")),
        name="fox_attn",
    )(q, ka, va)


def _s5_prep_kernel(lre_ref, lim_ref, ls_ref, bre_ref, bim_ref, are_ref, aim_ref, bbr_ref, bbi_ref):
    lr, li = lre_ref[...], lim_ref[...]
    dt = jnp.exp(ls_ref[...])
    mag = jnp.exp(lr * dt)
    ar = mag * jnp.cos(li * dt)
    ai = mag * jnp.sin(li * dt)
    den = lr * lr + li * li
    cr = ((ar - 1.0) * lr + ai * li) / den
    ci = (ai * lr - (ar - 1.0) * li) / den
    br, bi = bre_ref[...], bim_ref[...]
    bbr_ref[...] = cr * br - ci * bi
    bbi_ref[...] = cr * bi + ci * br
    are_ref[...] = jnp.broadcast_to(ar, are_ref.shape)
    aim_ref[...] = jnp.broadcast_to(ai, aim_ref.shape)


def _s5_prep(lam_re, lam_im, log_step, b_re, b_im):
    depth = lam_re.shape[0]
    row = lambda a: a.reshape(depth, 1, S5_NSTATE)
    ls = jnp.repeat(log_step, S5_STATE, axis=1).reshape(depth, 1, S5_NSTATE)
    to_hp = lambda b: b.transpose(0, 3, 1, 2).reshape(depth, S5_GROUP_CH, S5_NSTATE)
    spec = lambda r: pl.BlockSpec((None, r, S5_NSTATE), lambda l: (l, 0, 0))
    vec = jax.ShapeDtypeStruct((depth, 8, S5_NSTATE), F32)
    mat = jax.ShapeDtypeStruct((depth, S5_GROUP_CH, S5_NSTATE), F32)
    return pl.pallas_call(
        _s5_prep_kernel,
        grid=(depth,),
        in_specs=[spec(1), spec(1), spec(1), spec(S5_GROUP_CH), spec(S5_GROUP_CH)],
        out_specs=[spec(8), spec(8), spec(S5_GROUP_CH), spec(S5_GROUP_CH)],
        out_shape=[vec, vec, mat, mat],
        name="s5_prep",
    )(row(lam_re), row(lam_im), ls, to_hp(b_re), to_hp(b_im))


def _s5_kernel(u_ref, are_ref, aim_ref, wb_ref, wc_ref, d_ref, wglu_ref, o_ref, x_ref, st_ref, y_ref):
    @pl.when(pl.program_id(0) == 0)
    def _():
        st_ref[...] = jnp.zeros_like(st_ref)

    n = S5_NSTATE
    n_tiles = 2 * n // MXU_DIM
    total_rows = u_ref.shape[1]
    steps = total_rows // 8
    width = 1024

    def load_u(rows):
        return jnp.concatenate([u_ref[s, rows, :] for s in range(MIX_W // LANES)], axis=1)

    def project_in(rows):
        ub = load_u(rows).astype(BF16)
        for j in range(n_tiles):
            kh = ((j % (n_tiles // 2)) * MXU_DIM // S5_STATE * S5_GROUP_CH) // MXU_DIM
            x_ref[rows, j * MXU_DIM:(j + 1) * MXU_DIM] = _dot(ub[:, kh * MXU_DIM:(kh + 1) * MXU_DIM], wb_ref[j])

    def scan(t0, t1, state):
        for i, c0 in enumerate(range(0, n, width)):
            re_sl = slice(c0, c0 + width)
            im_sl = slice(n + c0, n + c0 + width)
            ar = are_ref[:, re_sl]
            ai = aim_ref[:, re_sl]
            xr, xi = state[i]
            for t in range(t0, t1):
                rows = slice(t * 8, t * 8 + 8)
                xr, xi = (ar * xr - ai * xi + x_ref[rows, re_sl], ar * xi + ai * xr + x_ref[rows, im_sl])
                x_ref[rows, re_sl] = xr
                x_ref[rows, im_sl] = xi
            state[i] = (xr, xi)

    def project_out(rows):
        halves = []
        per_half = n // MXU_DIM // 2
        for nh in range(MIX_W // MXU_DIM):
            acc_re = None
            acc_im = None
            for kk in range(per_half):
                kr = nh * per_half + kk
                ki = n // MXU_DIM + kr
                pr = _dot(x_ref[rows, kr * MXU_DIM:(kr + 1) * MXU_DIM].astype(BF16), wc_ref[kr])
                pi = _dot(x_ref[rows, ki * MXU_DIM:(ki + 1) * MXU_DIM].astype(BF16), wc_ref[ki])
                acc_re = pr if acc_re is None else acc_re + pr
                acc_im = pi if acc_im is None else acc_im + pi
            halves.append(acc_re - acc_im)
        y = jnp.concatenate(halves, axis=1) + d_ref[...] * load_u(rows)
        y = 0.5 * y * (1.0 + jnp.tanh(math.sqrt(2.0 / math.pi) * (y + 0.044715 * (y * y * y))))
        y = y * _sigmoid(_dot(y.astype(BF16), wglu_ref[...]))
        for s in range(MIX_W // LANES):
            y_ref[s, rows, :] = y[:, s * LANES:(s + 1) * LANES]

    phase_rows = total_rows // S5_PHASES
    phases = [slice(p * phase_rows, (p + 1) * phase_rows) for p in range(S5_PHASES)]
    for rows in phases:
        project_in(rows)
    state = [(st_ref[:, c0:c0 + width], st_ref[:, n + c0:n + c0 + width]) for c0 in range(0, n, width)]
    for p, rows in enumerate(phases):
        scan(p * steps // S5_PHASES, (p + 1) * steps // S5_PHASES, state)
        project_out(rows)
    for i, c0 in enumerate(range(0, n, width)):
        st_ref[:, c0:c0 + width] = state[i][0]
        st_ref[:, n + c0:n + c0 + width] = state[i][1]
    bsz = o_ref.shape[0]
    for b in range(bsz):
        o_ref[b] = jnp.concatenate([y_ref[s, pl.ds(b, steps, stride=bsz), :]
                                    for s in range(MIX_W // LANES)], axis=1).astype(BF16)


def _s5(u_tm, a_re, a_im, wb, wc, d, wglu, bsz, layer):
    rows = S5_STEPS * bsz
    total = u_tm.shape[1]
    return pl.pallas_call(
        _s5_kernel,
        grid=(total // rows,),
        in_specs=[
            pl.BlockSpec((MIX_W // LANES, rows, LANES), lambda i: (0, i, 0)),
            _layer_spec(a_re, layer), _layer_spec(a_im, layer),
            _layer_spec(wb, layer), _layer_spec(wc, layer),
            _layer_spec(d, layer), _layer_spec(wglu, layer),
        ],
        out_specs=pl.BlockSpec((bsz, S5_STEPS, MIX_W), lambda i: (0, i, 0)),
        out_shape=jax.ShapeDtypeStruct((bsz, total // bsz, MIX_W), BF16),
        scratch_shapes=[pltpu.VMEM((rows, 2 * S5_NSTATE), F32), pltpu.VMEM((8, 2 * S5_NSTATE), F32),
                        pltpu.VMEM((MIX_W // LANES, rows, LANES), F32)],
        compiler_params=_params(("arbitrary",)),
        name="s5",
    )(u_tm, a_re, a_im, wb, wc, d, wglu)


def _s5_weights(bbar_re, bbar_im, c_re, c_im):
    half = S5_NSTATE // MXU_DIM
    g_per_tile = MXU_DIM // S5_STATE
    g_per_slab = MXU_DIM // S5_GROUP_CH
    depth = bbar_re.shape[0]
    t = jnp.arange(half)[:, None, None]
    r = jnp.arange(MXU_DIM)[None, :, None]
    c = jnp.arange(MXU_DIM)[None, None, :]
    slab = (t * g_per_tile) // g_per_slab
    b_keep = slab * g_per_slab + r // S5_GROUP_CH == t * g_per_tile + c // S5_STATE
    c_keep = t * g_per_tile + r // S5_STATE == slab * g_per_slab + c // S5_GROUP_CH

    def b_tiles(bb):
        bb = bb.reshape(depth, S5_GROUP_CH, half, MXU_DIM).transpose(0, 2, 1, 3)
        return jnp.where(b_keep, jnp.tile(bb, (1, 1, g_per_slab, 1)), 0.0)

    def c_tiles(cc):
        cc = cc.transpose(0, 1, 3, 2).reshape(depth, half, MXU_DIM, S5_GROUP_CH)
        return jnp.where(c_keep, jnp.tile(cc, (1, 1, 1, g_per_slab)), 0.0)

    wb = jnp.concatenate([b_tiles(bbar_re), b_tiles(bbar_im)], axis=1).astype(BF16)
    wc = jnp.concatenate([c_tiles(c_re), c_tiles(c_im)], axis=1).astype(BF16)
    return wb, wc


def _merge_kernel(alpha, h_ref, ya_ref, yb_ref, yc_ref, wg_ref, wa_ref, wb_ref, wc_ref, wo_ref,
                  g_ref, b_ref, o_ref):
    for r0 in range(0, h_ref.shape[0], ROW_SUB):
        rows = slice(r0, r0 + ROW_SUB)
        h = h_ref[rows, :]
        hb = h.astype(BF16)
        merged = None
        for i, (y_ref, w_ref) in enumerate(((ya_ref, wa_ref), (yb_ref, wb_ref), (yc_ref, wc_ref))):
            gate = _sigmoid(_dot(hb, wg_ref[:, i * D_MODEL:(i + 1) * D_MODEL]))
            term = gate * _dot(y_ref[rows, :], w_ref[...])
            merged = term if merged is None else merged + term
        mix = _dot(merged.astype(BF16), wo_ref[...])
        o_ref[rows, :] = _layer_norm(alpha * h + mix, g_ref[...], b_ref[...])


def _merge(h, ya, yb, yc, wg, wa, wb, wc, wo, g, b, alpha, layer):
    bsz, seq, _ = h.shape
    tm = MERGE_ROWS
    row_spec = lambda w: pl.BlockSpec((None, tm, w), lambda bb, i: (bb, i, 0))
    return pl.pallas_call(
        functools.partial(_merge_kernel, alpha),
        grid=(bsz, seq // tm),
        in_specs=[
            row_spec(D_MODEL), row_spec(MIX_W), row_spec(MIX_W), row_spec(MIX_W),
            _layer_spec(wg, layer), _layer_spec(wa, layer), _layer_spec(wb, layer),
            _layer_spec(wc, layer), _layer_spec(wo, layer), _layer_spec(g, layer), _layer_spec(b, layer),
        ],
        out_specs=row_spec(D_MODEL),
        out_shape=jax.ShapeDtypeStruct(h.shape, F32),
        compiler_params=_params(("arbitrary", "arbitrary")),
        name="merge_ln",
    )(h, ya, yb, yc, wg, wa, wb, wc, wo, g, b)


def _ffn_kernel(alpha, h_ref, wg_ref, wu_ref, wd_ref, g_ref, b_ref, o_ref):
    for r0 in range(0, h_ref.shape[0], ROW_SUB):
        rows = slice(r0, r0 + ROW_SUB)
        h = h_ref[rows, :]
        hb = h.astype(BF16)
        acc = None
        for c0 in range(0, FFN_HIDDEN, FFN_CHUNK):
            sl = slice(c0, min(c0 + FFN_CHUNK, FFN_HIDDEN))
            a = _dot(hb, wg_ref[:, sl])
            hid = (a * _sigmoid(a)) * _dot(hb, wu_ref[:, sl])
            part = _dot(hid.astype(BF16), wd_ref[sl, :])
            acc = part if acc is None else acc + part
        o_ref[rows, :] = _layer_norm(alpha * h + acc, g_ref[...], b_ref[...])


def _ffn(h, wg, wu, wd, g, b, alpha, layer):
    bsz, seq, _ = h.shape
    tm = FFN_ROWS
    row_spec = pl.BlockSpec((None, tm, D_MODEL), lambda bb, i: (bb, i, 0))
    return pl.pallas_call(
        functools.partial(_ffn_kernel, alpha),
        grid=(bsz, seq // tm),
        in_specs=[row_spec, _layer_spec(wg, layer), _layer_spec(wu, layer), _layer_spec(wd, layer),
                  _layer_spec(g, layer), _layer_spec(b, layer)],
        out_specs=row_spec,
        out_shape=jax.ShapeDtypeStruct(h.shape, F32),
        compiler_params=_params(("arbitrary", "arbitrary")),
        name="ffn_ln",
    )(h, wg, wu, wd, g, b)


def kernel(x, w_in, hg_lower_bounds, hg_norm_w, fox_b_f, s5_lambda_re, s5_lambda_im, s5_log_step,
           s5_b_re, s5_b_im, s5_c_re, s5_c_im, s5_d, s5_w_glu, w_br_a, w_br_b, w_br_c, w_out,
           ln1_g, ln1_b, w_ffn_gate, w_ffn_up, w_ffn_down, ln2_g, ln2_b):
    depth = w_in.shape[0]
    bsz, seq, _ = x.shape
    alpha = (2 * depth) ** 0.25
    o_s5 = FOX_COL0 + 3 * MIX_W + FOX_HEADS
    o_gate = o_s5 + MIX_W

    bf = lambda a: a.astype(BF16)
    vec = lambda a: a.reshape(depth, 1, -1)
    w_main = bf(w_in)
    w_s5, w_gate = w_main[:, :, o_s5:o_gate], w_main[:, :, o_gate:]
    w_glu_b = bf(s5_w_glu)
    w_a, w_b, w_c, w_o = bf(w_br_a), bf(w_br_b), bf(w_br_c), bf(w_out)
    ffn_f32 = tuple(w.reshape(-1, w.shape[-1]) for w in (w_ffn_gate, w_ffn_up, w_ffn_down))
    fox_bias = jnp.pad(fox_b_f, ((0, 0), (0, LANES - FOX_HEADS))).reshape(depth, 1, LANES)
    a_re, a_im, bbar_re, bbar_im = _s5_prep(s5_lambda_re, s5_lambda_im, s5_log_step, s5_b_re, s5_b_im)
    wb, wc = _s5_weights(bbar_re, bbar_im, s5_c_re, s5_c_im)
    norm_w, s5_skip = vec(hg_norm_w), vec(s5_d)
    g1, b1, g2, b2 = vec(ln1_g), vec(ln1_b), vec(ln2_g), vec(ln2_b)

    h = x
    for l in range(depth):
        q, ka, va, u_tm, *ffn_b = _in_proj(h, w_main, fox_bias, w_s5, l, ffn_f32 if l == 0 else ())
        if l == 0:
            w_fg, w_fu, w_fd = (wb16.reshape(wf.shape) for wb16, wf in
                                zip(ffn_b, (w_ffn_gate, w_ffn_up, w_ffn_down)))
        ya = _hgrn2(h, w_main, hg_lower_bounds, norm_w, l)
        yb = _fox(q, ka, va)
        yc = _s5(u_tm, a_re, a_im, wb, wc, s5_skip, w_glu_b, bsz, l)
        h = _merge(h, ya, yb, yc, w_gate, w_a, w_b, w_c, w_o, g1, b1, alpha, l)
        h = _ffn(h, w_fg, w_fu, w_fd, g2, b2, alpha, l)
    return h
```

```python
import functools
import math

import jax
import jax.numpy as jnp
from jax import lax
from jax.experimental import pallas as pl
from jax.experimental.pallas import tpu as pltpu

F32 = jnp.float32
BF16 = jnp.bfloat16

D_MODEL = 1024
MIX_W = 512
HG_HEADS = 4
HG_KEY = 128
HG_CHUNK = 64
HG_REF_ROW = HG_CHUNK // 2 - 1
FOX_HEADS = 8
FOX_COL0 = 4 * MIX_W
FOX_HEAD_DIM = 64
S5_GROUPS = 32
S5_STATE = 64
S5_GROUP_CH = 16
S5_NSTATE = S5_GROUPS * S5_STATE
FFN_HIDDEN = 2816
LN_EPS = 1e-5
RMS_EPS = 1e-6
EXP2_CLAMP = 115.0
LOG2E = math.log2(math.e)

LANES = 128
MXU_DIM = 256
VMEM_LIMIT = 56 * 1024 * 1024

PROJ_ROWS = 1024
PROJ_GROUP = 2
HG_ROWS = 256
FOX_Q = 512
FOX_KV = 512
S5_PHASES = 4
S5_STEPS = 128
MERGE_ROWS = 1024
FFN_ROWS = 1024
ROW_SUB = 256
FFN_CHUNK = 6 * MXU_DIM


def _dot(a, b):
    return jnp.dot(a, b, preferred_element_type=F32)


def _dot_nt(a, b):
    return lax.dot_general(a, b, (((1,), (1,)), ((), ())), preferred_element_type=F32)


def _dot_tn(a, b):
    return lax.dot_general(a, b, (((0,), (0,)), ((), ())), preferred_element_type=F32)


def _sigmoid(x):
    return 1.0 / (1.0 + jnp.exp2(x * -LOG2E))


def _layer_norm(z, g, b):
    mu = jnp.mean(z, axis=-1, keepdims=True)
    zc = z - mu
    var = jnp.mean(zc * zc, axis=-1, keepdims=True)
    return zc * lax.rsqrt(var + LN_EPS) * g + b


def _cumsum(x, axis):
    n = x.shape[axis]
    idx = lax.broadcasted_iota(jnp.int32, x.shape, axis)
    d = 1
    while d < n:
        x = x + jnp.where(idx >= d, pltpu.roll(x, d, axis=axis), 0.0)
        d *= 2
    return x


def _layer_spec(w, layer):
    zeros = (0,) * (w.ndim - 1)
    return pl.BlockSpec((None,) + w.shape[1:], lambda *_: (layer,) + zeros, pipeline_mode=pl.Buffered(1))


def _params(sem):
    return pltpu.CompilerParams(dimension_semantics=sem, vmem_limit_bytes=VMEM_LIMIT)


def _fox_lanes(h, lane):
    if h % 2 == 0:
        return lane < FOX_HEAD_DIM, FOX_HEAD_DIM
    return lane >= FOX_HEAD_DIM, 0


def _proj_kernel(n_cast, x_ref, wq_ref, wk_ref, wv_ref, wff_ref, bf_ref, ws5_ref, *refs):
    cast_in = refs[:n_cast]
    q_ref, ka_ref, va_ref, s5_ref = refs[n_cast:n_cast + 4]
    cast_out = refs[n_cast + 4:2 * n_cast + 4]
    cum_ref = refs[2 * n_cast + 4]
    for src, dst in zip(cast_in, cast_out):
        dst[...] = src[...].astype(BF16)
    bsz, tt, _ = x_ref.shape

    @pl.when(pl.program_id(0) == 0)
    def _():
        cum_ref[...] = jnp.zeros_like(cum_ref)

    lane = lax.broadcasted_iota(jnp.int32, (1, LANES), 1)
    for b0 in range(0, bsz, PROJ_GROUP):
        nb = PROJ_GROUP
        rows = nb * tt
        xb = x_ref[b0:b0 + nb].reshape(rows, D_MODEL).astype(BF16)
        k = _dot(xb, wk_ref[...])
        v = _dot(xb, wv_ref[...])
        z = _dot(xb, wff_ref[...]) + bf_ref[...]
        q_ref[b0:b0 + nb] = _dot(xb, wq_ref[...]).astype(BF16).reshape(nb, tt, MIX_W)
        u = _dot(xb, ws5_ref[...])

        logsig = jnp.minimum(z, 0.0) - jnp.log(1.0 + jnp.exp(-jnp.abs(z)))
        t_idx = lax.broadcasted_iota(jnp.int32, (rows, LANES), 0) % tt
        cum = logsig
        d = 1
        while d < tt:
            cum = cum + jnp.where(t_idx >= d, pltpu.roll(cum, d, axis=0), 0.0)
            d *= 2
        carry = cum_ref[b0:b0 + nb, :]
        cum = jnp.concatenate([cum[b * tt:(b + 1) * tt] + carry[b:b + 1] for b in range(nb)], axis=0)
        cum_ref[b0:b0 + nb, :] = jnp.concatenate([cum[(b + 1) * tt - 1:(b + 1) * tt] for b in range(nb)], axis=0)

        bias = cum * (-LOG2E)
        for h in range(FOX_HEADS):
            sl = slice((h // 2) * LANES, (h // 2 + 1) * LANES)
            own, aux = _fox_lanes(h, lane)
            bh = jnp.broadcast_to(bias[:, h:h + 1], (rows, LANES))
            hi = bh.astype(BF16).astype(F32)
            mid = (bh - hi).astype(BF16).astype(F32)
            lo = bh - hi - mid
            extra = jnp.where(lane == aux, hi, jnp.where(lane == aux + 1, mid,
                                                         jnp.where(lane == aux + 2, lo, 0.0)))
            ka = jnp.where(own, k[:, sl], extra).astype(BF16)
            va = jnp.where(own, v[:, sl], jnp.where(lane == aux, 1.0, 0.0)).astype(BF16)
            for b in range(nb):
                ka_ref[b0 + b, h] = ka[b * tt:(b + 1) * tt]
                va_ref[b0 + b, h] = va[b * tt:(b + 1) * tt]

        for b in range(nb):
            for s in range(MIX_W // LANES):
                s5_ref[s, pl.ds(b0 + b, tt, stride=bsz), :] = u[b * tt:(b + 1) * tt, s * LANES:(s + 1) * LANES]


def _in_proj(h, w_all, bf, ws5, layer, cast=()):
    bsz, seq, _ = h.shape
    tt = PROJ_ROWS // bsz
    n_steps = seq // tt
    cast_specs = [pl.BlockSpec((w.shape[0] // n_steps, w.shape[1]), lambda i: (i, 0)) for w in cast]
    fox_spec = lambda j: pl.BlockSpec((None, D_MODEL, MIX_W), lambda i: (layer, 0, FOX_COL0 // MIX_W + j))
    head_spec = pl.BlockSpec((bsz, FOX_HEADS, tt, LANES), lambda i: (0, 0, i, 0))
    head_shape = jax.ShapeDtypeStruct((bsz, FOX_HEADS, seq, LANES), BF16)
    return pl.pallas_call(
        functools.partial(_proj_kernel, len(cast)),
        grid=(n_steps,),
        in_specs=[
            pl.BlockSpec((bsz, tt, D_MODEL), lambda i: (0, i, 0)),
            fox_spec(0), fox_spec(1), fox_spec(2),
            pl.BlockSpec((None, D_MODEL, LANES), lambda i: (layer, 0, (FOX_COL0 + 3 * MIX_W) // LANES)),
            pl.BlockSpec((None, 1, LANES), lambda i: (layer, 0, 0)),
            _layer_spec(ws5, layer),
        ] + cast_specs,
        out_specs=[
            pl.BlockSpec((bsz, tt, MIX_W), lambda i: (0, i, 0)),
            head_spec, head_spec,
            pl.BlockSpec((MIX_W // LANES, bsz * tt, LANES), lambda i: (0, i, 0)),
        ] + cast_specs,
        out_shape=[
            jax.ShapeDtypeStruct((bsz, seq, MIX_W), BF16),
            head_shape, head_shape,
            jax.ShapeDtypeStruct((MIX_W // LANES, seq * bsz, LANES), F32),
        ] + [jax.ShapeDtypeStruct(w.shape, BF16) for w in cast],
        scratch_shapes=[pltpu.VMEM((bsz, LANES), F32)],
        compiler_params=_params(("arbitrary",)),
        name="in_proj",
    )(h, w_all, w_all, w_all, w_all, bf, ws5, *cast)


def _hgrn2_kernel(layer, h_ref, hn_ref, w_ref, lbp_ref, nw_ref, o_ref, st_ref, hg0_ref, hg1_ref):
    i = pl.program_id(1)

    @pl.when(i == 0)
    def _():
        st_ref[...] = jnp.zeros_like(st_ref)
        hg0_ref[...] = _dot(h_ref[...].astype(BF16), w_ref[...])

    lbp = lbp_ref[...]
    e = jnp.exp(lbp - jnp.max(lbp, axis=0, keepdims=True))
    sm = e / jnp.sum(e, axis=0, keepdims=True)
    cum = sm[0:1]
    for j in range(1, layer + 1):
        cum = cum + sm[j:j + 1]
    lb = cum - sm[0:1]

    nw = nw_ref[...]
    c = HG_CHUNK
    n_chunks = h_ref.shape[0] // c
    tri = (lax.broadcasted_iota(jnp.int32, (c, c), 0) >= lax.broadcasted_iota(jnp.int32, (c, c), 1))

    def gates(ci, hg_ref):
        rows = slice(ci * c, (ci + 1) * c)
        q = hg_ref[rows, 0:MIX_W]
        fz = hg_ref[rows, MIX_W:2 * MIX_W]
        f = lb + (1.0 - lb) * _sigmoid(fz)
        k = 1.0 - f
        g = _cumsum(jnp.log2(f), 0)
        g_ref = g[HG_REF_ROW:HG_REF_ROW + 1]
        g_last = g[c - 1:c]
        d_ref = g - g_ref
        q_rel = (q * jnp.exp2(jnp.minimum(d_ref, EXP2_CLAMP))).astype(BF16)
        k_rel = (k * jnp.exp2(jnp.minimum(-d_ref, EXP2_CLAMP))).astype(BF16)
        q_in = (q * jnp.exp2(g)).astype(BF16)
        k_end = (k * jnp.exp2(g_last - g)).astype(BF16)
        s_decay = jnp.exp2(g_last)
        return q_rel, k_rel, q_in, k_end, s_decay

    def mix(ci, hg_ref, operands):
        rows = slice(ci * c, (ci + 1) * c)
        q_rel, k_rel, q_in, k_end, s_decay = operands
        v = hg_ref[rows, 2 * MIX_W:3 * MIX_W]
        gate = hg_ref[rows, 3 * MIX_W:4 * MIX_W]
        for h in range(HG_HEADS):
            sl = slice(h * HG_KEY, (h + 1) * HG_KEY)
            st = st_ref[h]
            v_t = v[:, sl].T.astype(BF16)
            scores = jnp.where(tri, _dot_nt(q_rel[:, sl], k_rel[:, sl]), 0.0)
            lhs = jnp.concatenate([q_in[:, sl], scores.astype(BF16)], axis=1)
            rhs_t = jnp.concatenate([st.astype(BF16), v_t], axis=1)
            o = _dot_nt(lhs, rhs_t)
            st_ref[h] = st * s_decay[:, sl] + _dot(v_t, k_end[:, sl])
            o = o * lax.rsqrt(jnp.mean(o * o, axis=-1, keepdims=True) + RMS_EPS) * nw
            gh = gate[:, sl]
            o_ref[rows, sl] = (o * (gh * _sigmoid(gh))).astype(BF16)

    def tile(cur_ref, nxt_ref):
        hn = hn_ref[...].astype(BF16)
        cols = w_ref.shape[1] // n_chunks
        operands = gates(0, cur_ref)
        for ci in range(n_chunks):
            ahead = gates(ci + 1, cur_ref) if ci + 1 < n_chunks else None
            mix(ci, cur_ref, operands)
            nxt_ref[:, ci * cols:(ci + 1) * cols] = _dot(hn, w_ref[:, ci * cols:(ci + 1) * cols])
            operands = ahead

    @pl.when(i % 2 == 0)
    def _():
        tile(hg0_ref, hg1_ref)

    @pl.when(i % 2 == 1)
    def _():
        tile(hg1_ref, hg0_ref)


def _hgrn2(h, w_all, lbp, nw, layer):
    bsz, seq, _ = h.shape
    tr = HG_ROWS
    n_tiles = seq // tr
    hg = pltpu.VMEM((tr, 4 * MIX_W), F32)
    return pl.pallas_call(
        functools.partial(_hgrn2_kernel, layer),
        grid=(bsz, n_tiles),
        in_specs=[
            pl.BlockSpec((None, tr, D_MODEL), lambda b, i: (b, i, 0)),
            pl.BlockSpec((None, tr, D_MODEL), lambda b, i: (b, jnp.minimum(i + 1, n_tiles - 1), 0)),
            pl.BlockSpec((None, D_MODEL, 4 * MIX_W), lambda b, i: (layer, 0, 0)),
            pl.BlockSpec(lbp.shape, lambda b, i: (0, 0)),
            _layer_spec(nw, layer),
        ],
        out_specs=pl.BlockSpec((None, tr, MIX_W), lambda b, i: (b, i, 0)),
        out_shape=jax.ShapeDtypeStruct((bsz, seq, MIX_W), BF16),
        scratch_shapes=[pltpu.VMEM((HG_HEADS, HG_KEY, HG_KEY), F32), hg, hg],
        compiler_params=_params(("arbitrary", "arbitrary")),
        name="hgrn2",
    )(h, h, w_all, lbp, nw)


def _fox_kernel(q_ref, ka_ref, va_ref, o_ref, qa_ref, m_ref, acc_ref, s_ref):
    qi = pl.program_id(1)
    tq, tk = FOX_Q, FOX_KV
    n_q = q_ref.shape[0] // tq
    lane = lax.broadcasted_iota(jnp.int32, (1, LANES), 1)
    causal = (lax.broadcasted_iota(jnp.int32, (tq, tk), 0) >= lax.broadcasted_iota(jnp.int32, (tq, tk), 1))

    def load_queries(t):
        for h in range(FOX_HEADS):
            sl = slice((h // 2) * LANES, (h // 2 + 1) * LANES)
            own, aux = _fox_lanes(h, lane)
            ones = jnp.where((lane >= aux) & (lane < aux + 3), 1.0, 0.0)
            q = q_ref[t * tq:(t + 1) * tq, sl].astype(F32) * (FOX_HEAD_DIM ** -0.5 * LOG2E)
            qa_ref[h] = jnp.where(own, q, ones).astype(BF16)

    def reset_state():
        m_ref[...] = jnp.full(m_ref.shape, -1e30, F32)
        acc_ref[...] = jnp.zeros(acc_ref.shape, F32)

    hq = tq // 2
    spans = ((slice(0, hq), tk // 2), (slice(hq, tq), tk))

    def logits(h, kb, buf, diag=False):
        if not diag:
            s_ref[buf, h] = _dot_nt(qa_ref[h], ka_ref[h, kb * tk:(kb + 1) * tk, :])
            return
        for rows, nk in spans:
            s_ref[buf, h, rows, 0:nk] = _dot_nt(qa_ref[h, rows, :], ka_ref[h, kb * tk:kb * tk + nk, :])

    def softmax_pv(h, kb, s, rows):
        nk = s.shape[1]
        m_prev = m_ref[h, rows, :]
        m_new = jnp.maximum(m_prev, jnp.max(s, axis=-1, keepdims=True))
        p = jnp.concatenate([jnp.exp2(s[:, j * LANES:(j + 1) * LANES] - m_new)
                             for j in range(nk // LANES)], axis=1)
        pv = _dot(p.astype(BF16), va_ref[h, kb * tk:kb * tk + nk, :])
        acc_ref[h, rows, :] = jnp.exp2(m_prev - m_new) * acc_ref[h, rows, :] + pv
        m_ref[h, rows, :] = m_new

    def update(h, kb, buf, masked):
        if not masked:
            softmax_pv(h, kb, s_ref[buf, h], slice(0, tq))
            return
        for rows, nk in spans:
            s = s_ref[buf, h, rows, 0:nk]
            softmax_pv(h, kb, jnp.where(causal[rows, 0:nk], s, -1e30), rows)

    def write_out():
        for pair in range(FOX_HEADS // 2):
            a = acc_ref[2 * pair]
            b = acc_ref[2 * pair + 1]
            o = jnp.where(lane < FOX_HEAD_DIM, a / a[:, FOX_HEAD_DIM:FOX_HEAD_DIM + 1], b / b[:, 0:1])
            o_ref[:, pair * LANES:(pair + 1) * LANES] = o.astype(BF16)

    start = [0]
    for t in range(1, n_q):
        start.append(1 - (start[t - 1] + t - 1) % 2)
    for t in range(n_q):
        @pl.when(qi == t)
        def _(t=t):
            if t == 0:
                load_queries(0)
                reset_state()
                for h in range(FOX_HEADS):
                    logits(h, 0, start[0], diag=True)
            for j in range(t):
                for h in range(FOX_HEADS):
                    logits(h, j + 1, (start[t] + j + 1) % 2, diag=(j + 1 == t))
                    update(h, j, (start[t] + j) % 2, False)
            if t + 1 < n_q:
                load_queries(t + 1)
            for h in range(FOX_HEADS):
                update(h, t, (start[t] + t) % 2, True)
                if t + 1 < n_q:
                    logits(h, 0, start[t + 1])
            write_out()
            if t + 1 < n_q:
                reset_state()


def _fox(q, ka, va):
    bsz, seq, _ = q.shape
    tq = FOX_Q
    state = pltpu.VMEM((FOX_HEADS, tq, LANES), F32)
    full_spec = pl.BlockSpec((None, FOX_HEADS, seq, LANES), lambda b, i: (b, 0, 0, 0))
    return pl.pallas_call(
        _fox_kernel,
        grid=(bsz, seq // tq),
        in_specs=[pl.BlockSpec((None, seq, MIX_W), lambda b, i: (b, 0, 0)), full_spec, full_spec],
        out_specs=pl.BlockSpec((None, tq, MIX_W), lambda b, i: (b, i, 0)),
        out_shape=jax.ShapeDtypeStruct((bsz, seq, MIX_W), BF16),
        scratch_shapes=[pltpu.VMEM((FOX_HEADS, tq, LANES), BF16), state, state,
                        pltpu.VMEM((2, FOX_HEADS, tq, FOX_KV), F32)],
        compiler_params=_params(("arbitrary", "arbitrary")),
        name="fox_attn",
    )(q, ka, va)


def _s5_prep_kernel(lre_ref, lim_ref, ls_ref, bre_ref, bim_ref, are_ref, aim_ref, bbr_ref, bbi_ref):
    lr, li = lre_ref[...], lim_ref[...]
    dt = jnp.exp(ls_ref[...])
    mag = jnp.exp(lr * dt)
    ar = mag * jnp.cos(li * dt)
    ai = mag * jnp.sin(li * dt)
    den = lr * lr + li * li
    cr = ((ar - 1.0) * lr + ai * li) / den
    ci = (ai * lr - (ar - 1.0) * li) / den
    br, bi = bre_ref[...], bim_ref[...]
    bbr_ref[...] = cr * br - ci * bi
    bbi_ref[...] = cr * bi + ci * br
    are_ref[...] = jnp.broadcast_to(ar, are_ref.shape)
    aim_ref[...] = jnp.broadcast_to(ai, aim_ref.shape)


def _s5_prep(lam_re, lam_im, log_step, b_re, b_im):
    depth = lam_re.shape[0]
    row = lambda a: a.reshape(depth, 1, S5_NSTATE)
    ls = jnp.repeat(log_step, S5_STATE, axis=1).reshape(depth, 1, S5_NSTATE)
    to_hp = lambda b: b.transpose(0, 3, 1, 2).reshape(depth, S5_GROUP_CH, S5_NSTATE)
    spec = lambda r: pl.BlockSpec((None, r, S5_NSTATE), lambda l: (l, 0, 0))
    vec = jax.ShapeDtypeStruct((depth, 8, S5_NSTATE), F32)
    mat = jax.ShapeDtypeStruct((depth, S5_GROUP_CH, S5_NSTATE), F32)
    return pl.pallas_call(
        _s5_prep_kernel,
        grid=(depth,),
        in_specs=[spec(1), spec(1), spec(1), spec(S5_GROUP_CH), spec(S5_GROUP_CH)],
        out_specs=[spec(8), spec(8), spec(S5_GROUP_CH), spec(S5_GROUP_CH)],
        out_shape=[vec, vec, mat, mat],
        name="s5_prep",
    )(row(lam_re), row(lam_im), ls, to_hp(b_re), to_hp(b_im))


def _s5_kernel(u_ref, are_ref, aim_ref, wb_ref, wc_ref, d_ref, wglu_ref, o_ref, x_ref, st_ref, y_ref):
    @pl.when(pl.program_id(0) == 0)
    def _():
        st_ref[...] = jnp.zeros_like(st_ref)

    n = S5_NSTATE
    n_tiles = 2 * n // MXU_DIM
    total_rows = u_ref.shape[1]
    steps = total_rows // 8
    width = 1024

    def load_u(rows):
        return jnp.concatenate([u_ref[s, rows, :] for s in range(MIX_W // LANES)], axis=1)

    def project_in(rows):
        ub = load_u(rows).astype(BF16)
        for j in range(n_tiles):
            kh = ((j % (n_tiles // 2)) * MXU_DIM // S5_STATE * S5_GROUP_CH) // MXU_DIM
            x_ref[rows, j * MXU_DIM:(j + 1) * MXU_DIM] = _dot(ub[:, kh * MXU_DIM:(kh + 1) * MXU_DIM], wb_ref[j])

    def scan(t0, t1, state):
        for i, c0 in enumerate(range(0, n, width)):
            re_sl = slice(c0, c0 + width)
            im_sl = slice(n + c0, n + c0 + width)
            ar = are_ref[:, re_sl]
            ai = aim_ref[:, re_sl]
            xr, xi = state[i]
            for t in range(t0, t1):
                rows = slice(t * 8, t * 8 + 8)
                xr, xi = (ar * xr - ai * xi + x_ref[rows, re_sl], ar * xi + ai * xr + x_ref[rows, im_sl])
                x_ref[rows, re_sl] = xr
                x_ref[rows, im_sl] = xi
            state[i] = (xr, xi)

    def project_out(rows):
        halves = []
        per_half = n // MXU_DIM // 2
        for nh in range(MIX_W // MXU_DIM):
            acc_re = None
            acc_im = None
            for kk in range(per_half):
                kr = nh * per_half + kk
                ki = n // MXU_DIM + kr
                pr = _dot(x_ref[rows, kr * MXU_DIM:(kr + 1) * MXU_DIM].astype(BF16), wc_ref[kr])
                pi = _dot(x_ref[rows, ki * MXU_DIM:(ki + 1) * MXU_DIM].astype(BF16), wc_ref[ki])
                acc_re = pr if acc_re is None else acc_re + pr
                acc_im = pi if acc_im is None else acc_im + pi
            halves.append(acc_re - acc_im)
        y = jnp.concatenate(halves, axis=1) + d_ref[...] * load_u(rows)
        y = 0.5 * y * (1.0 + jnp.tanh(math.sqrt(2.0 / math.pi) * (y + 0.044715 * (y * y * y))))
        y = y * _sigmoid(_dot(y.astype(BF16), wglu_ref[...]))
        for s in range(MIX_W // LANES):
            y_ref[s, rows, :] = y[:, s * LANES:(s + 1) * LANES]

    phase_rows = total_rows // S5_PHASES
    phases = [slice(p * phase_rows, (p + 1) * phase_rows) for p in range(S5_PHASES)]
    for rows in phases:
        project_in(rows)
    state = [(st_ref[:, c0:c0 + width], st_ref[:, n + c0:n + c0 + width]) for c0 in range(0, n, width)]
    for p, rows in enumerate(phases):
        scan(p * steps // S5_PHASES, (p + 1) * steps // S5_PHASES, state)
        project_out(rows)
    for i, c0 in enumerate(range(0, n, width)):
        st_ref[:, c0:c0 + width] = state[i][0]
        st_ref[:, n + c0:n + c0 + width] = state[i][1]
    bsz = o_ref.shape[0]
    for b in range(bsz):
        o_ref[b] = jnp.concatenate([y_ref[s, pl.ds(b, steps, stride=bsz), :]
                                    for s in range(MIX_W // LANES)], axis=1).astype(BF16)


def _s5(u_tm, a_re, a_im, wb, wc, d, wglu, bsz, layer):
    rows = S5_STEPS * bsz
    total = u_tm.shape[1]
    return pl.pallas_call(
        _s5_kernel,
        grid=(total // rows,),
        in_specs=[
            pl.BlockSpec((MIX_W // LANES, rows, LANES), lambda i: (0, i, 0)),
            _layer_spec(a_re, layer), _layer_spec(a_im, layer),
            _layer_spec(wb, layer), _layer_spec(wc, layer),
            _layer_spec(d, layer), _layer_spec(wglu, layer),
        ],
        out_specs=pl.BlockSpec((bsz, S5_STEPS, MIX_W), lambda i: (0, i, 0)),
        out_shape=jax.ShapeDtypeStruct((bsz, total // bsz, MIX_W), BF16),
        scratch_shapes=[pltpu.VMEM((rows, 2 * S5_NSTATE), F32), pltpu.VMEM((8, 2 * S5_NSTATE), F32),
                        pltpu.VMEM((MIX_W // LANES, rows, LANES), F32)],
        compiler_params=_params(("arbitrary",)),
        name="s5",
    )(u_tm, a_re, a_im, wb, wc, d, wglu)


def _s5_weights(bbar_re, bbar_im, c_re, c_im):
    half = S5_NSTATE // MXU_DIM
    g_per_tile = MXU_DIM // S5_STATE
    g_per_slab = MXU_DIM // S5_GROUP_CH
    depth = bbar_re.shape[0]
    t = jnp.arange(half)[:, None, None]
    r = jnp.arange(MXU_DIM)[None, :, None]
    c = jnp.arange(MXU_DIM)[None, None, :]
    slab = (t * g_per_tile) // g_per_slab
    b_keep = slab * g_per_slab + r // S5_GROUP_CH == t * g_per_tile + c // S5_STATE
    c_keep = t * g_per_tile + r // S5_STATE == slab * g_per_slab + c // S5_GROUP_CH

    def b_tiles(bb):
        bb = bb.reshape(depth, S5_GROUP_CH, half, MXU_DIM).transpose(0, 2, 1, 3)
        return jnp.where(b_keep, jnp.tile(bb, (1, 1, g_per_slab, 1)), 0.0)

    def c_tiles(cc):
        cc = cc.transpose(0, 1, 3, 2).reshape(depth, half, MXU_DIM, S5_GROUP_CH)
        return jnp.where(c_keep, jnp.tile(cc, (1, 1, 1, g_per_slab)), 0.0)

    wb = jnp.concatenate([b_tiles(bbar_re), b_tiles(bbar_im)], axis=1).astype(BF16)
    wc = jnp.concatenate([c_tiles(c_re), c_tiles(c_im)], axis=1).astype(BF16)
    return wb, wc


def _merge_kernel(alpha, h_ref, ya_ref, yb_ref, yc_ref, wg_ref, wa_ref, wb_ref, wc_ref, wo_ref,
                  g_ref, b_ref, o_ref):
    for r0 in range(0, h_ref.shape[0], ROW_SUB):
        rows = slice(r0, r0 + ROW_SUB)
        h = h_ref[rows, :]
        hb = h.astype(BF16)
        merged = None
        for i, (y_ref, w_ref) in enumerate(((ya_ref, wa_ref), (yb_ref, wb_ref), (yc_ref, wc_ref))):
            gate = _sigmoid(_dot(hb, wg_ref[:, i * D_MODEL:(i + 1) * D_MODEL]))
            term = gate * _dot(y_ref[rows, :], w_ref[...])
            merged = term if merged is None else merged + term
        mix = _dot(merged.astype(BF16), wo_ref[...])
        o_ref[rows, :] = _layer_norm(alpha * h + mix, g_ref[...], b_ref[...])


def _merge(h, ya, yb, yc, wg, wa, wb, wc, wo, g, b, alpha, layer):
    bsz, seq, _ = h.shape
    tm = MERGE_ROWS
    row_spec = lambda w: pl.BlockSpec((None, tm, w), lambda bb, i: (bb, i, 0))
    return pl.pallas_call(
        functools.partial(_merge_kernel, alpha),
        grid=(bsz, seq // tm),
        in_specs=[
            row_spec(D_MODEL), row_spec(MIX_W), row_spec(MIX_W), row_spec(MIX_W),
            _layer_spec(wg, layer), _layer_spec(wa, layer), _layer_spec(wb, layer),
            _layer_spec(wc, layer), _layer_spec(wo, layer), _layer_spec(g, layer), _layer_spec(b, layer),
        ],
        out_specs=row_spec(D_MODEL),
        out_shape=jax.ShapeDtypeStruct(h.shape, F32),
        compiler_params=_params(("arbitrary", "arbitrary")),
        name="merge_ln",
    )(h, ya, yb, yc, wg, wa, wb, wc, wo, g, b)


def _ffn_kernel(alpha, h_ref, wg_ref, wu_ref, wd_ref, g_ref, b_ref, o_ref):
    for r0 in range(0, h_ref.shape[0], ROW_SUB):
        rows = slice(r0, r0 + ROW_SUB)
        h = h_ref[rows, :]
        hb = h.astype(BF16)
        acc = None
        for c0 in range(0, FFN_HIDDEN, FFN_CHUNK):
            sl = slice(c0, min(c0 + FFN_CHUNK, FFN_HIDDEN))
            a = _dot(hb, wg_ref[:, sl])
            hid = (a * _sigmoid(a)) * _dot(hb, wu_ref[:, sl])
            part = _dot(hid.astype(BF16), wd_ref[sl, :])
            acc = part if acc is None else acc + part
        o_ref[rows, :] = _layer_norm(alpha * h + acc, g_ref[...], b_ref[...])


def _ffn(h, wg, wu, wd, g, b, alpha, layer):
    bsz, seq, _ = h.shape
    tm = FFN_ROWS
    row_spec = pl.BlockSpec((None, tm, D_MODEL), lambda bb, i: (bb, i, 0))
    return pl.pallas_call(
        functools.partial(_ffn_kernel, alpha),
        grid=(bsz, seq // tm),
        in_specs=[row_spec, _layer_spec(wg, layer), _layer_spec(wu, layer), _layer_spec(wd, layer),
                  _layer_spec(g, layer), _layer_spec(b, layer)],
        out_specs=row_spec,
        out_shape=jax.ShapeDtypeStruct(h.shape, F32),
        compiler_params=_params(("arbitrary", "arbitrary")),
        name="ffn_ln",
    )(h, wg, wu, wd, g, b)


def kernel(x, w_in, hg_lower_bounds, hg_norm_w, fox_b_f, s5_lambda_re, s5_lambda_im, s5_log_step,
           s5_b_re, s5_b_im, s5_c_re, s5_c_im, s5_d, s5_w_glu, w_br_a, w_br_b, w_br_c, w_out,
           ln1_g, ln1_b, w_ffn_gate, w_ffn_up, w_ffn_down, ln2_g, ln2_b):
    depth = w_in.shape[0]
    bsz, seq, _ = x.shape
    alpha = (2 * depth) ** 0.25
    o_s5 = FOX_COL0 + 3 * MIX_W + FOX_HEADS
    o_gate = o_s5 + MIX_W

    bf = lambda a: a.astype(BF16)
    vec = lambda a: a.reshape(depth, 1, -1)
    w_main = bf(w_in)
    w_s5, w_gate = w_main[:, :, o_s5:o_gate], w_main[:, :, o_gate:]
    late_f32 = (s5_w_glu, w_br_a, w_br_b, w_br_c, w_out, w_ffn_gate, w_ffn_up, w_ffn_down)
    late_rows = tuple(w.reshape(-1, w.shape[-1]) for w in late_f32)
    fox_bias = jnp.pad(fox_b_f, ((0, 0), (0, LANES - FOX_HEADS))).reshape(depth, 1, LANES)
    a_re, a_im, bbar_re, bbar_im = _s5_prep(s5_lambda_re, s5_lambda_im, s5_log_step, s5_b_re, s5_b_im)
    wb, wc = _s5_weights(bbar_re, bbar_im, s5_c_re, s5_c_im)
    norm_w, s5_skip = vec(hg_norm_w), vec(s5_d)
    g1, b1, g2, b2 = vec(ln1_g), vec(ln1_b), vec(ln2_g), vec(ln2_b)

    h = x
    for l in range(depth):
        q, ka, va, u_tm, *late_b = _in_proj(h, w_main, fox_bias, w_s5, l, late_rows if l == 0 else ())
        if l == 0:
            w_glu_b, w_a, w_b, w_c, w_o, w_fg, w_fu, w_fd = (
                w16.reshape(w32.shape) for w16, w32 in zip(late_b, late_f32))
        ya = _hgrn2(h, w_main, hg_lower_bounds, norm_w, l)
        yb = _fox(q, ka, va)
        yc = _s5(u_tm, a_re, a_im, wb, wc, s5_skip, w_glu_b, bsz, l)
        h = _merge(h, ya, yb, yc, w_gate, w_a, w_b, w_c, w_o, g1, b1, alpha, l)
        h = _ffn(h, w_fg, w_fu, w_fd, g2, b2, alpha, l)
    return h
```

```python
import functools
import math

import jax
import jax.numpy as jnp
from jax import lax
from jax.experimental import pallas as pl
from jax.experimental.pallas import tpu as pltpu

F32 = jnp.float32
BF16 = jnp.bfloat16

D_MODEL = 1024
MIX_W = 512
HG_HEADS = 4
HG_KEY = 128
HG_CHUNK = 64
HG_REF_ROW = HG_CHUNK // 2 - 1
FOX_HEADS = 8
FOX_COL0 = 4 * MIX_W
FOX_HEAD_DIM = 64
S5_GROUPS = 32
S5_STATE = 64
S5_GROUP_CH = 16
S5_NSTATE = S5_GROUPS * S5_STATE
FFN_HIDDEN = 2816
LN_EPS = 1e-5
RMS_EPS = 1e-6
EXP2_CLAMP = 115.0
LOG2E = math.log2(math.e)

LANES = 128
MXU_DIM = 256
VMEM_LIMIT = 56 * 1024 * 1024

PROJ_ROWS = 1024
PROJ_GROUP = 2
HG_ROWS = 256
FOX_Q = 512
FOX_KV = 512
S5_PHASES = 2
S5_STEPS = 128
MERGE_ROWS = 1024
FFN_ROWS = 1024
ROW_SUB = 256
FFN_CHUNK = 6 * MXU_DIM


def _dot(a, b):
    return jnp.dot(a, b, preferred_element_type=F32)


def _dot_nt(a, b):
    return lax.dot_general(a, b, (((1,), (1,)), ((), ())), preferred_element_type=F32)


def _dot_tn(a, b):
    return lax.dot_general(a, b, (((0,), (0,)), ((), ())), preferred_element_type=F32)


def _sigmoid(x):
    return 1.0 / (1.0 + jnp.exp2(x * -LOG2E))


def _layer_norm(z, g, b):
    mu = jnp.mean(z, axis=-1, keepdims=True)
    zc = z - mu
    var = jnp.mean(zc * zc, axis=-1, keepdims=True)
    return zc * lax.rsqrt(var + LN_EPS) * g + b


def _cumsum(x, axis):
    n = x.shape[axis]
    idx = lax.broadcasted_iota(jnp.int32, x.shape, axis)
    d = 1
    while d < n:
        x = x + jnp.where(idx >= d, pltpu.roll(x, d, axis=axis), 0.0)
        d *= 2
    return x


def _layer_spec(w, layer):
    zeros = (0,) * (w.ndim - 1)
    return pl.BlockSpec((None,) + w.shape[1:], lambda *_: (layer,) + zeros, pipeline_mode=pl.Buffered(1))


def _params(sem):
    return pltpu.CompilerParams(dimension_semantics=sem, vmem_limit_bytes=VMEM_LIMIT)


def _fox_lanes(h, lane):
    if h % 2 == 0:
        return lane < FOX_HEAD_DIM, FOX_HEAD_DIM
    return lane >= FOX_HEAD_DIM, 0


def _proj_kernel(n_cast, x_ref, wq_ref, wk_ref, wv_ref, wff_ref, bf_ref, ws5_ref, *refs):
    cast_in = refs[:n_cast]
    q_ref, ka_ref, va_ref, s5_ref = refs[n_cast:n_cast + 4]
    cast_out = refs[n_cast + 4:2 * n_cast + 4]
    cum_ref = refs[2 * n_cast + 4]
    for src, dst in zip(cast_in, cast_out):
        dst[...] = src[...].astype(BF16)
    bsz, tt, _ = x_ref.shape

    @pl.when(pl.program_id(0) == 0)
    def _():
        cum_ref[...] = jnp.zeros_like(cum_ref)

    lane = lax.broadcasted_iota(jnp.int32, (1, LANES), 1)
    for b0 in range(0, bsz, PROJ_GROUP):
        nb = PROJ_GROUP
        rows = nb * tt
        xb = x_ref[b0:b0 + nb].reshape(rows, D_MODEL).astype(BF16)
        k = _dot(xb, wk_ref[...])
        v = _dot(xb, wv_ref[...])
        z = _dot(xb, wff_ref[...]) + bf_ref[...]
        q_ref[b0:b0 + nb] = _dot(xb, wq_ref[...]).astype(BF16).reshape(nb, tt, MIX_W)
        u = _dot(xb, ws5_ref[...])

        logsig = jnp.minimum(z, 0.0) - jnp.log(1.0 + jnp.exp(-jnp.abs(z)))
        t_idx = lax.broadcasted_iota(jnp.int32, (rows, LANES), 0) % tt
        cum = logsig
        d = 1
        while d < tt:
            cum = cum + jnp.where(t_idx >= d, pltpu.roll(cum, d, axis=0), 0.0)
            d *= 2
        carry = cum_ref[b0:b0 + nb, :]
        cum = jnp.concatenate([cum[b * tt:(b + 1) * tt] + carry[b:b + 1] for b in range(nb)], axis=0)
        cum_ref[b0:b0 + nb, :] = jnp.concatenate([cum[(b + 1) * tt - 1:(b + 1) * tt] for b in range(nb)], axis=0)

        bias = cum * (-LOG2E)
        for h in range(FOX_HEADS):
            sl = slice((h // 2) * LANES, (h // 2 + 1) * LANES)
            own, aux = _fox_lanes(h, lane)
            bh = jnp.broadcast_to(bias[:, h:h + 1], (rows, LANES))
            hi = bh.astype(BF16).astype(F32)
            mid = (bh - hi).astype(BF16).astype(F32)
            lo = bh - hi - mid
            extra = jnp.where(lane == aux, hi, jnp.where(lane == aux + 1, mid,
                                                         jnp.where(lane == aux + 2, lo, 0.0)))
            ka = jnp.where(own, k[:, sl], extra).astype(BF16)
            va = jnp.where(own, v[:, sl], jnp.where(lane == aux, 1.0, 0.0)).astype(BF16)
            for b in range(nb):
                ka_ref[b0 + b, h] = ka[b * tt:(b + 1) * tt]
                va_ref[b0 + b, h] = va[b * tt:(b + 1) * tt]

        for b in range(nb):
            for s in range(MIX_W // LANES):
                s5_ref[s, pl.ds(b0 + b, tt, stride=bsz), :] = u[b * tt:(b + 1) * tt, s * LANES:(s + 1) * LANES]


def _in_proj(h, w_all, bf, ws5, layer, cast=()):
    bsz, seq, _ = h.shape
    tt = PROJ_ROWS // bsz
    n_steps = seq // tt
    cast_specs = [pl.BlockSpec((w.shape[0] // n_steps, w.shape[1]), lambda i: (i, 0)) for w in cast]
    fox_spec = lambda j: pl.BlockSpec((None, D_MODEL, MIX_W), lambda i: (layer, 0, FOX_COL0 // MIX_W + j))
    head_spec = pl.BlockSpec((bsz, FOX_HEADS, tt, LANES), lambda i: (0, 0, i, 0))
    head_shape = jax.ShapeDtypeStruct((bsz, FOX_HEADS, seq, LANES), BF16)
    return pl.pallas_call(
        functools.partial(_proj_kernel, len(cast)),
        grid=(n_steps,),
        in_specs=[
            pl.BlockSpec((bsz, tt, D_MODEL), lambda i: (0, i, 0)),
            fox_spec(0), fox_spec(1), fox_spec(2),
            pl.BlockSpec((None, D_MODEL, LANES), lambda i: (layer, 0, (FOX_COL0 + 3 * MIX_W) // LANES)),
            pl.BlockSpec((None, 1, LANES), lambda i: (layer, 0, 0)),
            _layer_spec(ws5, layer),
        ] + cast_specs,
        out_specs=[
            pl.BlockSpec((bsz, tt, MIX_W), lambda i: (0, i, 0)),
            head_spec, head_spec,
            pl.BlockSpec((MIX_W // LANES, bsz * tt, LANES), lambda i: (0, i, 0)),
        ] + cast_specs,
        out_shape=[
            jax.ShapeDtypeStruct((bsz, seq, MIX_W), BF16),
            head_shape, head_shape,
            jax.ShapeDtypeStruct((MIX_W // LANES, seq * bsz, LANES), F32),
        ] + [jax.ShapeDtypeStruct(w.shape, BF16) for w in cast],
        scratch_shapes=[pltpu.VMEM((bsz, LANES), F32)],
        compiler_params=_params(("arbitrary",)),
        name="in_proj",
    )(h, w_all, w_all, w_all, w_all, bf, ws5, *cast)


def _hgrn2_kernel(layer, h_ref, hn_ref, w_ref, lbp_ref, nw_ref, o_ref, st_ref, hg0_ref, hg1_ref):
    i = pl.program_id(1)

    @pl.when(i == 0)
    def _():
        st_ref[...] = jnp.zeros_like(st_ref)
        hg0_ref[...] = _dot(h_ref[...].astype(BF16), w_ref[...])

    lbp = lbp_ref[...]
    e = jnp.exp(lbp - jnp.max(lbp, axis=0, keepdims=True))
    sm = e / jnp.sum(e, axis=0, keepdims=True)
    cum = sm[0:1]
    for j in range(1, layer + 1):
        cum = cum + sm[j:j + 1]
    lb = cum - sm[0:1]

    nw = nw_ref[...]
    c = HG_CHUNK
    n_chunks = h_ref.shape[0] // c
    tri = (lax.broadcasted_iota(jnp.int32, (c, c), 0) >= lax.broadcasted_iota(jnp.int32, (c, c), 1))

    def gates(ci, hg_ref):
        rows = slice(ci * c, (ci + 1) * c)
        q = hg_ref[rows, 0:MIX_W]
        fz = hg_ref[rows, MIX_W:2 * MIX_W]
        f = lb + (1.0 - lb) * _sigmoid(fz)
        k = 1.0 - f
        g = _cumsum(jnp.log2(f), 0)
        g_ref = g[HG_REF_ROW:HG_REF_ROW + 1]
        g_last = g[c - 1:c]
        d_ref = g - g_ref
        q_rel = (q * jnp.exp2(jnp.minimum(d_ref, EXP2_CLAMP))).astype(BF16)
        k_rel = (k * jnp.exp2(jnp.minimum(-d_ref, EXP2_CLAMP))).astype(BF16)
        q_in = (q * jnp.exp2(g)).astype(BF16)
        k_end = (k * jnp.exp2(g_last - g)).astype(BF16)
        s_decay = jnp.exp2(g_last)
        return q_rel, k_rel, q_in, k_end, s_decay

    def mix(ci, hg_ref, operands):
        rows = slice(ci * c, (ci + 1) * c)
        q_rel, k_rel, q_in, k_end, s_decay = operands
        v = hg_ref[rows, 2 * MIX_W:3 * MIX_W]
        gate = hg_ref[rows, 3 * MIX_W:4 * MIX_W]
        for h in range(HG_HEADS):
            sl = slice(h * HG_KEY, (h + 1) * HG_KEY)
            st = st_ref[h]
            v_t = v[:, sl].T.astype(BF16)
            scores = jnp.where(tri, _dot_nt(q_rel[:, sl], k_rel[:, sl]), 0.0)
            lhs = jnp.concatenate([q_in[:, sl], scores.astype(BF16)], axis=1)
            rhs_t = jnp.concatenate([st.astype(BF16), v_t], axis=1)
            o = _dot_nt(lhs, rhs_t)
            st_ref[h] = st * s_decay[:, sl] + _dot(v_t, k_end[:, sl])
            o = o * lax.rsqrt(jnp.mean(o * o, axis=-1, keepdims=True) + RMS_EPS) * nw
            gh = gate[:, sl]
            o_ref[rows, sl] = (o * (gh * _sigmoid(gh))).astype(BF16)

    def tile(cur_ref, nxt_ref):
        hn = hn_ref[...].astype(BF16)
        cols = w_ref.shape[1] // n_chunks
        operands = gates(0, cur_ref)
        for ci in range(n_chunks):
            ahead = gates(ci + 1, cur_ref) if ci + 1 < n_chunks else None
            mix(ci, cur_ref, operands)
            nxt_ref[:, ci * cols:(ci + 1) * cols] = _dot(hn, w_ref[:, ci * cols:(ci + 1) * cols])
            operands = ahead

    @pl.when(i % 2 == 0)
    def _():
        tile(hg0_ref, hg1_ref)

    @pl.when(i % 2 == 1)
    def _():
        tile(hg1_ref, hg0_ref)


def _hgrn2(h, w_all, lbp, nw, layer):
    bsz, seq, _ = h.shape
    tr = HG_ROWS
    n_tiles = seq // tr
    hg = pltpu.VMEM((tr, 4 * MIX_W), F32)
    return pl.pallas_call(
        functools.partial(_hgrn2_kernel, layer),
        grid=(bsz, n_tiles),
        in_specs=[
            pl.BlockSpec((None, tr, D_MODEL), lambda b, i: (b, i, 0)),
            pl.BlockSpec((None, tr, D_MODEL), lambda b, i: (b, jnp.minimum(i + 1, n_tiles - 1), 0)),
            pl.BlockSpec((None, D_MODEL, 4 * MIX_W), lambda b, i: (layer, 0, 0)),
            pl.BlockSpec(lbp.shape, lambda b, i: (0, 0)),
            _layer_spec(nw, layer),
        ],
        out_specs=pl.BlockSpec((None, tr, MIX_W), lambda b, i: (b, i, 0)),
        out_shape=jax.ShapeDtypeStruct((bsz, seq, MIX_W), BF16),
        scratch_shapes=[pltpu.VMEM((HG_HEADS, HG_KEY, HG_KEY), F32), hg, hg],
        compiler_params=_params(("arbitrary", "arbitrary")),
        name="hgrn2",
    )(h, h, w_all, lbp, nw)


def _fox_kernel(q_ref, ka_ref, va_ref, o_ref, qa_ref, m_ref, acc_ref, s_ref):
    qi = pl.program_id(1)
    tq, tk = FOX_Q, FOX_KV
    n_q = q_ref.shape[0] // tq
    lane = lax.broadcasted_iota(jnp.int32, (1, LANES), 1)
    causal = (lax.broadcasted_iota(jnp.int32, (tq, tk), 0) >= lax.broadcasted_iota(jnp.int32, (tq, tk), 1))

    def load_queries(t):
        for h in range(FOX_HEADS):
            sl = slice((h // 2) * LANES, (h // 2 + 1) * LANES)
            own, aux = _fox_lanes(h, lane)
            ones = jnp.where((lane >= aux) & (lane < aux + 3), 1.0, 0.0)
            q = q_ref[t * tq:(t + 1) * tq, sl].astype(F32) * (FOX_HEAD_DIM ** -0.5 * LOG2E)
            qa_ref[h] = jnp.where(own, q, ones).astype(BF16)

    def reset_state():
        m_ref[...] = jnp.full(m_ref.shape, -1e30, F32)
        acc_ref[...] = jnp.zeros(acc_ref.shape, F32)

    hq = tq // 2
    spans = ((slice(0, hq), tk // 2), (slice(hq, tq), tk))

    def logits(h, kb, buf, diag=False):
        if not diag:
            s_ref[buf, h] = _dot_nt(qa_ref[h], ka_ref[h, kb * tk:(kb + 1) * tk, :])
            return
        for rows, nk in spans:
            s_ref[buf, h, rows, 0:nk] = _dot_nt(qa_ref[h, rows, :], ka_ref[h, kb * tk:kb * tk + nk, :])

    def softmax_pv(h, kb, s, rows):
        nk = s.shape[1]
        m_prev = m_ref[h, rows, :]
        m_new = jnp.maximum(m_prev, jnp.max(s, axis=-1, keepdims=True))
        p = jnp.concatenate([jnp.exp2(s[:, j * LANES:(j + 1) * LANES] - m_new)
                             for j in range(nk // LANES)], axis=1)
        pv = _dot(p.astype(BF16), va_ref[h, kb * tk:kb * tk + nk, :])
        acc_ref[h, rows, :] = jnp.exp2(m_prev - m_new) * acc_ref[h, rows, :] + pv
        m_ref[h, rows, :] = m_new

    def update(h, kb, buf, masked):
        if not masked:
            softmax_pv(h, kb, s_ref[buf, h], slice(0, tq))
            return
        for rows, nk in spans:
            s = s_ref[buf, h, rows, 0:nk]
            softmax_pv(h, kb, jnp.where(causal[rows, 0:nk], s, -1e30), rows)

    def write_out():
        for pair in range(FOX_HEADS // 2):
            a = acc_ref[2 * pair]
            b = acc_ref[2 * pair + 1]
            o = jnp.where(lane < FOX_HEAD_DIM, a / a[:, FOX_HEAD_DIM:FOX_HEAD_DIM + 1], b / b[:, 0:1])
            o_ref[:, pair * LANES:(pair + 1) * LANES] = o.astype(BF16)

    start = [0]
    for t in range(1, n_q):
        start.append(1 - (start[t - 1] + t - 1) % 2)
    for t in range(n_q):
        @pl.when(qi == t)
        def _(t=t):
            if t == 0:
                load_queries(0)
                reset_state()
                for h in range(FOX_HEADS):
                    logits(h, 0, start[0], diag=True)
            for j in range(t):
                for h in range(FOX_HEADS):
                    logits(h, j + 1, (start[t] + j + 1) % 2, diag=(j + 1 == t))
                    update(h, j, (start[t] + j) % 2, False)
            if t + 1 < n_q:
                load_queries(t + 1)
            for h in range(FOX_HEADS):
                update(h, t, (start[t] + t) % 2, True)
                if t + 1 < n_q:
                    logits(h, 0, start[t + 1])
            write_out()
            if t + 1 < n_q:
                reset_state()


def _fox(q, ka, va):
    bsz, seq, _ = q.shape
    tq = FOX_Q
    state = pltpu.VMEM((FOX_HEADS, tq, LANES), F32)
    full_spec = pl.BlockSpec((None, FOX_HEADS, seq, LANES), lambda b, i: (b, 0, 0, 0))
    return pl.pallas_call(
        _fox_kernel,
        grid=(bsz, seq // tq),
        in_specs=[pl.BlockSpec((None, seq, MIX_W), lambda b, i: (b, 0, 0)), full_spec, full_spec],
        out_specs=pl.BlockSpec((None, tq, MIX_W), lambda b, i: (b, i, 0)),
        out_shape=jax.ShapeDtypeStruct((bsz, seq, MIX_W), BF16),
        scratch_shapes=[pltpu.VMEM((FOX_HEADS, tq, LANES), BF16), state, state,
                        pltpu.VMEM((2, FOX_HEADS, tq, FOX_KV), F32)],
        compiler_params=_params(("arbitrary", "arbitrary")),
        name="fox_attn",
    )(q, ka, va)


def _s5_prep_kernel(lre_ref, lim_ref, ls_ref, bre_ref, bim_ref, are_ref, aim_ref, bbr_ref, bbi_ref):
    lr, li = lre_ref[...], lim_ref[...]
    dt = jnp.exp(ls_ref[...])
    mag = jnp.exp(lr * dt)
    ar = mag * jnp.cos(li * dt)
    ai = mag * jnp.sin(li * dt)
    den = lr * lr + li * li
    cr = ((ar - 1.0) * lr + ai * li) / den
    ci = (ai * lr - (ar - 1.0) * li) / den
    br, bi = bre_ref[...], bim_ref[...]
    bbr_ref[...] = cr * br - ci * bi
    bbi_ref[...] = cr * bi + ci * br
    are_ref[...] = jnp.broadcast_to(ar, are_ref.shape)
    aim_ref[...] = jnp.broadcast_to(ai, aim_ref.shape)


def _s5_prep(lam_re, lam_im, log_step, b_re, b_im):
    depth = lam_re.shape[0]
    row = lambda a: a.reshape(depth, 1, S5_NSTATE)
    ls = jnp.repeat(log_step, S5_STATE, axis=1).reshape(depth, 1, S5_NSTATE)
    to_hp = lambda b: b.transpose(0, 3, 1, 2).reshape(depth, S5_GROUP_CH, S5_NSTATE)
    spec = lambda r: pl.BlockSpec((None, r, S5_NSTATE), lambda l: (l, 0, 0))
    vec = jax.ShapeDtypeStruct((depth, 8, S5_NSTATE), F32)
    mat = jax.ShapeDtypeStruct((depth, S5_GROUP_CH, S5_NSTATE), F32)
    return pl.pallas_call(
        _s5_prep_kernel,
        grid=(depth,),
        in_specs=[spec(1), spec(1), spec(1), spec(S5_GROUP_CH), spec(S5_GROUP_CH)],
        out_specs=[spec(8), spec(8), spec(S5_GROUP_CH), spec(S5_GROUP_CH)],
        out_shape=[vec, vec, mat, mat],
        name="s5_prep",
    )(row(lam_re), row(lam_im), ls, to_hp(b_re), to_hp(b_im))


def _s5_kernel(u_ref, are_ref, aim_ref, wb_ref, wc_ref, d_ref, wglu_ref, o_ref, x_ref, st_ref, y_ref):
    @pl.when(pl.program_id(0) == 0)
    def _():
        st_ref[...] = jnp.zeros_like(st_ref)

    n = S5_NSTATE
    n_tiles = 2 * n // MXU_DIM
    total_rows = u_ref.shape[1]
    steps = total_rows // 8
    width = 1024

    def load_u(rows):
        return jnp.concatenate([u_ref[s, rows, :] for s in range(MIX_W // LANES)], axis=1)

    def project_in(rows):
        ub = load_u(rows).astype(BF16)
        for j in range(n_tiles):
            kh = ((j % (n_tiles // 2)) * MXU_DIM // S5_STATE * S5_GROUP_CH) // MXU_DIM
            x_ref[rows, j * MXU_DIM:(j + 1) * MXU_DIM] = _dot(ub[:, kh * MXU_DIM:(kh + 1) * MXU_DIM], wb_ref[j])

    def scan(t0, t1, state):
        for i, c0 in enumerate(range(0, n, width)):
            re_sl = slice(c0, c0 + width)
            im_sl = slice(n + c0, n + c0 + width)
            ar = are_ref[:, re_sl]
            ai = aim_ref[:, re_sl]
            xr, xi = state[i]
            for t in range(t0, t1):
                rows = slice(t * 8, t * 8 + 8)
                xr, xi = (ar * xr - ai * xi + x_ref[rows, re_sl], ar * xi + ai * xr + x_ref[rows, im_sl])
                x_ref[rows, re_sl] = xr
                x_ref[rows, im_sl] = xi
            state[i] = (xr, xi)

    def project_out(rows):
        halves = []
        per_half = n // MXU_DIM // 2
        for nh in range(MIX_W // MXU_DIM):
            acc_re = None
            acc_im = None
            for kk in range(per_half):
                kr = nh * per_half + kk
                ki = n // MXU_DIM + kr
                pr = _dot(x_ref[rows, kr * MXU_DIM:(kr + 1) * MXU_DIM].astype(BF16), wc_ref[kr])
                pi = _dot(x_ref[rows, ki * MXU_DIM:(ki + 1) * MXU_DIM].astype(BF16), wc_ref[ki])
                acc_re = pr if acc_re is None else acc_re + pr
                acc_im = pi if acc_im is None else acc_im + pi
            halves.append(acc_re - acc_im)
        y = jnp.concatenate(halves, axis=1) + d_ref[...] * load_u(rows)
        y = 0.5 * y * (1.0 + jnp.tanh(math.sqrt(2.0 / math.pi) * (y + 0.044715 * (y * y * y))))
        y = y * _sigmoid(_dot(y.astype(BF16), wglu_ref[...]))
        for s in range(MIX_W // LANES):
            y_ref[s, rows, :] = y[:, s * LANES:(s + 1) * LANES]

    phase_rows = total_rows // S5_PHASES
    phases = [slice(p * phase_rows, (p + 1) * phase_rows) for p in range(S5_PHASES)]
    for rows in phases:
        project_in(rows)
    state = [(st_ref[:, c0:c0 + width], st_ref[:, n + c0:n + c0 + width]) for c0 in range(0, n, width)]
    for p, rows in enumerate(phases):
        scan(p * steps // S5_PHASES, (p + 1) * steps // S5_PHASES, state)
        project_out(rows)
    for i, c0 in enumerate(range(0, n, width)):
        st_ref[:, c0:c0 + width] = state[i][0]
        st_ref[:, n + c0:n + c0 + width] = state[i][1]
    bsz = o_ref.shape[0]
    for b in range(bsz):
        o_ref[b] = jnp.concatenate([y_ref[s, pl.ds(b, steps, stride=bsz), :]
                                    for s in range(MIX_W // LANES)], axis=1).astype(BF16)


def _s5(u_tm, a_re, a_im, wb, wc, d, wglu, bsz, layer):
    rows = S5_STEPS * bsz
    total = u_tm.shape[1]
    return pl.pallas_call(
        _s5_kernel,
        grid=(total // rows,),
        in_specs=[
            pl.BlockSpec((MIX_W // LANES, rows, LANES), lambda i: (0, i, 0)),
            _layer_spec(a_re, layer), _layer_spec(a_im, layer),
            _layer_spec(wb, layer), _layer_spec(wc, layer),
            _layer_spec(d, layer), _layer_spec(wglu, layer),
        ],
        out_specs=pl.BlockSpec((bsz, S5_STEPS, MIX_W), lambda i: (0, i, 0)),
        out_shape=jax.ShapeDtypeStruct((bsz, total // bsz, MIX_W), BF16),
        scratch_shapes=[pltpu.VMEM((rows, 2 * S5_NSTATE), F32), pltpu.VMEM((8, 2 * S5_NSTATE), F32),
                        pltpu.VMEM((MIX_W // LANES, rows, LANES), F32)],
        compiler_params=_params(("arbitrary",)),
        name="s5",
    )(u_tm, a_re, a_im, wb, wc, d, wglu)


def _s5_weights(bbar_re, bbar_im, c_re, c_im):
    half = S5_NSTATE // MXU_DIM
    g_per_tile = MXU_DIM // S5_STATE
    g_per_slab = MXU_DIM // S5_GROUP_CH
    depth = bbar_re.shape[0]
    t = jnp.arange(half)[:, None, None]
    r = jnp.arange(MXU_DIM)[None, :, None]
    c = jnp.arange(MXU_DIM)[None, None, :]
    slab = (t * g_per_tile) // g_per_slab
    b_keep = slab * g_per_slab + r // S5_GROUP_CH == t * g_per_tile + c // S5_STATE
    c_keep = t * g_per_tile + r // S5_STATE == slab * g_per_slab + c // S5_GROUP_CH

    def b_tiles(bb):
        bb = bb.reshape(depth, S5_GROUP_CH, half, MXU_DIM).transpose(0, 2, 1, 3)
        return jnp.where(b_keep, jnp.tile(bb, (1, 1, g_per_slab, 1)), 0.0)

    def c_tiles(cc):
        cc = cc.transpose(0, 1, 3, 2).reshape(depth, half, MXU_DIM, S5_GROUP_CH)
        return jnp.where(c_keep, jnp.tile(cc, (1, 1, 1, g_per_slab)), 0.0)

    wb = jnp.concatenate([b_tiles(bbar_re), b_tiles(bbar_im)], axis=1).astype(BF16)
    wc = jnp.concatenate([c_tiles(c_re), c_tiles(c_im)], axis=1).astype(BF16)
    return wb, wc


def _merge_kernel(alpha, h_ref, ya_ref, yb_ref, yc_ref, wg_ref, wa_ref, wb_ref, wc_ref, wo_ref,
                  g_ref, b_ref, o_ref):
    for r0 in range(0, h_ref.shape[0], ROW_SUB):
        rows = slice(r0, r0 + ROW_SUB)
        h = h_ref[rows, :]
        hb = h.astype(BF16)
        merged = None
        for i, (y_ref, w_ref) in enumerate(((ya_ref, wa_ref), (yb_ref, wb_ref), (yc_ref, wc_ref))):
            gate = _sigmoid(_dot(hb, wg_ref[:, i * D_MODEL:(i + 1) * D_MODEL]))
            term = gate * _dot(y_ref[rows, :], w_ref[...])
            merged = term if merged is None else merged + term
        mix = _dot(merged.astype(BF16), wo_ref[...])
        o_ref[rows, :] = _layer_norm(alpha * h + mix, g_ref[...], b_ref[...])


def _merge(h, ya, yb, yc, wg, wa, wb, wc, wo, g, b, alpha, layer):
    bsz, seq, _ = h.shape
    tm = MERGE_ROWS
    row_spec = lambda w: pl.BlockSpec((None, tm, w), lambda bb, i: (bb, i, 0))
    return pl.pallas_call(
        functools.partial(_merge_kernel, alpha),
        grid=(bsz, seq // tm),
        in_specs=[
            row_spec(D_MODEL), row_spec(MIX_W), row_spec(MIX_W), row_spec(MIX_W),
            _layer_spec(wg, layer), _layer_spec(wa, layer), _layer_spec(wb, layer),
            _layer_spec(wc, layer), _layer_spec(wo, layer), _layer_spec(g, layer), _layer_spec(b, layer),
        ],
        out_specs=row_spec(D_MODEL),
        out_shape=jax.ShapeDtypeStruct(h.shape, F32),
        compiler_params=_params(("arbitrary", "arbitrary")),
        name="merge_ln",
    )(h, ya, yb, yc, wg, wa, wb, wc, wo, g, b)


def _ffn_kernel(alpha, h_ref, wg_ref, wu_ref, wd_ref, g_ref, b_ref, o_ref):
    for r0 in range(0, h_ref.shape[0], ROW_SUB):
        rows = slice(r0, r0 + ROW_SUB)
        h = h_ref[rows, :]
        hb = h.astype(BF16)
        acc = None
        for c0 in range(0, FFN_HIDDEN, FFN_CHUNK):
            sl = slice(c0, min(c0 + FFN_CHUNK, FFN_HIDDEN))
            a = _dot(hb, wg_ref[:, sl])
            hid = (a * _sigmoid(a)) * _dot(hb, wu_ref[:, sl])
            part = _dot(hid.astype(BF16), wd_ref[sl, :])
            acc = part if acc is None else acc + part
        o_ref[rows, :] = _layer_norm(alpha * h + acc, g_ref[...], b_ref[...])


def _ffn(h, wg, wu, wd, g, b, alpha, layer):
    bsz, seq, _ = h.shape
    tm = FFN_ROWS
    row_spec = pl.BlockSpec((None, tm, D_MODEL), lambda bb, i: (bb, i, 0))
    return pl.pallas_call(
        functools.partial(_ffn_kernel, alpha),
        grid=(bsz, seq // tm),
        in_specs=[row_spec, _layer_spec(wg, layer), _layer_spec(wu, layer), _layer_spec(wd, layer),
                  _layer_spec(g, layer), _layer_spec(b, layer)],
        out_specs=row_spec,
        out_shape=jax.ShapeDtypeStruct(h.shape, F32),
        compiler_params=_params(("arbitrary", "arbitrary")),
        name="ffn_ln",
    )(h, wg, wu, wd, g, b)


def kernel(x, w_in, hg_lower_bounds, hg_norm_w, fox_b_f, s5_lambda_re, s5_lambda_im, s5_log_step,
           s5_b_re, s5_b_im, s5_c_re, s5_c_im, s5_d, s5_w_glu, w_br_a, w_br_b, w_br_c, w_out,
           ln1_g, ln1_b, w_ffn_gate, w_ffn_up, w_ffn_down, ln2_g, ln2_b):
    depth = w_in.shape[0]
    bsz, seq, _ = x.shape
    alpha = (2 * depth) ** 0.25
    o_s5 = FOX_COL0 + 3 * MIX_W + FOX_HEADS
    o_gate = o_s5 + MIX_W

    bf = lambda a: a.astype(BF16)
    vec = lambda a: a.reshape(depth, 1, -1)
    w_main = bf(w_in)
    w_s5, w_gate = w_main[:, :, o_s5:o_gate], w_main[:, :, o_gate:]
    late_f32 = (s5_w_glu, w_br_a, w_br_b, w_br_c, w_out, w_ffn_gate, w_ffn_up, w_ffn_down)
    late_rows = tuple(w.reshape(-1, w.shape[-1]) for w in late_f32)
    fox_bias = jnp.pad(fox_b_f, ((0, 0), (0, LANES - FOX_HEADS))).reshape(depth, 1, LANES)
    a_re, a_im, bbar_re, bbar_im = _s5_prep(s5_lambda_re, s5_lambda_im, s5_log_step, s5_b_re, s5_b_im)
    wb, wc = _s5_weights(bbar_re, bbar_im, s5_c_re, s5_c_im)
    norm_w, s5_skip = vec(hg_norm_w), vec(s5_d)
    g1, b1, g2, b2 = vec(ln1_g), vec(ln1_b), vec(ln2_g), vec(ln2_b)

    h = x
    for l in range(depth):
        q, ka, va, u_tm, *late_b = _in_proj(h, w_main, fox_bias, w_s5, l, late_rows if l == 0 else ())
        if l == 0:
            w_glu_b, w_a, w_b, w_c, w_o, w_fg, w_fu, w_fd = (
                w16.reshape(w32.shape) for w16, w32 in zip(late_b, late_f32))
        ya = _hgrn2(h, w_main, hg_lower_bounds, norm_w, l)
        yb = _fox(q, ka, va)
        yc = _s5(u_tm, a_re, a_im, wb, wc, s5_skip, w_glu_b, bsz, l)
        h = _merge(h, ya, yb, yc, w_gate, w_a, w_b, w_c, w_o, g1, b1, alpha, l)
        h = _ffn(h, w_fg, w_fu, w_fd, g2, b2, alpha, l)
    return h
```
